```python
import jax, jax.numpy as jnp
from jax import lax
import numpy as np

D_MODEL = 4096
BATCH = 4
SEQ = 2048
DEPTH = 2
DEC_BATCH = 8
DEC_SEQ = 1
PAST_LEN = 16384
PAGE_SIZE = 128

N_A_LAYERS = DEPTH // 2
N_B_LAYERS = DEPTH - N_A_LAYERS
LRU_WIDTH = D_MODEL
LRU_BLOCKS = 16
LRU_BW = LRU_WIDTH // LRU_BLOCKS
CONV_W = 4
LRU_C = 8.0
HEAD_DIM = 128
N_HEADS = D_MODEL // HEAD_DIM
N_KV = 4
GQA = N_HEADS // N_KV
L_CMP = 32
D_CMP = 16
L_SEL = 64
N_SEL = 16
WINDOW = 512
WIN_QBLK = 128
SEL_QBLK = 32
D_FF = -((-8 * D_MODEL) // (3 * 256)) * 256
ALPHA = (2.0 * DEPTH) ** 0.25
BETA = (8.0 * DEPTH) ** -0.25
NEG = -1e30
FORCE_BONUS = 1e3
ATTN_SCALE = HEAD_DIM ** -0.5

kernel_name = 'yoco_rglru_nsa_decoder_step'


def layer_norm(x, g, b, eps=1e-5):
    xf = x.astype(jnp.float32)
    mu = xf.mean(-1, keepdims=True)
    var = jnp.square(xf - mu).mean(-1, keepdims=True)
    return ((xf - mu) * lax.rsqrt(var + eps) * g + b).astype(x.dtype)


def ada_modulation(c, w_ada, b_ada):
    m = jax.nn.silu(c) @ w_ada + b_ada
    return [t[:, None, :] for t in jnp.split(m, 6, axis=-1)]


def modulate(x, shift, scale):
    return x * (1.0 + scale) + shift


def post_norm_residual(x, f_out, gate, g, b):
    return layer_norm(ALPHA * x + (1.0 + gate) * f_out, g, b)


def swiglu(u, w_up, w_down):
    gt, v = jnp.split(u @ w_up, 2, axis=-1)
    return (jax.nn.silu(gt) * v) @ w_down


def alibi_slopes(n):
    return jnp.exp2(-8.0 * jnp.arange(1, n + 1, dtype=jnp.float32) / n)


def masked_softmax(s, mask):
    s = jnp.where(mask, s, NEG)
    m = jnp.max(s, axis=-1, keepdims=True)
    p = jnp.where(mask, jnp.exp(s - m), 0.0)
    return p / jnp.maximum(p.sum(-1, keepdims=True), 1e-30)


def rglru_block(u, conv_buf, h0, w_in, conv_w, conv_b, w_r, b_r, w_i, b_i, lam, w_out):
    B, S, _ = u.shape
    gx, rx = jnp.split(u @ w_in, 2, axis=-1)
    xp = jnp.concatenate([conv_buf.astype(rx.dtype), rx], axis=1)
    xc = conv_b + sum(xp[:, j:j + S] * conv_w[j] for j in range(CONV_W))
    xb = xc.reshape(B, S, LRU_BLOCKS, LRU_BW)
    r = jax.nn.sigmoid((jnp.einsum('bsnk,nkj->bsnj', xb, w_r).reshape(B, S, LRU_WIDTH) + b_r).astype(jnp.float32))
    i = jax.nn.sigmoid((jnp.einsum('bsnk,nkj->bsnj', xb, w_i).reshape(B, S, LRU_WIDTH) + b_i).astype(jnp.float32))
    log_a = -LRU_C * r * jax.nn.softplus(-lam.astype(jnp.float32))
    a = jnp.exp(log_a)
    b_in = jnp.sqrt(-jnp.expm1(2.0 * log_a)) * (i * xc.astype(jnp.float32))

    def step(h, ab):
        h = ab[0] * h + ab[1]
        return h, h

    h_last, hs = lax.scan(step, h0.astype(jnp.float32), (a.swapaxes(0, 1), b_in.swapaxes(0, 1)))
    y = (jax.nn.gelu(gx) * hs.swapaxes(0, 1).astype(u.dtype)) @ w_out
    return y, xp[:, -(CONV_W - 1):], h_last.astype(u.dtype)


def project_shared_kv(z, w_kv):
    B, S, _ = z.shape
    kv = (z @ w_kv).reshape(B, S, 6, N_KV, HEAD_DIM)
    return [kv[:, :, j] for j in range(6)]


def chunk_sums(rows, w_cmp):
    B, T = rows.shape[:2]
    n = T // D_CMP
    rc = rows[:, :n * D_CMP].reshape(B, n, D_CMP, N_KV, HEAD_DIM)
    return (jnp.einsum('bnjhd,jhd->bnhd', rc, w_cmp[:D_CMP]),
            jnp.einsum('bnjhd,jhd->bnhd', rc, w_cmp[D_CMP:]))


def compressed_blocks(first, second):
    blocks = first[:, :-1] + second[:, 1:]
    ends = (jnp.arange(blocks.shape[1]) + 2) * D_CMP - 1
    return blocks, ends


def queries_and_gates(u, w_qg):
    B, S, _ = u.shape
    qg = u @ w_qg
    q = qg[..., :N_HEADS * HEAD_DIM].reshape(B, S, N_KV, GQA, HEAD_DIM)
    g = jax.nn.sigmoid(qg[..., N_HEADS * HEAD_DIM:].astype(jnp.float32)).reshape(B, S, N_KV, GQA, 3)
    return q, g


def cmp_attention(q, q_pos, k_c, v_c, ends, slopes):
    s = jnp.einsum('bqhgd,bnhd->bqhgn', q, k_c).astype(jnp.float32) * ATTN_SCALE
    dist = (q_pos[:, None] - ends[None, :]).astype(jnp.float32)
    s = s - slopes[None, None, :, :, None] * dist[None, :, None, None, :]
    p = masked_softmax(s, (dist >= 0)[None, :, None, None, :])
    o = jnp.einsum('bqhgn,bnhd->bqhgd', p.astype(v_c.dtype), v_c)
    return o, p


def select_blocks(p, q_pos, n_sel_blocks):
    nc = p.shape[-1]
    c_start = jnp.arange(nc) * D_CMP
    s_start = jnp.arange(n_sel_blocks) * L_SEL
    cover = (c_start[:, None] < s_start[None, :] + L_SEL) & (c_start[:, None] + L_CMP > s_start[None, :])
    imp = jnp.einsum('bqhgn,nj->bqhj', p, cover.astype(jnp.float32))
    cur = q_pos // L_SEL
    jj = jnp.arange(n_sel_blocks)
    valid = jj[None, :] <= cur[:, None]
    forced = (jj[None, :] == 0) | (jj[None, :] == cur[:, None]) | (jj[None, :] == cur[:, None] - 1)
    score = jnp.where(valid[None, :, None, :], imp + FORCE_BONUS * forced[None, :, None, :], NEG)
    top, idx = lax.top_k(score, min(N_SEL, n_sel_blocks))
    return idx, top > 0.5 * NEG


def gather_dense_blocks(rb, idx):
    bi = jnp.arange(rb.shape[0])[:, None, None, None]
    hi = jnp.arange(N_KV)[None, None, :, None]
    return rb[bi, hi, idx]


def gather_paged_blocks(pool, page_table, new_rb, idx):
    nb_past = page_table.shape[1] * PAGE_SIZE // L_SEL
    bi = jnp.arange(idx.shape[0])[:, None, None, None]
    hi = jnp.arange(N_KV)[None, None, :, None]
    start = jnp.minimum(idx, nb_past - 1) * L_SEL
    phys = page_table[bi, start // PAGE_SIZE]
    off = (start % PAGE_SIZE)[..., None] + jnp.arange(L_SEL)
    past_rows = pool[phys[..., None], off, hi[..., None]]
    new_rows = new_rb[bi, hi, jnp.clip(idx - nb_past, 0, new_rb.shape[2] - 1)]
    return jnp.where((idx >= nb_past)[..., None, None], new_rows, past_rows)


def sel_attention(q, q_pos, idx, blk_ok, kg, vg, slopes):
    s = jnp.einsum('bqhgd,bqhnld->bqhgnl', q, kg).astype(jnp.float32) * ATTN_SCALE
    k_pos = idx[..., None] * L_SEL + jnp.arange(L_SEL)
    dist = (q_pos[None, :, None, None, None] - k_pos).astype(jnp.float32)
    mask = ((dist >= 0) & blk_ok[..., None])[:, :, :, None]
    s = s - slopes[None, None, :, :, None, None] * dist[:, :, :, None]
    shp = s.shape
    p = masked_softmax(s.reshape(shp[:4] + (-1,)), jnp.broadcast_to(mask, shp).reshape(shp[:4] + (-1,)))
    vr = vg.reshape(vg.shape[:3] + (-1, HEAD_DIM))
    return jnp.einsum('bqhgm,bqhmd->bqhgd', p.astype(vr.dtype), vr)


def window_attend(q, q_pos, k, v, k_pos, slopes):
    s = jnp.einsum('bqhgd,bshd->bqhgs', q, k).astype(jnp.float32) * ATTN_SCALE
    dist = q_pos[:, None] - k_pos[None, :]
    mask = (dist >= 0) & (dist <= WINDOW) & (k_pos[None, :] >= 0)
    s = s - slopes[None, None, :, :, None] * dist.astype(jnp.float32)[None, :, None, None, :]
    p = masked_softmax(s, mask[None, :, None, None, :])
    return jnp.einsum('bqhgs,bshd->bqhgd', p.astype(v.dtype), v)


def merge_query_blocks(o):
    o = o.swapaxes(0, 1)
    return o.reshape((o.shape[0], -1) + o.shape[3:])


def combine_branches(o_c, o_s, o_w, g, w_o):
    o = g[..., 0:1] * o_c + g[..., 1:2] * o_s + g[..., 2:3] * o_w
    B, S = o.shape[:2]
    return o.reshape(B, S, N_HEADS * HEAD_DIM).astype(w_o.dtype) @ w_o


def prompt_context(kv, w_cmp_k, w_cmp_v):
    k_c, v_c, k_s, v_s, k_w, v_w = kv
    B, S = k_c.shape[:2]
    kc_blocks, ends = compressed_blocks(*chunk_sums(k_c, w_cmp_k))
    vc_blocks, _ = compressed_blocks(*chunk_sums(v_c, w_cmp_v))

    def to_blocks(r):
        return r.reshape(B, S // L_SEL, L_SEL, N_KV, HEAD_DIM).transpose(0, 3, 1, 2, 4)

    pad = jnp.zeros((B, WINDOW, N_KV, HEAD_DIM), k_w.dtype)
    return dict(k_cmp=kc_blocks, v_cmp=vc_blocks, ends=ends, n_sel=S // L_SEL,
                rbk=to_blocks(k_s), rbv=to_blocks(v_s),
                win_k=jnp.concatenate([pad, k_w], 1), win_v=jnp.concatenate([pad, v_w], 1))


def sample_context(kv, w_cmp_k, w_cmp_v, page_table, cache_k_cmp, cache_v_cmp, cache_k_win, cache_v_win):
    k_c, v_c, k_s, v_s, k_w, v_w = kv
    Bd, Sd = k_c.shape[:2]
    past = page_table.shape[1] * PAGE_SIZE

    def cmp_from_pages(pool, new_rows, w):
        past_rows = pool[page_table].reshape(Bd, past, N_KV, HEAD_DIM)
        f_p, s_p = chunk_sums(past_rows, w)
        f_n, s_n = chunk_sums(new_rows, w)
        return compressed_blocks(jnp.concatenate([f_p, f_n], 1), jnp.concatenate([s_p, s_n], 1))

    kc_blocks, ends = cmp_from_pages(cache_k_cmp, k_c, w_cmp_k)
    vc_blocks, _ = cmp_from_pages(cache_v_cmp, v_c, w_cmp_v)
    n_new = -(-Sd // L_SEL)

    def new_blocks(r):
        r = jnp.pad(r, ((0, 0), (0, n_new * L_SEL - Sd), (0, 0), (0, 0)))
        return r.reshape(Bd, n_new, L_SEL, N_KV, HEAD_DIM).transpose(0, 3, 1, 2, 4)

    w_buf = cache_k_win.shape[1]
    return dict(k_cmp=kc_blocks, v_cmp=vc_blocks, ends=ends, n_sel=past // L_SEL + n_new,
                new_rbk=new_blocks(k_s), new_rbv=new_blocks(v_s),
                win_k=jnp.concatenate([cache_k_win, k_w], 1), win_v=jnp.concatenate([cache_v_win, v_w], 1),
                win_pos=past - w_buf + jnp.arange(w_buf + Sd))


def nsa_prompt(u, w_qg, w_o, ctx, slopes, pos):
    S = u.shape[1]
    q, g = queries_and_gates(u, w_qg)
    o_c, p = cmp_attention(q, pos, ctx['k_cmp'], ctx['v_cmp'], ctx['ends'], slopes)
    idx, ok = select_blocks(p, pos, ctx['n_sel'])

    def sel_block(i):
        q0 = i * SEL_QBLK
        ib = lax.dynamic_slice_in_dim(idx, q0, SEL_QBLK, 1)
        return sel_attention(lax.dynamic_slice_in_dim(q, q0, SEL_QBLK, 1), q0 + jnp.arange(SEL_QBLK), ib,
                             lax.dynamic_slice_in_dim(ok, q0, SEL_QBLK, 1),
                             gather_dense_blocks(ctx['rbk'], ib), gather_dense_blocks(ctx['rbv'], ib), slopes)

    def win_block(i):
        q0 = i * WIN_QBLK
        kb = lax.dynamic_slice_in_dim(ctx['win_k'], q0, WINDOW + WIN_QBLK, 1)
        vb = lax.dynamic_slice_in_dim(ctx['win_v'], q0, WINDOW + WIN_QBLK, 1)
        return window_attend(lax.dynamic_slice_in_dim(q, q0, WIN_QBLK, 1), q0 + jnp.arange(WIN_QBLK), kb, vb,
                             q0 - WINDOW + jnp.arange(WINDOW + WIN_QBLK), slopes)

    o_s = merge_query_blocks(lax.map(sel_block, jnp.arange(S // SEL_QBLK)))
    o_w = merge_query_blocks(lax.map(win_block, jnp.arange(S // WIN_QBLK)))
    return combine_branches(o_c, o_s, o_w, g, w_o)


def nsa_sample(u, w_qg, w_o, ctx, slopes, pos, page_table, cache_k_sel, cache_v_sel):
    q, g = queries_and_gates(u, w_qg)
    o_c, p = cmp_attention(q, pos, ctx['k_cmp'], ctx['v_cmp'], ctx['ends'], slopes)
    idx, ok = select_blocks(p, pos, ctx['n_sel'])
    kg = gather_paged_blocks(cache_k_sel, page_table, ctx['new_rbk'], idx)
    vg = gather_paged_blocks(cache_v_sel, page_table, ctx['new_rbv'], idx)
    o_s = sel_attention(q, pos, idx, ok, kg, vg, slopes)
    o_w = window_attend(q, pos, ctx['win_k'], ctx['win_v'], ctx['win_pos'], slopes)
    return combine_branches(o_c, o_s, o_w, g, w_o)


def setup_inputs(seed: int = 0) -> dict:
    key = jax.random.key(seed)
    ks = jax.random.split(key, 40)

    def nrm(i, shape, scale):
        return jax.random.normal(ks[i], shape, jnp.float32) * scale

    n_pages = PAST_LEN // PAGE_SIZE
    n_used = DEC_BATCH * n_pages
    n_pool = n_used + max(1, n_used // 4)
    w_buf = min(WINDOW, PAST_LEN)
    page_table = jax.random.permutation(ks[4], n_pool)[:n_used].reshape(DEC_BATCH, n_pages).astype(jnp.int32)
    pool_shape = (n_pool, PAGE_SIZE, N_KV, HEAD_DIM)
    u_a = jax.random.uniform(ks[24], (N_A_LAYERS, LRU_WIDTH), jnp.float32, 0.9, 0.999)
    s_a = u_a ** (1.0 / LRU_C)
    kv_scale = jnp.array([1.0, BETA, 1.0, BETA, 1.0, BETA], jnp.float32)
    w_kv = (nrm(26, (D_MODEL, 6, N_KV * HEAD_DIM), D_MODEL ** -0.5) * kv_scale[None, :, None]).reshape(D_MODEL, 6 * N_KV * HEAD_DIM)
    return {
        'x_prompt': nrm(0, (BATCH, SEQ, D_MODEL), 1.0),
        'x_sample': nrm(1, (DEC_BATCH, DEC_SEQ, D_MODEL), 1.0),
        'c_prompt': nrm(2, (BATCH, D_MODEL), 1.0),
        'c_sample': nrm(3, (DEC_BATCH, D_MODEL), 1.0),
        'page_table': page_table,
        'cache_k_cmp': nrm(5, pool_shape, 1.0),
        'cache_v_cmp': nrm(6, pool_shape, BETA),
        'cache_k_sel': nrm(7, pool_shape, 1.0),
        'cache_v_sel': nrm(8, pool_shape, BETA),
        'cache_k_win': nrm(9, (DEC_BATCH, w_buf, N_KV, HEAD_DIM), 1.0),
        'cache_v_win': nrm(10, (DEC_BATCH, w_buf, N_KV, HEAD_DIM), BETA),
        'state_h': nrm(11, (N_A_LAYERS, DEC_BATCH, LRU_WIDTH), 0.5),
        'state_conv': nrm(12, (N_A_LAYERS, DEC_BATCH, CONV_W - 1, LRU_WIDTH), 1.0),
        'w_ada': nrm(13, (DEPTH, D_MODEL, 6 * D_MODEL), 0.5 * D_MODEL ** -0.5),
        'b_ada': nrm(14, (DEPTH, 6 * D_MODEL), 0.01),
        'ln_g': 1.0 + nrm(15, (DEPTH, 2, D_MODEL), 0.02),
        'ln_b': nrm(16, (DEPTH, 2, D_MODEL), 0.02),
        'a_w_in': nrm(17, (N_A_LAYERS, D_MODEL, 2 * LRU_WIDTH), D_MODEL ** -0.5),
        'a_conv_w': nrm(18, (N_A_LAYERS, CONV_W, LRU_WIDTH), CONV_W ** -0.5),
        'a_conv_b': nrm(19, (N_A_LAYERS, LRU_WIDTH), 0.01),
        'a_w_r': nrm(20, (N_A_LAYERS, LRU_BLOCKS, LRU_BW, LRU_BW), LRU_BW ** -0.5),
        'a_b_r': nrm(21, (N_A_LAYERS, LRU_WIDTH), 0.01),
        'a_w_i': nrm(22, (N_A_LAYERS, LRU_BLOCKS, LRU_BW, LRU_BW), LRU_BW ** -0.5),
        'a_b_i': nrm(23, (N_A_LAYERS, LRU_WIDTH), 0.01),
        'a_lambda': jnp.log(s_a) - jnp.log1p(-s_a),
        'a_w_out': nrm(25, (N_A_LAYERS, LRU_WIDTH, D_MODEL), BETA * LRU_WIDTH ** -0.5),
        'w_kv': w_kv,
        'w_cmp_k': (1.0 + nrm(27, (L_CMP, N_KV, HEAD_DIM), 0.5)) * L_CMP ** -0.5,
        'w_cmp_v': (1.0 + nrm(28, (L_CMP, N_KV, HEAD_DIM), 0.5)) * L_CMP ** -0.5,
        'b_w_qg': nrm(29, (N_B_LAYERS, D_MODEL, N_HEADS * HEAD_DIM + 3 * N_HEADS), D_MODEL ** -0.5),
        'b_w_o': nrm(30, (N_B_LAYERS, N_HEADS * HEAD_DIM, D_MODEL), BETA * (N_HEADS * HEAD_DIM) ** -0.5),
        'f_w_up': nrm(31, (DEPTH, D_MODEL, 2 * D_FF), D_MODEL ** -0.5),
        'f_w_down': nrm(32, (DEPTH, D_FF, D_MODEL), BETA * D_FF ** -0.5),
    }


def reference(x_prompt, x_sample, c_prompt, c_sample, page_table,
              cache_k_cmp, cache_v_cmp, cache_k_sel, cache_v_sel, cache_k_win, cache_v_win,
              state_h, state_conv,
              w_ada, b_ada, ln_g, ln_b,
              a_w_in, a_conv_w, a_conv_b, a_w_r, a_b_r, a_w_i, a_b_i, a_lambda, a_w_out,
              w_kv, w_cmp_k, w_cmp_v, b_w_qg, b_w_o, f_w_up, f_w_down):
    slopes = alibi_slopes(N_HEADS).reshape(N_KV, GQA)
    B, S, _ = x_prompt.shape
    Bd, Sd, _ = x_sample.shape
    past = page_table.shape[1] * PAGE_SIZE
    pos_p = jnp.arange(S, dtype=jnp.int32)
    pos_s = past + jnp.arange(Sd, dtype=jnp.int32)
    xp, xs = x_prompt, x_sample
    h_p, h_s, cb_p, cb_s = [], [], [], []
    kv_p = kv_s = ctx_p = ctx_s = None
    for l in range(DEPTH):
        mp = ada_modulation(c_prompt, w_ada[l], b_ada[l])
        ms = ada_modulation(c_sample, w_ada[l], b_ada[l])
        up = modulate(xp, mp[0], mp[1])
        us = modulate(xs, ms[0], ms[1])
        if l < N_A_LAYERS:
            a_par = (a_conv_w[l], a_conv_b[l], a_w_r[l], a_b_r[l], a_w_i[l], a_b_i[l], a_lambda[l], a_w_out[l])
            yp, cbp, hp = rglru_block(up, jnp.zeros((B, CONV_W - 1, LRU_WIDTH), up.dtype),
                                      jnp.zeros((B, LRU_WIDTH), up.dtype), a_w_in[l], *a_par)
            ys, cbs, hs = rglru_block(us, state_conv[l], state_h[l], a_w_in[l], *a_par)
            h_p.append(hp)
            h_s.append(hs)
            cb_p.append(cbp)
            cb_s.append(cbs)
        else:
            j = l - N_A_LAYERS
            yp = nsa_prompt(up, b_w_qg[j], b_w_o[j], ctx_p, slopes, pos_p)
            ys = nsa_sample(us, b_w_qg[j], b_w_o[j], ctx_s, slopes, pos_s, page_table, cache_k_sel, cache_v_sel)
        xp = post_norm_residual(xp, yp, mp[2], ln_g[l, 0], ln_b[l, 0])
        xs = post_norm_residual(xs, ys, ms[2], ln_g[l, 0], ln_b[l, 0])
        xp = post_norm_residual(xp, swiglu(modulate(xp, mp[3], mp[4]), f_w_up[l], f_w_down[l]), mp[5], ln_g[l, 1], ln_b[l, 1])
        xs = post_norm_residual(xs, swiglu(modulate(xs, ms[3], ms[4]), f_w_up[l], f_w_down[l]), ms[5], ln_g[l, 1], ln_b[l, 1])
        if l == N_A_LAYERS - 1:
            kv_p = project_shared_kv(xp, w_kv)
            kv_s = project_shared_kv(xs, w_kv)
            ctx_p = prompt_context(kv_p, w_cmp_k, w_cmp_v)
            ctx_s = sample_context(kv_s, w_cmp_k, w_cmp_v, page_table, cache_k_cmp, cache_v_cmp, cache_k_win, cache_v_win)
    w_buf = cache_k_win.shape[1]
    k_win_p = kv_p[4][:, -min(WINDOW, S):]
    v_win_p = kv_p[5][:, -min(WINDOW, S):]
    k_win_s = ctx_s['win_k'][:, -w_buf:]
    v_win_s = ctx_s['win_v'][:, -w_buf:]
    new_h_p = jnp.stack(h_p)
    new_h_s = jnp.stack(h_s)
    new_conv_p = jnp.stack(cb_p)
    new_conv_s = jnp.stack(cb_s)
    return (xp, xs, kv_p[0], kv_s[0], kv_p[1], kv_s[1], kv_p[2], kv_s[2], kv_p[3], kv_s[3],
            k_win_p, k_win_s, v_win_p, v_win_s, new_h_p, new_h_s, new_conv_p, new_conv_s)
```

```python
import functools

import numpy as np
import jax
import jax.numpy as jnp
from jax import lax
from jax.experimental import pallas as pl
from jax.experimental.pallas import tpu as pltpu

F32 = jnp.float32
BF16 = jnp.bfloat16

HEAD_DIM = 128
N_KV = 4
L_CMP = 32
D_CMP = 16
L_SEL = 64
N_SEL = 16
WINDOW = 512
PAGE_SIZE = 128
CONV_W = 4
LRU_BLOCKS = 16
LRU_C = 8.0
NEG = -1e30
FORCE_BONUS = 1e3
ATTN_SCALE = HEAD_DIM ** -0.5
LN_EPS = 1e-5
SEL_SHIFT = L_SEL.bit_length() - 1
assert 1 << SEL_SHIFT == L_SEL

V7X_VMEM_BYTES = 64 * 1024 * 1024
V7X_LANES = 128
V7X_SUBLANES = 8
VMEM_CAP = V7X_VMEM_BYTES - 8 * 1024 * 1024


def _params(sem, vmem_bytes):
    return pltpu.CompilerParams(dimension_semantics=sem,
                                vmem_limit_bytes=int(min(max(vmem_bytes, 16 << 20), VMEM_CAP)))


def _nt_dot(a, b, **kw):
    return lax.dot_general(a, b, (((1,), (1,)), ((), ())), preferred_element_type=F32, **kw)


def _ada_kernel(c_ref, w_ref, b_ref, o_ref):
    c = c_ref[...]
    s = (c * jax.nn.sigmoid(c)).astype(BF16)
    o_ref[...] = jnp.dot(s, w_ref[...].astype(BF16), preferred_element_type=F32) + b_ref[...]


def _ada(c_all, w_ada, b_ada, tn=1024):
    depth, d, n6 = w_ada.shape
    r = c_all.shape[0]
    return pl.pallas_call(
        _ada_kernel,
        grid=(depth, n6 // tn),
        in_specs=[pl.BlockSpec((r, d), lambda l, n: (0, 0)),
                  pl.BlockSpec((None, d, tn), lambda l, n: (l, 0, n)),
                  pl.BlockSpec((None, 1, tn), lambda l, n: (l, 0, n))],
        out_specs=pl.BlockSpec((None, r, tn), lambda l, n: (l, 0, n)),
        out_shape=jax.ShapeDtypeStruct((depth, r, n6), F32),
        compiler_params=_params(("arbitrary", "arbitrary"), 2 * d * tn * 4 + d * tn * 2 + (8 << 20)),
        name="ada",
    )(c_all, w_ada, b_ada.reshape(depth, 1, n6))


def _modulate_kernel(x_ref, sh_ref, sc_ref, u_ref):
    u_ref[...] = (x_ref[...] * (1.0 + sc_ref[...]) + sh_ref[...]).astype(u_ref.dtype)


def _row_specs(t, d, tm, nb):
    tiles_per_batch = (t // nb) // tm
    xspec = pl.BlockSpec((tm, d), lambda i: (i, 0))

    def vspec(r):
        return pl.BlockSpec((None, r, d), lambda i: (i // tiles_per_batch, 0, 0))
    return xspec, vspec


def _modulate(x, shift, scale, tm):
    t, d = x.shape
    nb, r, _ = shift.shape
    xspec, vspec = _row_specs(t, d, tm, nb)
    return pl.pallas_call(
        _modulate_kernel,
        grid=(t // tm,),
        in_specs=[xspec, vspec(r), vspec(r)],
        out_specs=xspec,
        out_shape=jax.ShapeDtypeStruct((t, d), BF16),
        compiler_params=_params(("arbitrary",), 6 * tm * d * 4),
        name="modulate",
    )(x, shift, scale)


def _ln_mod_kernel(*refs, alpha, want_u, want_xb):
    x_ref, y_ref, gate_ref, g_ref, b_ref = refs[:5]
    pos = 5
    if want_u:
        sh_ref, sc_ref = refs[pos:pos + 2]
        pos += 2
    xo_ref = refs[pos]
    pos += 1
    v = alpha * x_ref[...] + (1.0 + gate_ref[...]) * y_ref[...]
    mu = jnp.mean(v, axis=-1, keepdims=True)
    dlt = v - mu
    var = jnp.mean(dlt * dlt, axis=-1, keepdims=True)
    xn = dlt * lax.rsqrt(var + LN_EPS) * g_ref[...] + b_ref[...]
    xo_ref[...] = xn
    if want_u:
        refs[pos][...] = (xn * (1.0 + sc_ref[...]) + sh_ref[...]).astype(BF16)
        pos += 1
    if want_xb:
        refs[pos][...] = xn.astype(BF16)


def _ln_mod(x, y, gate, ln_g, ln_b, alpha, tm, shift=None, scale=None, want_xb=False):
    t, d = x.shape
    nb, r, _ = gate.shape
    xspec, vspec = _row_specs(t, d, tm, nb)
    pspec = pl.BlockSpec((1, d), lambda i: (0, 0))
    want_u = shift is not None
    ins = [x, y, gate, ln_g.reshape(1, d), ln_b.reshape(1, d)]
    in_specs = [xspec, xspec, vspec(r), pspec, pspec]
    out_shape = [jax.ShapeDtypeStruct((t, d), F32)]
    out_specs = [xspec]
    if want_u:
        ins += [shift, scale]
        in_specs += [vspec(r), vspec(r)]
        out_shape.append(jax.ShapeDtypeStruct((t, d), BF16))
        out_specs.append(xspec)
    if want_xb:
        out_shape.append(jax.ShapeDtypeStruct((t, d), BF16))
        out_specs.append(xspec)
    return pl.pallas_call(
        functools.partial(_ln_mod_kernel, alpha=alpha, want_u=want_u, want_xb=want_xb),
        grid=(t // tm,),
        in_specs=in_specs, out_specs=out_specs, out_shape=out_shape,
        compiler_params=_params(("arbitrary",), 12 * tm * d * 4),
        name="ln_mod",
    )(*ins)


def _gelu_tanh(x):
    return jax.nn.gelu(x, approximate=True)


def _mm_kernel(*refs, n_w, cast_w, epilogue):
    x_ref = refs[0]
    w_refs = refs[1:1 + n_w]
    o_ref = refs[1 + n_w]
    scr = refs[2 + n_w:]
    if cast_w == "scratch":
        @pl.when(pl.program_id(1) == 0)
        def _():
            for w_ref, s_ref in zip(w_refs, scr):
                s_ref[...] = w_ref[...].astype(BF16)
        ws = [s_ref[...] for s_ref in scr]
    elif cast_w == "inline":
        ws = [w_ref[...].astype(BF16) for w_ref in w_refs]
    else:
        ws = [w_ref[...] for w_ref in w_refs]
    x = x_ref[...]
    accs = [jnp.dot(x, w, preferred_element_type=F32) for w in ws]
    o_ref[...] = epilogue(*accs).astype(o_ref.dtype)


def _matmul(x, w, n_out, *, tm, tn, out_dtype, layer=None, epilogue=None, col_off=0, col_off2=None,
            name="mm"):
    m, k = x.shape
    assert m % tm == 0 and n_out % tn == 0 and col_off % tn == 0
    assert (w.ndim == 3) == (layer is not None) and w.shape[-2] == k
    offs = [col_off // tn] + ([] if col_off2 is None else [col_off2 // tn])
    n_w = len(offs)
    if epilogue is None:
        epilogue = lambda a: a
    m_tiles = m // tm
    if w.dtype == BF16:
        cast_w = "none"
    else:
        cast_w = "scratch" if m_tiles > 1 else "inline"
    scratch = [pltpu.VMEM((k, tn), BF16) for _ in range(n_w)] if cast_w == "scratch" else []
    w_bytes = k * tn * w.dtype.itemsize
    vmem = (2 * tm * k * 2 + n_w * (2 * w_bytes + k * tn * 2)
            + 2 * tm * tn * 4 * (n_w + 1) + (4 << 20))
    if layer is None:
        w_specs = [pl.BlockSpec((k, tn), functools.partial(lambda n, i, o: (0, n + o), o=o)) for o in offs]
    else:
        w_specs = [pl.BlockSpec((None, k, tn), functools.partial(lambda n, i, o: (layer, 0, n + o), o=o))
                   for o in offs]
    return pl.pallas_call(
        functools.partial(_mm_kernel, n_w=n_w, cast_w=cast_w, epilogue=epilogue),
        grid=(n_out // tn, m_tiles),
        in_specs=[pl.BlockSpec((tm, k), lambda n, i: (i, 0))] + w_specs,
        out_specs=pl.BlockSpec((tm, tn), lambda n, i: (i, n)),
        out_shape=jax.ShapeDtypeStruct((m, n_out), out_dtype),
        scratch_shapes=scratch,
        compiler_params=_params(("arbitrary", "arbitrary"), vmem),
        name=name,
    )(x, *([w] * n_w))


def _swiglu_epilogue(g, v):
    return g * jax.nn.sigmoid(g) * v


def _softplus_neg(lam):
    return jnp.maximum(-lam, 0.0) + jnp.log1p(jnp.exp(-jnp.abs(lam)))


def _lru_gates(xc, wr_ref, br_ref, wi_ref, bi_ref, sp, a_ref, b_ref, nblk, bw):
    xcb = xc.astype(BF16)
    for k in range(nblk):
        cs = slice(k * bw, (k + 1) * bw)
        xb = xcb[:, cs]
        r = jax.nn.sigmoid(jnp.dot(xb, wr_ref[k], preferred_element_type=F32) + br_ref[:, cs])
        ig = jax.nn.sigmoid(jnp.dot(xb, wi_ref[k], preferred_element_type=F32) + bi_ref[:, cs])
        log_a = (-LRU_C) * r * sp[:, cs]
        a_ref[:, cs] = jnp.exp(log_a)
        th = jnp.tanh(log_a)
        b_ref[:, cs] = jnp.sqrt(-2.0 * th / (1.0 - th)) * (ig * xc[:, cs])


SCAN_LANES = 4 * V7X_LANES


def _rglru_kernel(rx_ref, ggx_ref, cb0_ref, h0_ref, cw_ref, cbias_ref, wr_ref, br_ref, wi_ref, bi_ref,
                  lam_ref, y_ref, hlast_ref, ctail_ref, xpad_ref, a_ref, b_ref, hcar_ref, *, tm, nblk, bw):
    s = pl.program_id(1)
    width = nblk * bw
    pad = V7X_SUBLANES
    tail = CONV_W - 1

    @pl.when(s == 0)
    def _():
        xpad_ref[pad - tail:pad, :] = cb0_ref[...]
        hcar_ref[...] = h0_ref[...]

    rx = rx_ref[...]
    xpad_ref[pad:pad + tm, :] = rx
    cw = cw_ref[...]
    xc = cbias_ref[...] + rx * cw[tail:tail + 1]
    for j in range(tail):
        xc = xc + xpad_ref[pad - tail + j:pad - tail + j + tm, :] * cw[j:j + 1]
    xpad_ref[pad - tail:pad, :] = rx[tm - tail:tm, :]

    _lru_gates(xc, wr_ref, br_ref, wi_ref, bi_ref, _softplus_neg(lam_ref[...]), a_ref, b_ref, nblk, bw)

    row = lax.broadcasted_iota(jnp.int32, (V7X_SUBLANES, SCAN_LANES), 0)
    for c in range(width // SCAN_LANES):
        cs = slice(c * SCAN_LANES, (c + 1) * SCAN_LANES)

        def group(gi, h, cs=cs):
            r0 = pl.multiple_of(gi * V7X_SUBLANES, V7X_SUBLANES)
            a = a_ref[pl.ds(r0, V7X_SUBLANES), cs]
            b = b_ref[pl.ds(r0, V7X_SUBLANES), cs]
            for sft in (1, 2, 4):
                keep = row >= sft
                a_s = jnp.where(keep, pltpu.roll(a, sft, 0), 1.0)
                b_s = jnp.where(keep, pltpu.roll(b, sft, 0), 0.0)
                b = a * b_s + b
                a = a * a_s
            hs = a * h + b
            b_ref[pl.ds(r0, V7X_SUBLANES), cs] = hs
            return jnp.broadcast_to(hs[V7X_SUBLANES - 1:V7X_SUBLANES, :], hs.shape)

        h_in = jnp.broadcast_to(hcar_ref[:, cs], (V7X_SUBLANES, SCAN_LANES))
        h_out = lax.fori_loop(0, tm // V7X_SUBLANES, group, h_in)
        hcar_ref[:, cs] = h_out[0:1, :]

    y_ref[...] = (ggx_ref[...].astype(F32) * b_ref[...]).astype(BF16)

    @pl.when(s == pl.num_programs(1) - 1)
    def _():
        hlast_ref[...] = hcar_ref[...]
        ctail_ref[...] = rx[tm - tail:tm, :]


def _rglru_prompt(rx, ggx, conv_buf, h0, cw, cbias, wr, br, wi, bi, lam, nb, tm):
    t, width = rx.shape
    s_len = t // nb
    nblk, bw, _ = wr.shape
    tail = CONV_W - 1
    ns = s_len // tm
    row = lambda a: a.reshape(1, width)
    xspec = pl.BlockSpec((tm, width), lambda b, s: (b * ns + s, 0))
    pspec = pl.BlockSpec((1, width), lambda b, s: (0, 0))
    wspec = pl.BlockSpec((nblk, bw, bw), lambda b, s: (0, 0, 0))
    vmem = (2 * tm * width * (4 + 2 + 2) + 3 * (tm + 8) * width * 4 + 4 * nblk * bw * bw * 2 + (6 << 20))
    return pl.pallas_call(
        functools.partial(_rglru_kernel, tm=tm, nblk=nblk, bw=bw),
        grid=(nb, ns),
        in_specs=[xspec, xspec,
                  pl.BlockSpec((None, tail, width), lambda b, s: (b, 0, 0)),
                  pl.BlockSpec((None, 1, width), lambda b, s: (b, 0, 0)),
                  pl.BlockSpec((CONV_W, width), lambda b, s: (0, 0)), pspec,
                  wspec, pspec, wspec, pspec, pspec],
        out_specs=[xspec,
                   pl.BlockSpec((None, 1, width), lambda b, s: (b, 0, 0)),
                   pl.BlockSpec((None, tail, width), lambda b, s: (b, 0, 0))],
        out_shape=[jax.ShapeDtypeStruct((t, width), BF16),
                   jax.ShapeDtypeStruct((nb, 1, width), F32),
                   jax.ShapeDtypeStruct((nb, tail, width), F32)],
        scratch_shapes=[pltpu.VMEM((tm + V7X_SUBLANES, width), F32), pltpu.VMEM((tm, width), F32),
                        pltpu.VMEM((tm, width), F32), pltpu.VMEM((1, width), F32)],
        compiler_params=_params(("arbitrary", "arbitrary"), vmem),
        name="rglru",
    )(rx, ggx, conv_buf, h0.reshape(nb, 1, width), cw, row(cbias), wr, row(br), wi, row(bi), row(lam))


def _rglru_step_kernel(rx_ref, ggx_ref, cb_ref, h0_ref, cw_ref, cbias_ref, wr_ref, br_ref, wi_ref, bi_ref,
                       lam_ref, y_ref, h_ref, cnew_ref, a_ref, b_ref, *, nblk, bw):
    tail = CONV_W - 1
    rx = rx_ref[...]
    cw = cw_ref[...]
    xc = cbias_ref[...] + rx * cw[tail:tail + 1]
    for j in range(tail):
        xc = xc + cb_ref[j] * cw[j:j + 1]
    _lru_gates(xc, wr_ref, br_ref, wi_ref, bi_ref, _softplus_neg(lam_ref[...]), a_ref, b_ref, nblk, bw)
    h = a_ref[...] * h0_ref[...] + b_ref[...]
    h_ref[...] = h
    y_ref[...] = (ggx_ref[...].astype(F32) * h).astype(BF16)
    for j in range(tail - 1):
        cnew_ref[j] = cb_ref[j + 1]
    cnew_ref[tail - 1] = rx


def _rglru_step(rx, ggx, conv_t, h0, cw, cbias, wr, br, wi, bi, lam):
    bd, width = rx.shape
    nblk, bw, _ = wr.shape
    row = lambda a: a.reshape(1, width)
    return pl.pallas_call(
        functools.partial(_rglru_step_kernel, nblk=nblk, bw=bw),
        out_shape=[jax.ShapeDtypeStruct((bd, width), BF16), jax.ShapeDtypeStruct((bd, width), F32),
                   jax.ShapeDtypeStruct(conv_t.shape, F32)],
        scratch_shapes=[pltpu.VMEM((bd, width), F32), pltpu.VMEM((bd, width), F32)],
        compiler_params=pltpu.CompilerParams(vmem_limit_bytes=32 << 20),
        name="rglru_step",
    )(rx, ggx, conv_t, h0, cw, row(cbias), wr, row(br), wi, row(bi), row(lam))


def _chunk_sums(x, w):
    n = x.shape[0] // D_CMP
    x3 = x.reshape(n, D_CMP, x.shape[1])
    first = jnp.sum(x3 * w[None, :D_CMP, :], axis=1)
    second = jnp.sum(x3 * w[None, D_CMP:, :], axis=1)
    return first, second


def _blocks_from_sums(first, second):
    n = first.shape[0]
    nxt = pltpu.roll(second, n - 1, 0)
    rows = lax.broadcasted_iota(jnp.int32, first.shape, 0)
    return jnp.where(rows < n - 1, first + nxt, 0.0)


def _cmp_blocks_kernel(k_ref, v_ref, wk_ref, wv_ref, kc_ref, vc_ref):
    kc_ref[...] = _blocks_from_sums(*_chunk_sums(k_ref[...], wk_ref[...])).astype(BF16)
    vc_ref[...] = _blocks_from_sums(*_chunk_sums(v_ref[...], wv_ref[...])).astype(BF16)


def _cmp_blocks_prompt(k_c, v_c, wk, wv, nb):
    t, c = k_c.shape
    s_len = t // nb
    nch = s_len // D_CMP
    kvspec = pl.BlockSpec((s_len, HEAD_DIM), lambda b, h: (b, h))
    wspec = pl.BlockSpec((L_CMP, HEAD_DIM), lambda b, h: (0, h))
    ospec = pl.BlockSpec((None, None, nch, HEAD_DIM), lambda b, h: (b, h, 0, 0))
    osh = jax.ShapeDtypeStruct((nb, N_KV, nch, HEAD_DIM), BF16)
    return pl.pallas_call(
        _cmp_blocks_kernel, grid=(nb, N_KV),
        in_specs=[kvspec, kvspec, wspec, wspec], out_specs=[ospec, ospec], out_shape=[osh, osh],
        compiler_params=_params(("arbitrary", "arbitrary"), 24 << 20),
        name="cmp_blocks",
    )(k_c, v_c, wk.reshape(L_CMP, c), wv.reshape(L_CMP, c))


def _block_scores_topk_t(imp_t, q0, nsel, tq):
    nsp = -(-nsel // V7X_SUBLANES) * V7X_SUBLANES
    jt = lax.broadcasted_iota(jnp.int32, (nsp, tq), 0)
    qpos = q0 + lax.broadcasted_iota(jnp.int32, (nsp, tq), 1)
    cur = qpos >> SEL_SHIFT
    valid = (jt <= cur) & (jt < nsel)
    forced = (jt == 0) | (jt == cur) | (jt == cur - 1)
    score = jnp.where(valid, imp_t[:nsp] + jnp.where(forced, FORCE_BONUS, 0.0), NEG)
    rank = jnp.zeros((nsp, tq), F32)
    for i in range(nsel):
        ri = score[i:i + 1, :]
        beats = (ri > score) | ((ri == score) & (jt > i))
        rank = rank + jnp.where(beats, 1.0, 0.0)
    return jnp.where((rank < float(min(N_SEL, nsel))) & valid, 1.0, 0.0)


def _nsa_prompt_kernel(slopes_ref, q_ref, gt_ref, kc_ref, vc_ref, ks_ref, vs_ref, kw_ref, vw_ref, e_ref,
                       o_ref, ksb, vsb, kwb, vwb, mask_ref, m_ref, l_ref, acc_ref, oc_ref, os_ref,
                       *, tq, gqa, s_len):
    h = pl.program_id(1)
    i = pl.program_id(2)
    q0 = i * tq
    hd = HEAD_DIM
    ncp = kc_ref.shape[0]
    n_cmp = ncp - 1
    nsel = s_len // L_SEL
    nkt = s_len // tq

    @pl.when(i == 0)
    def _():
        ksb[...] = ks_ref[...].astype(BF16)
        vsb[...] = vs_ref[...].astype(BF16)
        kwb[...] = kw_ref[...].astype(BF16)
        vwb[...] = vw_ref[...].astype(BF16)

    slopes = [slopes_ref[h * gqa + g] for g in range(gqa)]
    qs = jnp.concatenate([q_ref[:, g * hd:(g + 1) * hd] for g in range(gqa)], axis=0)

    sc = _nt_dot(qs, kc_ref[...]) * ATTN_SCALE
    n_io = lax.broadcasted_iota(jnp.int32, (tq, ncp), 1)
    qpos_c = q0 + lax.broadcasted_iota(jnp.int32, (tq, ncp), 0)
    dist_c = (qpos_c - ((n_io + 2) * D_CMP - 1)).astype(F32)
    ok_c = (dist_c >= 0.0) & (n_io < n_cmp)
    psum = jnp.zeros((tq, ncp), F32)
    pcs = []
    for g in range(gqa):
        s = jnp.where(ok_c, sc[g * tq:(g + 1) * tq] - slopes[g] * dist_c, NEG)
        p = jnp.where(ok_c, jnp.exp(s - jnp.max(s, axis=1, keepdims=True)), 0.0)
        p = p / jnp.maximum(jnp.sum(p, axis=1, keepdims=True), 1e-30)
        psum = psum + p
        pcs.append(p.astype(BF16))
    oc_ref[...] = jnp.dot(jnp.concatenate(pcs, axis=0), vc_ref[...], preferred_element_type=F32)

    j_io = lax.broadcasted_iota(jnp.int32, (V7X_LANES, ncp), 0)
    n_io2 = lax.broadcasted_iota(jnp.int32, (V7X_LANES, ncp), 1)
    cover_t = jnp.where((n_io2 * D_CMP < j_io * L_SEL + L_SEL) & (n_io2 * D_CMP + L_CMP > j_io * L_SEL)
                        & (n_io2 < n_cmp) & (j_io < nsel), 1.0, 0.0)
    imp_t = _nt_dot(cover_t, psum, precision=lax.Precision.HIGHEST)
    sel_t = _block_scores_topk_t(imp_t, q0, nsel, tq)
    sel_t = jnp.concatenate([sel_t, jnp.zeros((V7X_LANES - sel_t.shape[0], tq), F32)], axis=0)
    sel = sel_t.T.astype(BF16)
    for kt in range(nkt):
        mask_ref[kt] = jnp.dot(sel, e_ref[:, kt * tq:(kt + 1) * tq], preferred_element_type=F32)

    def attend(k_scr, v_scr, kt_lo, kt_hi, bias_fn, out_ref):
        m_ref[...] = jnp.full(m_ref.shape, -jnp.inf, F32)
        l_ref[...] = jnp.zeros(l_ref.shape, F32)
        acc_ref[...] = jnp.zeros(acc_ref.shape, F32)

        def body(kt, carry):
            k0 = pl.multiple_of(kt * tq, tq)
            s_all = _nt_dot(qs, k_scr[pl.ds(k0, tq), :]) * ATTN_SCALE
            kpos = k0 + lax.broadcasted_iota(jnp.int32, (tq, tq), 1)
            qpos = q0 + lax.broadcasted_iota(jnp.int32, (tq, tq), 0)
            dist = (qpos - kpos).astype(F32)
            bias = bias_fn(kt, dist)
            ps, alphas = [], []
            for g in range(gqa):
                rows = slice(g * tq, (g + 1) * tq)
                s = s_all[rows] - slopes[g] * dist + bias
                m_prev = m_ref[rows]
                m_new = jnp.maximum(m_prev, jnp.max(s, axis=1, keepdims=True))
                alpha = jnp.exp(m_prev - m_new)
                p = jnp.exp(s - m_new)
                l_ref[rows] = alpha * l_ref[rows] + jnp.sum(p, axis=1, keepdims=True)
                m_ref[rows] = m_new
                ps.append(p.astype(BF16))
                alphas.append(alpha)
            pv = jnp.dot(jnp.concatenate(ps, axis=0), v_scr[pl.ds(k0, tq), :], preferred_element_type=F32)
            acc_ref[...] = jnp.concatenate(alphas, axis=0) * acc_ref[...] + pv
            return carry

        lax.fori_loop(kt_lo, kt_hi, body, 0)
        out_ref[...] = acc_ref[...] / jnp.maximum(l_ref[...], 1e-30)

    attend(ksb, vsb, 0, i + 1,
           lambda kt, dist: jnp.where((mask_ref[kt] > 0.5) & (dist >= 0.0), 0.0, NEG), os_ref)
    attend(kwb, vwb, jnp.maximum(i - WINDOW // tq, 0), i + 1,
           lambda kt, dist: jnp.where((dist >= 0.0) & (dist <= float(WINDOW)), 0.0, NEG), acc_ref)

    gt = gt_ref[...]
    for g in range(gqa):
        rows = slice(g * tq, (g + 1) * tq)
        o = (gt[:, 3 * g:3 * g + 1] * oc_ref[rows] + gt[:, 3 * g + 1:3 * g + 2] * os_ref[rows]
             + gt[:, 3 * g + 2:3 * g + 3] * acc_ref[rows])
        o_ref[:, g * hd:(g + 1) * hd] = o.astype(BF16)


def _nsa_prompt(q, gates, kc, vc, k_s, v_s, k_w, v_w, slopes, nb, tq):
    t, dq = q.shape
    s_len = t // nb
    gqa = dq // (N_KV * HEAD_DIM)
    nq = s_len // tq
    ncp = kc.shape[2]
    assert WINDOW % tq == 0 and s_len % tq == 0 and s_len // L_SEL <= V7X_LANES
    expand = np.zeros((V7X_LANES, s_len), np.float32)
    expand[np.arange(s_len) // L_SEL, np.arange(s_len)] = 1.0
    rows = gqa * tq
    qspec = pl.BlockSpec((tq, gqa * HEAD_DIM), lambda b, h, i, sl: (b * nq + i, h))
    gspec = pl.BlockSpec((tq, V7X_LANES), lambda b, h, i, sl: (b * nq + i, h))
    cspec = pl.BlockSpec((None, None, ncp, HEAD_DIM), lambda b, h, i, sl: (b, h, 0, 0))
    kvspec = pl.BlockSpec((s_len, HEAD_DIM), lambda b, h, i, sl: (b, h))
    espec = pl.BlockSpec((V7X_LANES, s_len), lambda b, h, i, sl: (0, 0))
    kv_scr = pltpu.VMEM((s_len, HEAD_DIM), BF16)
    return pl.pallas_call(
        functools.partial(_nsa_prompt_kernel, tq=tq, gqa=gqa, s_len=s_len),
        grid_spec=pltpu.PrefetchScalarGridSpec(
            num_scalar_prefetch=1, grid=(nb, N_KV, nq),
            in_specs=[qspec, gspec, cspec, cspec, kvspec, kvspec, kvspec, kvspec, espec],
            out_specs=qspec,
            scratch_shapes=[kv_scr, kv_scr, kv_scr, kv_scr,
                            pltpu.VMEM((nq, tq, tq), F32),
                            pltpu.VMEM((rows, 1), F32), pltpu.VMEM((rows, 1), F32),
                            pltpu.VMEM((rows, HEAD_DIM), F32), pltpu.VMEM((rows, HEAD_DIM), F32),
                            pltpu.VMEM((rows, HEAD_DIM), F32)]),
        out_shape=jax.ShapeDtypeStruct((t, dq), BF16),
        compiler_params=_params(("arbitrary", "arbitrary", "arbitrary"), 48 << 20),
        name="nsa_prompt",
    )(slopes, q, gates, kc, vc, k_s, v_s, k_w, v_w, jnp.asarray(expand, BF16))


def _page_chunks_kernel(pt_ref, *refs, pages):
    k_pages = refs[:pages]
    v_pages = refs[pages:2 * pages]
    wk_ref, wv_ref, fk_ref, sk_ref, fv_ref, sv_ref = refs[2 * pages:]
    per = PAGE_SIZE // D_CMP
    for p in range(pages):
        rs = slice(p * per, (p + 1) * per)
        fk_ref[rs, :], sk_ref[rs, :] = _chunk_sums(k_pages[p][...], wk_ref[...])
        fv_ref[rs, :], sv_ref[rs, :] = _chunk_sums(v_pages[p][...], wv_ref[...])


def _page_chunks(page_table, pool_k, pool_v, wk, wv, pages=8):
    bd, n_pages = page_table.shape
    n_pool, _, n_kv, hd = pool_k.shape
    c = n_kv * hd
    per = PAGE_SIZE // D_CMP
    nch = n_pages * per
    assert n_pages % pages == 0
    pk = pool_k.reshape(n_pool, PAGE_SIZE, c)
    pv = pool_v.reshape(n_pool, PAGE_SIZE, c)

    def page_spec(p):
        return pl.BlockSpec((None, PAGE_SIZE, c), lambda b, t, pt: (pt[b, t * pages + p], 0, 0))
    wspec = pl.BlockSpec((L_CMP, c), lambda b, t, pt: (0, 0))
    ospec = pl.BlockSpec((None, pages * per, c), lambda b, t, pt: (b, t, 0))
    osh = jax.ShapeDtypeStruct((bd, nch, c), F32)
    return pl.pallas_call(
        functools.partial(_page_chunks_kernel, pages=pages),
        grid_spec=pltpu.PrefetchScalarGridSpec(
            num_scalar_prefetch=1, grid=(bd, n_pages // pages),
            in_specs=[page_spec(p) for p in range(pages)] * 2 + [wspec, wspec],
            out_specs=[ospec] * 4),
        out_shape=[osh] * 4,
        compiler_params=_params(("arbitrary", "arbitrary"), 32 << 20),
        name="page_chunks",
    )(page_table, *([pk] * pages), *([pv] * pages), wk.reshape(L_CMP, c), wv.reshape(L_CMP, c))


def _nsa_sample_cmp_kernel(q_ref, sl_ref, fk_ref, sk_ref, fv_ref, sv_ref, cover_ref, oc_ref, idx_ref, ok_ref,
                           *, past, n_sel_blocks):
    gqa = q_ref.shape[0]
    nch = fk_ref.shape[0]
    n_cmp = nch - 1
    nj = cover_ref.shape[1]
    q_pos = past
    kc = _blocks_from_sums(fk_ref[...], sk_ref[...]).astype(BF16)
    vc = _blocks_from_sums(fv_ref[...], sv_ref[...]).astype(BF16)
    n_io = lax.broadcasted_iota(jnp.int32, (gqa, nch), 1)
    dist = (q_pos - ((n_io + 2) * D_CMP - 1)).astype(F32)
    ok = (dist >= 0.0) & (n_io < n_cmp)
    s = jnp.where(ok, _nt_dot(q_ref[...], kc) * ATTN_SCALE - sl_ref[...] * dist, NEG)
    p = jnp.where(ok, jnp.exp(s - jnp.max(s, axis=1, keepdims=True)), 0.0)
    p = p / jnp.maximum(jnp.sum(p, axis=1, keepdims=True), 1e-30)
    oc_ref[...] = jnp.dot(p.astype(BF16), vc, preferred_element_type=F32)

    psum = jnp.broadcast_to(jnp.sum(p, axis=0, keepdims=True), (gqa, nch))
    imp = jnp.dot(psum, cover_ref[...], precision=lax.Precision.HIGHEST, preferred_element_type=F32)
    j_io = lax.broadcasted_iota(jnp.int32, (gqa, nj), 1)
    cur = q_pos // L_SEL
    valid = (j_io <= cur) & (j_io < n_sel_blocks)
    forced = (j_io == 0) | (j_io == cur) | (j_io == cur - 1)
    score = jnp.where(valid, imp + jnp.where(forced, FORCE_BONUS, 0.0), NEG)
    score = jnp.where(j_io < n_sel_blocks, score, -jnp.inf)
    lane = lax.broadcasted_iota(jnp.int32, (gqa, V7X_LANES), 1)
    j_f = j_io.astype(F32)
    idx = jnp.zeros((gqa, V7X_LANES), F32)
    okv = jnp.zeros((gqa, V7X_LANES), jnp.int32)
    for t in range(min(N_SEL, n_sel_blocks)):
        mx = jnp.max(score, axis=1, keepdims=True)
        am = jnp.min(jnp.where(score == mx, j_f, float(nj)), axis=1, keepdims=True)
        idx = jnp.where(lane == t, am, idx)
        okv = jnp.where(lane == t, jnp.where(mx > 0.5 * NEG, 1, 0), okv)
        score = jnp.where(j_f == am, -jnp.inf, score)
    idx_ref[...] = idx.astype(jnp.int32)
    ok_ref[...] = okv


def _nsa_sample_cmp(q, slopes_col, sums, past):
    bh, gqa, hd = q.shape
    fk = sums[0]
    bd, nch, c = fk.shape
    n_sel_blocks = past // L_SEL + 1
    nj = -(-n_sel_blocks // V7X_LANES) * V7X_LANES
    n_io = np.arange(nch)[:, None]
    j_io = np.arange(nj)[None, :]
    cover = ((n_io * D_CMP < j_io * L_SEL + L_SEL) & (n_io * D_CMP + L_CMP > j_io * L_SEL)
             & (n_io < nch - 1) & (j_io < n_sel_blocks)).astype(np.float32)
    hspec = pl.BlockSpec((None, gqa, hd), lambda b, h: (b * N_KV + h, 0, 0))
    sspec = pl.BlockSpec((None, nch, hd), lambda b, h: (b, 0, h))
    ispec = pl.BlockSpec((None, gqa, V7X_LANES), lambda b, h: (b * N_KV + h, 0, 0))
    return pl.pallas_call(
        functools.partial(_nsa_sample_cmp_kernel, past=past, n_sel_blocks=n_sel_blocks),
        grid=(bd, N_KV),
        in_specs=[hspec, pl.BlockSpec((None, gqa, 1), lambda b, h: (h, 0, 0)), sspec, sspec, sspec, sspec,
                  pl.BlockSpec((nch, nj), lambda b, h: (0, 0))],
        out_specs=[hspec, ispec, ispec],
        out_shape=[jax.ShapeDtypeStruct((bh, gqa, hd), F32),
                   jax.ShapeDtypeStruct((bh, gqa, V7X_LANES), jnp.int32),
                   jax.ShapeDtypeStruct((bh, gqa, V7X_LANES), jnp.int32)],
        compiler_params=_params(("arbitrary", "arbitrary"), 32 << 20),
        name="nsa_sample_cmp",
    )(q, slopes_col, *sums, jnp.asarray(cover))


def _nsa_sample_sel_kernel(pt_ref, idx_ref, okb_ref, *refs, past, nsb):
    k_blks = refs[:nsb]
    v_blks = refs[nsb:2 * nsb]
    (q_ref, sl_ref, gt_ref, oc_ref, ksn_ref, vsn_ref, kwc_ref, vwc_ref, kwn_ref, vwn_ref, o_ref) = refs[2 * nsb:]
    b = pl.program_id(0)
    h = pl.program_id(1)
    gqa, hd = q_ref.shape
    base = (b * N_KV + h) * nsb
    nb_past = past // L_SEL
    q_pos = past
    q = q_ref[...]
    slope = sl_ref[...]

    row_io = lax.broadcasted_iota(jnp.int32, (L_SEL, hd), 0)
    new_k = jnp.where(row_io == 0, jnp.broadcast_to(ksn_ref[...], (L_SEL, hd)), 0.0)
    new_v = jnp.where(row_io == 0, jnp.broadcast_to(vsn_ref[...], (L_SEL, hd)), 0.0)
    lane = lax.broadcasted_iota(jnp.int32, (gqa, nsb * L_SEL), 1)
    kpos = lane & (L_SEL - 1)
    okl = jnp.zeros((gqa, nsb * L_SEL), jnp.int32)
    kg, vg = [], []
    for t in range(nsb):
        bid = idx_ref[base + t]
        is_new = bid >= nb_past
        kg.append(jnp.where(is_new, new_k, k_blks[t][...]).astype(BF16))
        vg.append(jnp.where(is_new, new_v, v_blks[t][...]).astype(BF16))
        in_t = (lane >> SEL_SHIFT) == t
        kpos = kpos + jnp.where(in_t, bid * L_SEL, 0)
        okl = okl + jnp.where(in_t, okb_ref[base + t], 0)
    kg = jnp.concatenate(kg, axis=0)
    vg = jnp.concatenate(vg, axis=0)
    dist = (q_pos - kpos).astype(F32)
    ok = (dist >= 0.0) & (okl > 0)
    s = jnp.where(ok, _nt_dot(q, kg) * ATTN_SCALE - slope * dist, NEG)
    p = jnp.where(ok, jnp.exp(s - jnp.max(s, axis=1, keepdims=True)), 0.0)
    p = p / jnp.maximum(jnp.sum(p, axis=1, keepdims=True), 1e-30)
    o_s = jnp.dot(p.astype(BF16), vg, preferred_element_type=F32)

    wbuf = kwc_ref.shape[0]
    i_io = lax.broadcasted_iota(jnp.int32, (gqa, wbuf), 1)
    kpos_w = past - wbuf + i_io
    dist_w = (q_pos - kpos_w).astype(F32)
    ok_w = (dist_w >= 0.0) & (dist_w <= float(WINDOW)) & (kpos_w >= 0)
    s_w = jnp.where(ok_w, _nt_dot(q, kwc_ref[...].astype(BF16)) * ATTN_SCALE - slope * dist_w, NEG)
    kn = jnp.broadcast_to(kwn_ref[...], (V7X_SUBLANES, hd)).astype(BF16)
    s_n = (_nt_dot(q, kn) * ATTN_SCALE)[:, 0:1]
    m = jnp.maximum(jnp.max(s_w, axis=1, keepdims=True), s_n)
    p_w = jnp.where(ok_w, jnp.exp(s_w - m), 0.0)
    p_n = jnp.exp(s_n - m)
    den = jnp.maximum(jnp.sum(p_w, axis=1, keepdims=True) + p_n, 1e-30)
    p_w = p_w / den
    p_n = p_n / den
    o_w = (jnp.dot(p_w.astype(BF16), vwc_ref[...].astype(BF16), preferred_element_type=F32)
           + p_n.astype(BF16).astype(F32) * vwn_ref[...].astype(BF16).astype(F32))

    gt = gt_ref[...]
    o_ref[...] = (gt[:, 0:1] * oc_ref[...] + gt[:, 1:2] * o_s + gt[:, 2:3] * o_w).astype(BF16)


def _nsa_sample_sel(page_table, idx_flat, ok_flat, pool_k, pool_v, q, slopes_col, gates, o_c,
                    ks_new, vs_new, cache_kw, cache_vw, kw_new, vw_new, past):
    bh, gqa, hd = q.shape
    bd = bh // N_KV
    nsb = idx_flat.shape[0] // bh
    n_pool = pool_k.shape[0]
    c = N_KV * hd
    halves = PAGE_SIZE // L_SEL
    nb_past = past // L_SEL
    wbuf = cache_kw.shape[1]
    pk = pool_k.reshape(n_pool * halves, L_SEL, c)
    pv = pool_v.reshape(n_pool * halves, L_SEL, c)

    def blk_spec(t):
        def imap(b, h, pt, idx, okb):
            bid = jnp.minimum(idx[(b * N_KV + h) * nsb + t], nb_past - 1)
            return (pt[b, bid // halves] * halves + bid % halves, 0, h)
        return pl.BlockSpec((None, L_SEL, hd), imap)
    hspec = pl.BlockSpec((None, gqa, hd), lambda b, h, *_: (b * N_KV + h, 0, 0))
    nspec = pl.BlockSpec((None, 1, hd), lambda b, h, *_: (b, 0, h))
    wspec = pl.BlockSpec((None, wbuf, hd), lambda b, h, *_: (b, 0, h))
    return pl.pallas_call(
        functools.partial(_nsa_sample_sel_kernel, past=past, nsb=nsb),
        grid_spec=pltpu.PrefetchScalarGridSpec(
            num_scalar_prefetch=3, grid=(bd, N_KV),
            in_specs=[blk_spec(t) for t in range(nsb)] * 2
            + [hspec, pl.BlockSpec((None, gqa, 1), lambda b, h, *_: (h, 0, 0)),
               pl.BlockSpec((None, gqa, 3), lambda b, h, *_: (b * N_KV + h, 0, 0)), hspec,
               nspec, nspec, wspec, wspec, nspec, nspec],
            out_specs=hspec),
        out_shape=jax.ShapeDtypeStruct((bh, gqa, hd), BF16),
        compiler_params=_params(("arbitrary", "arbitrary"), 32 << 20),
        name="nsa_sample_sel",
    )(page_table, idx_flat, ok_flat, *([pk] * nsb), *([pv] * nsb), q, slopes_col, gates, o_c,
      ks_new.reshape(bd, 1, c), vs_new.reshape(bd, 1, c), cache_kw.reshape(bd, wbuf, c),
      cache_vw.reshape(bd, wbuf, c), kw_new.reshape(bd, 1, c), vw_new.reshape(bd, 1, c))


TM_MM = 1024
TM_ROWS = 256
TQ = 256


def kernel(x_prompt, x_sample, c_prompt, c_sample, page_table, cache_k_cmp, cache_v_cmp, cache_k_sel, cache_v_sel, cache_k_win, cache_v_win, state_h, state_conv, w_ada, b_ada, ln_g, ln_b, a_w_in, a_conv_w, a_conv_b, a_w_r, a_b_r, a_w_i, a_b_i, a_lambda, a_w_out, w_kv, w_cmp_k, w_cmp_v, b_w_qg, b_w_o, f_w_up, f_w_down):
    nb, s_len, d = x_prompt.shape
    bd, sd, _ = x_sample.shape
    assert sd == 1
    depth = w_ada.shape[0]
    n_a = a_w_in.shape[0]
    assert depth - n_a == 1 or depth == n_a, "one shared-KV NSA layer stack"
    d_ff = f_w_down.shape[1]
    lru = a_w_out.shape[1]
    n_heads = d // HEAD_DIM
    gqa = n_heads // N_KV
    ckv = N_KV * HEAD_DIM
    past = page_table.shape[1] * PAGE_SIZE
    alpha = (2.0 * depth) ** 0.25
    t = nb * s_len
    tm_mm = min(TM_MM, s_len)
    tm_rows = min(TM_ROWS, s_len)
    tn = 512

    rows = -(-(nb + bd) // V7X_SUBLANES) * V7X_SUBLANES
    c_all = jnp.concatenate([c_prompt, c_sample, jnp.zeros((rows - nb - bd, d), F32)], axis=0)
    mods = _ada(c_all, w_ada, b_ada)

    def mod_p(l, k):
        return mods[l, :nb, k * d:(k + 1) * d].reshape(nb, 1, d)

    def mod_s(l, k):
        return mods[l, nb:nb + bd, k * d:(k + 1) * d].reshape(1, bd, d)

    slopes = jnp.exp2(-8.0 * jnp.arange(1, n_heads + 1, dtype=F32) / n_heads)
    slopes_col = slopes.reshape(N_KV, gqa, 1)

    xp = x_prompt.reshape(t, d)
    xs = x_sample.reshape(bd, d)
    up = _modulate(xp, mod_p(0, 0), mod_p(0, 1), tm_rows)
    us = _modulate(xs, mod_s(0, 0), mod_s(0, 1), bd)

    def ffn(x_p, x_s, u_p, u_s, l, nxt):
        w_down = f_w_down[l].astype(BF16)
        outs = []
        for x, u, tm_m, tm_r, mod in ((x_p, u_p, tm_mm, tm_rows, mod_p), (x_s, u_s, bd, bd, mod_s)):
            hmid = _matmul(u, f_w_up, d_ff, layer=l, tm=tm_m, tn=256, out_dtype=BF16, epilogue=_swiglu_epilogue,
                           col_off=0, col_off2=d_ff, name="ffn_up")
            y = _matmul(hmid, w_down, d, tm=min(tm_m, 512), tn=tn, out_dtype=F32, name="ffn_down")
            if nxt is None:
                outs.append(_ln_mod(x, y, mod(l, 5), ln_g[l, 1], ln_b[l, 1], alpha, tm_r))
            else:
                outs.append(_ln_mod(x, y, mod(l, 5), ln_g[l, 1], ln_b[l, 1], alpha, tm_r,
                                    shift=mod(nxt[0], 0), scale=mod(nxt[0], 1), want_xb=nxt[1]))
        return outs

    h_p, h_s, cb_p, cb_s = [], [], [], []
    kv_p = kv_s = None
    for l in range(depth):
        if l < n_a:
            wr = a_w_r[l].astype(BF16)
            wi = a_w_i[l].astype(BF16)
            gelu_bf = lambda a: _gelu_tanh(a)
            ggx_p = _matmul(up, a_w_in, lru, layer=l, tm=tm_mm, tn=tn, out_dtype=BF16, epilogue=gelu_bf, name="lru_in_g")
            rx_p = _matmul(up, a_w_in, lru, layer=l, tm=tm_mm, tn=tn, out_dtype=F32, col_off=lru, name="lru_in_r")
            yin_p, hl_p, ct_p = _rglru_prompt(
                rx_p, ggx_p, jnp.zeros((nb, CONV_W - 1, lru), F32), jnp.zeros((nb, lru), F32),
                a_conv_w[l], a_conv_b[l], wr, a_b_r[l], wi, a_b_i[l], a_lambda[l], nb, min(TM_ROWS, s_len))
            ggx_s = _matmul(us, a_w_in, lru, layer=l, tm=bd, tn=tn, out_dtype=BF16, epilogue=gelu_bf, name="lru_in_g")
            rx_s = _matmul(us, a_w_in, lru, layer=l, tm=bd, tn=tn, out_dtype=F32, col_off=lru, name="lru_in_r")
            yin_s, hl_s, ct_s = _rglru_step(rx_s, ggx_s, jnp.swapaxes(state_conv[l], 0, 1), state_h[l],
                                            a_conv_w[l], a_conv_b[l], wr, a_b_r[l], wi, a_b_i[l], a_lambda[l])
            h_p.append(hl_p.reshape(nb, lru))
            h_s.append(hl_s)
            cb_p.append(ct_p)
            cb_s.append(jnp.swapaxes(ct_s, 0, 1))
            y_p = _matmul(yin_p, a_w_out, d, layer=l, tm=tm_mm, tn=tn, out_dtype=F32, name="lru_out")
            y_s = _matmul(yin_s, a_w_out, d, layer=l, tm=bd, tn=tn, out_dtype=F32, name="lru_out")
        else:
            j = l - n_a
            w_g = b_w_qg[j][:, n_heads * HEAD_DIM:].reshape(d, N_KV, 3 * gqa)
            w_g = jnp.pad(w_g, ((0, 0), (0, 0), (0, V7X_LANES - 3 * gqa))).reshape(d, N_KV * V7X_LANES)
            q_p = _matmul(up, b_w_qg, d, layer=j, tm=tm_mm, tn=tn, out_dtype=BF16, name="nsa_q")
            g_p = _matmul(up, w_g, N_KV * V7X_LANES, tm=tm_mm, tn=N_KV * V7X_LANES, out_dtype=F32,
                          epilogue=jax.nn.sigmoid, name="nsa_gate")
            o_p = _nsa_prompt(q_p, g_p, ctx_p[0], ctx_p[1], kv_p[2], kv_p[3], kv_p[4], kv_p[5], slopes, nb,
                              min(TQ, s_len))
            q_s = _matmul(us, b_w_qg, d, layer=j, tm=bd, tn=tn, out_dtype=BF16, name="nsa_q")
            g_s = _matmul(us, w_g, N_KV * V7X_LANES, tm=bd, tn=N_KV * V7X_LANES, out_dtype=F32,
                          epilogue=jax.nn.sigmoid, name="nsa_gate")
            q_s = q_s.reshape(bd * N_KV, gqa, HEAD_DIM)
            g_s = g_s.reshape(bd * N_KV, V7X_LANES)[:, :3 * gqa].reshape(bd * N_KV, gqa, 3)
            o_c, idx, okb = _nsa_sample_cmp(q_s, slopes_col, ctx_s, past)
            nsb = min(N_SEL, past // L_SEL + 1)
            o_s = _nsa_sample_sel(page_table, idx[:, 0, :nsb].reshape(-1), okb[:, 0, :nsb].reshape(-1),
                                  cache_k_sel, cache_v_sel, q_s, slopes_col, g_s, o_c,
                                  kv_s[2], kv_s[3], cache_k_win, cache_v_win, kv_s[4], kv_s[5], past)
            y_p = _matmul(o_p, b_w_o, d, layer=j, tm=tm_mm, tn=tn, out_dtype=F32, name="nsa_out")
            y_s = _matmul(o_s.reshape(bd, d), b_w_o, d, layer=j, tm=bd, tn=tn, out_dtype=F32, name="nsa_out")

        xp, up = _ln_mod(xp, y_p, mod_p(l, 2), ln_g[l, 0], ln_b[l, 0], alpha, tm_rows,
                         shift=mod_p(l, 3), scale=mod_p(l, 4))
        xs, us = _ln_mod(xs, y_s, mod_s(l, 2), ln_g[l, 0], ln_b[l, 0], alpha, bd,
                         shift=mod_s(l, 3), scale=mod_s(l, 4))
        last = l == depth - 1
        res_p, res_s = ffn(xp, xs, up, us, l, None if last else (l + 1, l == n_a - 1))
        if last:
            xp, xs = res_p[0], res_s[0]
        elif l == n_a - 1:
            xp, up, xbp = res_p
            xs, us, xbs = res_s
            kv_p = [_matmul(xbp, w_kv, ckv, tm=tm_mm, tn=ckv, out_dtype=F32, col_off=jj * ckv, name="kv_proj")
                    for jj in range(6)]
            kv_s = [_matmul(xbs, w_kv, ckv, tm=bd, tn=ckv, out_dtype=F32, col_off=jj * ckv, name="kv_proj")
                    for jj in range(6)]
            ctx_p = _cmp_blocks_prompt(kv_p[0], kv_p[1], w_cmp_k, w_cmp_v, nb)
            ctx_s = _page_chunks(page_table, cache_k_cmp, cache_v_cmp, w_cmp_k, w_cmp_v)
        else:
            xp, up = res_p
            xs, us = res_s

    wwin = min(WINDOW, s_len)
    shp_p = (nb, s_len, N_KV, HEAD_DIM)
    shp_s = (bd, sd, N_KV, HEAD_DIM)
    kvp = [a.reshape(shp_p) for a in kv_p]
    kvs = [a.reshape(shp_s) for a in kv_s]
    wbuf = cache_k_win.shape[1]
    k_win_s = jnp.concatenate([cache_k_win, kvs[4]], axis=1)[:, -wbuf:]
    v_win_s = jnp.concatenate([cache_v_win, kvs[5]], axis=1)[:, -wbuf:]
    return (xp.reshape(nb, s_len, d), xs.reshape(bd, sd, d),
            kvp[0], kvs[0], kvp[1], kvs[1], kvp[2], kvs[2], kvp[3], kvs[3],
            kvp[4][:, -wwin:], k_win_s, kvp[5][:, -wwin:], v_win_s,
            jnp.stack(h_p), jnp.stack(h_s), jnp.stack(cb_p), jnp.stack(cb_s))
```

```python
import functools

import numpy as np
import jax
import jax.numpy as jnp
from jax import lax
from jax.experimental import pallas as pl
from jax.experimental.pallas import tpu as pltpu

F32 = jnp.float32
BF16 = jnp.bfloat16

HEAD_DIM = 128
N_KV = 4
L_CMP = 32
D_CMP = 16
L_SEL = 64
N_SEL = 16
WINDOW = 512
PAGE_SIZE = 128
CONV_W = 4
LRU_BLOCKS = 16
LRU_C = 8.0
NEG = -1e30
FORCE_BONUS = 1e3
ATTN_SCALE = HEAD_DIM ** -0.5
LN_EPS = 1e-5
EXP2_SCALE = ATTN_SCALE * 1.4426950408889634
ATTN_ROW_CHUNK = 32
SEL_SHIFT = L_SEL.bit_length() - 1
assert 1 << SEL_SHIFT == L_SEL

V7X_VMEM_BYTES = 64 * 1024 * 1024
V7X_LANES = 128
V7X_SUBLANES = 8
VMEM_CAP = V7X_VMEM_BYTES - 8 * 1024 * 1024


def _params(sem, vmem_bytes):
    return pltpu.CompilerParams(dimension_semantics=sem,
                                vmem_limit_bytes=int(min(max(vmem_bytes, 16 << 20), VMEM_CAP)))


def _nt_dot(a, b, **kw):
    return lax.dot_general(a, b, (((1,), (1,)), ((), ())), preferred_element_type=F32, **kw)


def _ada_kernel(c_ref, w_ref, b_ref, o_ref):
    c = c_ref[...]
    s = (c * jax.nn.sigmoid(c)).astype(BF16)
    o_ref[...] = jnp.dot(s, w_ref[...].astype(BF16), preferred_element_type=F32) + b_ref[...]


def _ada(c_all, w_ada, b_ada, tn=1024):
    depth, d, n6 = w_ada.shape
    r = c_all.shape[0]
    return pl.pallas_call(
        _ada_kernel,
        grid=(depth, n6 // tn),
        in_specs=[pl.BlockSpec((r, d), lambda l, n: (0, 0)),
                  pl.BlockSpec((None, d, tn), lambda l, n: (l, 0, n)),
                  pl.BlockSpec((None, 1, tn), lambda l, n: (l, 0, n))],
        out_specs=pl.BlockSpec((None, r, tn), lambda l, n: (l, 0, n)),
        out_shape=jax.ShapeDtypeStruct((depth, r, n6), F32),
        compiler_params=_params(("arbitrary", "arbitrary"), 2 * d * tn * 4 + d * tn * 2 + (8 << 20)),
        name="ada",
    )(c_all, w_ada, b_ada.reshape(depth, 1, n6))


def _modulate_kernel(x_ref, sh_ref, sc_ref, u_ref):
    u_ref[...] = (x_ref[...] * (1.0 + sc_ref[...]) + sh_ref[...]).astype(u_ref.dtype)


def _row_specs(t, d, tm, nb):
    tiles_per_batch = (t // nb) // tm
    xspec = pl.BlockSpec((tm, d), lambda i: (i, 0))

    def vspec(r):
        return pl.BlockSpec((None, r, d), lambda i: (i // tiles_per_batch, 0, 0))
    return xspec, vspec


def _modulate(x, shift, scale, tm):
    t, d = x.shape
    nb, r, _ = shift.shape
    xspec, vspec = _row_specs(t, d, tm, nb)
    return pl.pallas_call(
        _modulate_kernel,
        grid=(t // tm,),
        in_specs=[xspec, vspec(r), vspec(r)],
        out_specs=xspec,
        out_shape=jax.ShapeDtypeStruct((t, d), BF16),
        compiler_params=_params(("arbitrary",), 6 * tm * d * 4),
        name="modulate",
    )(x, shift, scale)


def _ln_mod_kernel(*refs, alpha, want_u, want_xb):
    x_ref, y_ref, gate_ref, g_ref, b_ref = refs[:5]
    pos = 5
    if want_u:
        sh_ref, sc_ref = refs[pos:pos + 2]
        pos += 2
    xo_ref = refs[pos]
    pos += 1
    v = alpha * x_ref[...] + (1.0 + gate_ref[...]) * y_ref[...]
    mu = jnp.mean(v, axis=-1, keepdims=True)
    dlt = v - mu
    var = jnp.mean(dlt * dlt, axis=-1, keepdims=True)
    xn = dlt * lax.rsqrt(var + LN_EPS) * g_ref[...] + b_ref[...]
    xo_ref[...] = xn
    if want_u:
        refs[pos][...] = (xn * (1.0 + sc_ref[...]) + sh_ref[...]).astype(BF16)
        pos += 1
    if want_xb:
        refs[pos][...] = xn.astype(BF16)


def _ln_mod(x, y, gate, ln_g, ln_b, alpha, tm, shift=None, scale=None, want_xb=False):
    t, d = x.shape
    nb, r, _ = gate.shape
    xspec, vspec = _row_specs(t, d, tm, nb)
    pspec = pl.BlockSpec((1, d), lambda i: (0, 0))
    want_u = shift is not None
    ins = [x, y, gate, ln_g.reshape(1, d), ln_b.reshape(1, d)]
    in_specs = [xspec, xspec, vspec(r), pspec, pspec]
    out_shape = [jax.ShapeDtypeStruct((t, d), F32)]
    out_specs = [xspec]
    if want_u:
        ins += [shift, scale]
        in_specs += [vspec(r), vspec(r)]
        out_shape.append(jax.ShapeDtypeStruct((t, d), BF16))
        out_specs.append(xspec)
    if want_xb:
        out_shape.append(jax.ShapeDtypeStruct((t, d), BF16))
        out_specs.append(xspec)
    return pl.pallas_call(
        functools.partial(_ln_mod_kernel, alpha=alpha, want_u=want_u, want_xb=want_xb),
        grid=(t // tm,),
        in_specs=in_specs, out_specs=out_specs, out_shape=out_shape,
        compiler_params=_params(("arbitrary",), 12 * tm * d * 4),
        name="ln_mod",
    )(*ins)


def _gelu_tanh(x):
    return jax.nn.gelu(x, approximate=True)


def _mm_kernel(*refs, n_w, cast_w, epilogue):
    x_ref = refs[0]
    w_refs = refs[1:1 + n_w]
    o_ref = refs[1 + n_w]
    scr = refs[2 + n_w:]
    if cast_w == "scratch":
        @pl.when(pl.program_id(1) == 0)
        def _():
            for w_ref, s_ref in zip(w_refs, scr):
                s_ref[...] = w_ref[...].astype(BF16)
        ws = [s_ref[...] for s_ref in scr]
    elif cast_w == "inline":
        ws = [w_ref[...].astype(BF16) for w_ref in w_refs]
    else:
        ws = [w_ref[...] for w_ref in w_refs]
    x = x_ref[...]
    accs = [jnp.dot(x, w, preferred_element_type=F32) for w in ws]
    o_ref[...] = epilogue(*accs).astype(o_ref.dtype)


def _matmul(x, w, n_out, *, tm, tn, out_dtype, layer=None, epilogue=None, col_off=0, col_off2=None,
            name="mm"):
    m, k = x.shape
    assert m % tm == 0 and n_out % tn == 0 and col_off % tn == 0
    assert (w.ndim == 3) == (layer is not None) and w.shape[-2] == k
    offs = [col_off // tn] + ([] if col_off2 is None else [col_off2 // tn])
    n_w = len(offs)
    if epilogue is None:
        epilogue = lambda a: a
    m_tiles = m // tm
    if w.dtype == BF16:
        cast_w = "none"
    else:
        cast_w = "scratch" if m_tiles > 1 else "inline"
    scratch = [pltpu.VMEM((k, tn), BF16) for _ in range(n_w)] if cast_w == "scratch" else []
    w_bytes = k * tn * w.dtype.itemsize
    vmem = (2 * tm * k * 2 + n_w * (2 * w_bytes + k * tn * 2)
            + 2 * tm * tn * 4 * (n_w + 1) + (4 << 20))
    if layer is None:
        w_specs = [pl.BlockSpec((k, tn), functools.partial(lambda n, i, o: (0, n + o), o=o)) for o in offs]
    else:
        w_specs = [pl.BlockSpec((None, k, tn), functools.partial(lambda n, i, o: (layer, 0, n + o), o=o))
                   for o in offs]
    return pl.pallas_call(
        functools.partial(_mm_kernel, n_w=n_w, cast_w=cast_w, epilogue=epilogue),
        grid=(n_out // tn, m_tiles),
        in_specs=[pl.BlockSpec((tm, k), lambda n, i: (i, 0))] + w_specs,
        out_specs=pl.BlockSpec((tm, tn), lambda n, i: (i, n)),
        out_shape=jax.ShapeDtypeStruct((m, n_out), out_dtype),
        scratch_shapes=scratch,
        compiler_params=_params(("arbitrary", "arbitrary"), vmem),
        name=name,
    )(x, *([w] * n_w))


def _swiglu_epilogue(g, v):
    return g * jax.nn.sigmoid(g) * v


def _softplus_neg(lam):
    return jnp.maximum(-lam, 0.0) + jnp.log1p(jnp.exp(-jnp.abs(lam)))


def _lru_gates(xc, wr_ref, br_ref, wi_ref, bi_ref, sp, a_ref, b_ref, nblk, bw):
    xcb = xc.astype(BF16)
    for k in range(nblk):
        cs = slice(k * bw, (k + 1) * bw)
        xb = xcb[:, cs]
        r = jax.nn.sigmoid(jnp.dot(xb, wr_ref[k], preferred_element_type=F32) + br_ref[:, cs])
        ig = jax.nn.sigmoid(jnp.dot(xb, wi_ref[k], preferred_element_type=F32) + bi_ref[:, cs])
        log_a = (-LRU_C) * r * sp[:, cs]
        a_ref[:, cs] = jnp.exp(log_a)
        th = jnp.tanh(log_a)
        b_ref[:, cs] = jnp.sqrt(-2.0 * th / (1.0 - th)) * (ig * xc[:, cs])


SCAN_LANES = 4 * V7X_LANES


def _rglru_kernel(rx_ref, ggx_ref, cb0_ref, h0_ref, cw_ref, cbias_ref, wr_ref, br_ref, wi_ref, bi_ref,
                  lam_ref, y_ref, hlast_ref, ctail_ref, xpad_ref, a_ref, b_ref, hcar_ref, *, tm, nblk, bw):
    s = pl.program_id(1)
    width = nblk * bw
    pad = V7X_SUBLANES
    tail = CONV_W - 1

    @pl.when(s == 0)
    def _():
        xpad_ref[pad - tail:pad, :] = cb0_ref[...]
        hcar_ref[...] = h0_ref[...]

    rx = rx_ref[...]
    xpad_ref[pad:pad + tm, :] = rx
    cw = cw_ref[...]
    xc = cbias_ref[...] + rx * cw[tail:tail + 1]
    for j in range(tail):
        xc = xc + xpad_ref[pad - tail + j:pad - tail + j + tm, :] * cw[j:j + 1]
    xpad_ref[pad - tail:pad, :] = rx[tm - tail:tm, :]

    _lru_gates(xc, wr_ref, br_ref, wi_ref, bi_ref, _softplus_neg(lam_ref[...]), a_ref, b_ref, nblk, bw)

    row = lax.broadcasted_iota(jnp.int32, (V7X_SUBLANES, SCAN_LANES), 0)
    for c in range(width // SCAN_LANES):
        cs = slice(c * SCAN_LANES, (c + 1) * SCAN_LANES)

        def group(gi, h, cs=cs):
            r0 = pl.multiple_of(gi * V7X_SUBLANES, V7X_SUBLANES)
            a = a_ref[pl.ds(r0, V7X_SUBLANES), cs]
            b = b_ref[pl.ds(r0, V7X_SUBLANES), cs]
            for sft in (1, 2, 4):
                keep = row >= sft
                a_s = jnp.where(keep, pltpu.roll(a, sft, 0), 1.0)
                b_s = jnp.where(keep, pltpu.roll(b, sft, 0), 0.0)
                b = a * b_s + b
                a = a * a_s
            hs = a * h + b
            b_ref[pl.ds(r0, V7X_SUBLANES), cs] = hs
            return jnp.broadcast_to(hs[V7X_SUBLANES - 1:V7X_SUBLANES, :], hs.shape)

        h_in = jnp.broadcast_to(hcar_ref[:, cs], (V7X_SUBLANES, SCAN_LANES))
        h_out = lax.fori_loop(0, tm // V7X_SUBLANES, group, h_in)
        hcar_ref[:, cs] = h_out[0:1, :]

    y_ref[...] = (ggx_ref[...].astype(F32) * b_ref[...]).astype(BF16)

    @pl.when(s == pl.num_programs(1) - 1)
    def _():
        hlast_ref[...] = hcar_ref[...]
        ctail_ref[...] = rx[tm - tail:tm, :]


def _rglru_prompt(rx, ggx, conv_buf, h0, cw, cbias, wr, br, wi, bi, lam, nb, tm):
    t, width = rx.shape
    s_len = t // nb
    nblk, bw, _ = wr.shape
    tail = CONV_W - 1
    ns = s_len // tm
    row = lambda a: a.reshape(1, width)
    xspec = pl.BlockSpec((tm, width), lambda b, s: (b * ns + s, 0))
    pspec = pl.BlockSpec((1, width), lambda b, s: (0, 0))
    wspec = pl.BlockSpec((nblk, bw, bw), lambda b, s: (0, 0, 0))
    vmem = (2 * tm * width * (4 + 2 + 2) + 3 * (tm + 8) * width * 4 + 4 * nblk * bw * bw * 2 + (6 << 20))
    return pl.pallas_call(
        functools.partial(_rglru_kernel, tm=tm, nblk=nblk, bw=bw),
        grid=(nb, ns),
        in_specs=[xspec, xspec,
                  pl.BlockSpec((None, tail, width), lambda b, s: (b, 0, 0)),
                  pl.BlockSpec((None, 1, width), lambda b, s: (b, 0, 0)),
                  pl.BlockSpec((CONV_W, width), lambda b, s: (0, 0)), pspec,
                  wspec, pspec, wspec, pspec, pspec],
        out_specs=[xspec,
                   pl.BlockSpec((None, 1, width), lambda b, s: (b, 0, 0)),
                   pl.BlockSpec((None, tail, width), lambda b, s: (b, 0, 0))],
        out_shape=[jax.ShapeDtypeStruct((t, width), BF16),
                   jax.ShapeDtypeStruct((nb, 1, width), F32),
                   jax.ShapeDtypeStruct((nb, tail, width), F32)],
        scratch_shapes=[pltpu.VMEM((tm + V7X_SUBLANES, width), F32), pltpu.VMEM((tm, width), F32),
                        pltpu.VMEM((tm, width), F32), pltpu.VMEM((1, width), F32)],
        compiler_params=_params(("arbitrary", "arbitrary"), vmem),
        name="rglru",
    )(rx, ggx, conv_buf, h0.reshape(nb, 1, width), cw, row(cbias), wr, row(br), wi, row(bi), row(lam))


def _rglru_step_kernel(rx_ref, ggx_ref, cb_ref, h0_ref, cw_ref, cbias_ref, wr_ref, br_ref, wi_ref, bi_ref,
                       lam_ref, y_ref, h_ref, cnew_ref, a_ref, b_ref, *, nblk, bw):
    tail = CONV_W - 1
    rx = rx_ref[...]
    cw = cw_ref[...]
    xc = cbias_ref[...] + rx * cw[tail:tail + 1]
    for j in range(tail):
        xc = xc + cb_ref[j] * cw[j:j + 1]
    _lru_gates(xc, wr_ref, br_ref, wi_ref, bi_ref, _softplus_neg(lam_ref[...]), a_ref, b_ref, nblk, bw)
    h = a_ref[...] * h0_ref[...] + b_ref[...]
    h_ref[...] = h
    y_ref[...] = (ggx_ref[...].astype(F32) * h).astype(BF16)
    for j in range(tail - 1):
        cnew_ref[j] = cb_ref[j + 1]
    cnew_ref[tail - 1] = rx


def _rglru_step(rx, ggx, conv_t, h0, cw, cbias, wr, br, wi, bi, lam):
    bd, width = rx.shape
    nblk, bw, _ = wr.shape
    row = lambda a: a.reshape(1, width)
    return pl.pallas_call(
        functools.partial(_rglru_step_kernel, nblk=nblk, bw=bw),
        out_shape=[jax.ShapeDtypeStruct((bd, width), BF16), jax.ShapeDtypeStruct((bd, width), F32),
                   jax.ShapeDtypeStruct(conv_t.shape, F32)],
        scratch_shapes=[pltpu.VMEM((bd, width), F32), pltpu.VMEM((bd, width), F32)],
        compiler_params=pltpu.CompilerParams(vmem_limit_bytes=32 << 20),
        name="rglru_step",
    )(rx, ggx, conv_t, h0, cw, row(cbias), wr, row(br), wi, row(bi), row(lam))


def _chunk_sums(x, w):
    n = x.shape[0] // D_CMP
    x3 = x.reshape(n, D_CMP, x.shape[1])
    first = jnp.sum(x3 * w[None, :D_CMP, :], axis=1)
    second = jnp.sum(x3 * w[None, D_CMP:, :], axis=1)
    return first, second


def _blocks_from_sums(first, second):
    n = first.shape[0]
    nxt = pltpu.roll(second, n - 1, 0)
    rows = lax.broadcasted_iota(jnp.int32, first.shape, 0)
    return jnp.where(rows < n - 1, first + nxt, 0.0)


COEF_PIECES = 3
POS_LOW_BITS = 7


def _pos_lanes(pos):
    lane = lax.broadcasted_iota(jnp.int32, pos.shape, 1)
    hi = ((pos >> POS_LOW_BITS) << POS_LOW_BITS).astype(F32)
    lo = (pos & ((1 << POS_LOW_BITS) - 1)).astype(F32)
    return jnp.where(lane < COEF_PIECES, hi, jnp.where(lane < 2 * COEF_PIECES, lo, 0.0)).astype(BF16)


def _coef_lanes(coef, shape):
    v = jnp.full(shape, coef, F32)
    c1 = v.astype(BF16).astype(F32)
    c2 = (v - c1).astype(BF16).astype(F32)
    c3 = v - c1 - c2
    k = lax.broadcasted_iota(jnp.int32, shape, 1)
    k = jnp.where(k >= COEF_PIECES, k - COEF_PIECES, k)
    live = lax.broadcasted_iota(jnp.int32, shape, 1) < 2 * COEF_PIECES
    out = jnp.where(k == 0, c1, jnp.where(k == 1, c2, c3))
    return jnp.where(live, out, 0.0).astype(BF16)


def _cmp_blocks_kernel(k_ref, v_ref, wk_ref, wv_ref, kc_ref, vc_ref):
    hd = k_ref.shape[1]
    nch = kc_ref.shape[0]
    kc_ref[:, :hd] = _blocks_from_sums(*_chunk_sums(k_ref[...], wk_ref[...])).astype(BF16)
    ends = (lax.broadcasted_iota(jnp.int32, (nch, V7X_LANES), 0) + 2) * D_CMP - 1
    kc_ref[:, hd:] = _pos_lanes(ends)
    vc_ref[...] = _blocks_from_sums(*_chunk_sums(v_ref[...], wv_ref[...])).astype(BF16)


def _cmp_blocks_prompt(k_c, v_c, wk, wv, nb):
    t, c = k_c.shape
    s_len = t // nb
    nch = s_len // D_CMP
    kvspec = pl.BlockSpec((s_len, HEAD_DIM), lambda b, h: (b, h))
    wspec = pl.BlockSpec((L_CMP, HEAD_DIM), lambda b, h: (0, h))

    def ospec(w):
        return pl.BlockSpec((None, None, nch, w), lambda b, h: (b, h, 0, 0))
    return pl.pallas_call(
        _cmp_blocks_kernel, grid=(nb, N_KV),
        in_specs=[kvspec, kvspec, wspec, wspec],
        out_specs=[ospec(HEAD_DIM + V7X_LANES), ospec(HEAD_DIM)],
        out_shape=[jax.ShapeDtypeStruct((nb, N_KV, nch, HEAD_DIM + V7X_LANES), BF16),
                   jax.ShapeDtypeStruct((nb, N_KV, nch, HEAD_DIM), BF16)],
        compiler_params=_params(("arbitrary", "arbitrary"), 24 << 20),
        name="cmp_blocks",
    )(k_c, v_c, wk.reshape(L_CMP, c), wv.reshape(L_CMP, c))


def _block_scores_topk_t(imp_t, q0, nsel, tq):
    nsp = -(-nsel // V7X_SUBLANES) * V7X_SUBLANES
    jt = lax.broadcasted_iota(jnp.int32, (nsp, tq), 0)
    qpos = q0 + lax.broadcasted_iota(jnp.int32, (nsp, tq), 1)
    cur = qpos >> SEL_SHIFT
    valid = (jt <= cur) & (jt < nsel)
    forced = (jt == 0) | (jt == cur) | (jt == cur - 1)
    score = jnp.where(valid, imp_t[:nsp] + jnp.where(forced, FORCE_BONUS, 0.0), NEG)
    rank = jnp.zeros((nsp, tq), F32)
    for i in range(nsel):
        ri = score[i:i + 1, :]
        beats = (ri > score) | ((ri == score) & (jt > i))
        rank = rank + jnp.where(beats, 1.0, 0.0)
    return jnp.where((rank < float(min(N_SEL, nsel))) & valid, 1.0, 0.0)


def _nsa_prompt_kernel(slopes_ref, q_ref, gt_ref, kc_ref, vc_ref, ks_ref, vs_ref, kw_ref, vw_ref, e_ref,
                       o_ref, ksa, vsb, kwa, vwb, qa_ref, s_ref, p_ref, bsel_ref, bwin_ref, bcmp_ref,
                       m_ref, acc_ref, oc_ref, os_ref, ow_ref, psum_ref, *, tq, gqa, s_len):
    h = pl.program_id(1)
    i = pl.program_id(2)
    q0 = i * tq
    hd = HEAD_DIM
    ncp = kc_ref.shape[0]
    n_cmp = ncp - 1
    nsel = s_len // L_SEL
    nkt = s_len // tq
    rows = gqa * tq
    rc = ATTN_ROW_CHUNK
    per_g = tq // rc
    lane_tiles = tq // V7X_LANES
    r_io = lax.broadcasted_iota(jnp.int32, (tq, tq), 0)
    c_io = lax.broadcasted_iota(jnp.int32, (tq, tq), 1)

    @pl.when(i == 0)
    def _():
        pcols = _pos_lanes(lax.broadcasted_iota(jnp.int32, (s_len, V7X_LANES), 0))
        ksa[:, :hd] = ks_ref[...].astype(BF16)
        ksa[:, hd:] = pcols
        kwa[:, :hd] = kw_ref[...].astype(BF16)
        kwa[:, hd:] = pcols
        ones = jnp.ones((s_len, V7X_LANES), BF16)
        vsb[:, :hd] = vs_ref[...].astype(BF16)
        vsb[:, hd:] = ones
        vwb[:, :hd] = vw_ref[...].astype(BF16)
        vwb[:, hd:] = ones
        wshape = bwin_ref.shape[1:]
        rw = lax.broadcasted_iota(jnp.int32, wshape, 0)
        cw = lax.broadcasted_iota(jnp.int32, wshape, 1)
        for v in range(bwin_ref.shape[0]):
            dist = min(v * tq, WINDOW) + rw - cw
            bwin_ref[v] = jnp.where((dist >= 0) & (dist <= WINDOW), 0.0, NEG)

    for g in range(gqa):
        rs = slice(g * tq, (g + 1) * tq)
        qa_ref[rs, :hd] = q_ref[:, g * hd:(g + 1) * hd]
        qa_ref[rs, hd:] = _coef_lanes(slopes_ref[h * gqa + g] * (1.0 / ATTN_SCALE), (tq, V7X_LANES))

    def head_rows(c, g):
        if isinstance(c, int):
            return slice(g * tq + c * rc, g * tq + (c + 1) * rc)
        return pl.ds(pl.multiple_of(g * tq + c * rc, rc), rc)

    def tile_rows(c):
        if isinstance(c, int):
            return slice(c * rc, (c + 1) * rc)
        return pl.ds(pl.multiple_of(c * rc, rc), rc)

    lanes = [slice(j * V7X_LANES, (j + 1) * V7X_LANES) for j in range(lane_tiles)]

    s_ref[:, :ncp] = _nt_dot(qa_ref[...], kc_ref[...])
    n_io = lax.broadcasted_iota(jnp.int32, (tq, ncp), 1)
    qpos_c = q0 + lax.broadcasted_iota(jnp.int32, (tq, ncp), 0)
    bcmp_ref[...] = jnp.where((qpos_c >= (n_io + 2) * D_CMP - 1) & (n_io < n_cmp), 0.0, NEG)

    def cmp_chunk(c, carry):
        br = tile_rows(c)
        bias = bcmp_ref[br, :]
        live = bias == 0.0
        tot = jnp.zeros((rc, ncp), F32)
        for g in range(gqa):
            rr = head_rows(c, g)
            s = s_ref[rr, :ncp] + bias
            p = jnp.where(live, jnp.exp2((s - jnp.max(s, axis=1, keepdims=True)) * EXP2_SCALE), 0.0)
            p = p / jnp.maximum(jnp.sum(p, axis=1, keepdims=True), 1e-30)
            tot = tot + p
            p_ref[rr, :ncp] = p.astype(BF16)
        psum_ref[br, :] = tot
        return carry

    for c in range(per_g):
        cmp_chunk(c, 0)
    oc_ref[...] = jnp.dot(p_ref[:, :ncp], vc_ref[...], preferred_element_type=F32)
    psum = psum_ref[...]

    j_io = lax.broadcasted_iota(jnp.int32, (V7X_LANES, ncp), 0)
    n_io2 = lax.broadcasted_iota(jnp.int32, (V7X_LANES, ncp), 1)
    cover_t = jnp.where((n_io2 * D_CMP < j_io * L_SEL + L_SEL) & (n_io2 * D_CMP + L_CMP > j_io * L_SEL)
                        & (n_io2 < n_cmp) & (j_io < nsel), 1.0, 0.0)
    imp_t = _nt_dot(cover_t, psum, precision=lax.Precision.HIGHEST)
    sel_t = _block_scores_topk_t(imp_t, q0, nsel, tq)
    sel_t = jnp.concatenate([sel_t, jnp.zeros((V7X_LANES - sel_t.shape[0], tq), F32)], axis=0)
    sel = sel_t.T.astype(BF16)
    for kt in range(nkt):
        @pl.when(kt <= i)
        def _(kt=kt):
            picked = jnp.dot(sel, e_ref[:, kt * tq:(kt + 1) * tq], preferred_element_type=F32)
            causal_slack = jnp.where(kt < i, tq, 0)
            bsel_ref[kt] = jnp.where((picked > 0.5) & (c_io <= r_io + causal_slack), 0.0, NEG)

    def softmax_passes(width, bias_ref, first):
        lanes = [slice(j * V7X_LANES, (j + 1) * V7X_LANES) for j in range(width // V7X_LANES)]
        for c in range(per_g):
            br = tile_rows(c)
            bias = [bias_ref[br, ls] for ls in lanes]
            for g in range(gqa):
                rr = head_rows(c, g)
                top = functools.reduce(jnp.maximum, [s_ref[rr, ls] + b for ls, b in zip(lanes, bias)])
                m_new = jnp.max(top, axis=1, keepdims=True)
                if first:
                    m_ref[rr, :] = jnp.broadcast_to(m_new, (rc, V7X_LANES))
                else:
                    m_prev = m_ref[rr, :]
                    m_new = jnp.maximum(m_prev, m_new)
                    alpha = jnp.exp2((m_prev - m_new) * EXP2_SCALE)
                    m_ref[rr, :] = m_new
                    acc_ref[rr, :hd] = alpha * acc_ref[rr, :hd]
                    acc_ref[rr, hd:] = alpha * acc_ref[rr, hd:]
        for c in range(per_g):
            br = tile_rows(c)
            bias = [bias_ref[br, ls] for ls in lanes]
            for g in range(gqa):
                rr = head_rows(c, g)
                m_new = m_ref[rr, :]
                for ls, b in zip(lanes, bias):
                    p_ref[rr, ls] = jnp.exp2((s_ref[rr, ls] + b - m_new) * EXP2_SCALE).astype(BF16)

    def sel_tile(kt, first):
        k0 = kt * tq if first else pl.multiple_of(kt * tq, tq)
        s_ref[:, :tq] = _nt_dot(qa_ref[...], ksa[pl.ds(k0, tq), :])
        softmax_passes(tq, bsel_ref.at[kt], first)
        pv = jnp.dot(p_ref[:, :tq], vsb[pl.ds(k0, tq), :], preferred_element_type=F32)
        if first:
            acc_ref[...] = pv
        else:
            acc_ref[...] += pv

    sel_tile(0, True)

    def sel_rest(kt, carry):
        sel_tile(kt, False)
        return carry

    lax.fori_loop(1, i + 1, sel_rest, 0)
    os_ref[...] = acc_ref[:, :hd] / jnp.maximum(acc_ref[:, hd:], 1e-30)

    wk = bwin_ref.shape[2]
    k0w = pl.multiple_of(jnp.maximum(q0 - WINDOW, 0), tq)
    s_ref[...] = _nt_dot(qa_ref[...], kwa[pl.ds(k0w, wk), :])
    softmax_passes(wk, bwin_ref.at[jnp.minimum(i, bwin_ref.shape[0] - 1)], True)
    pv = jnp.dot(p_ref[...], vwb[pl.ds(k0w, wk), :], preferred_element_type=F32)
    ow_ref[...] = pv[:, :hd] / jnp.maximum(pv[:, hd:], 1e-30)

    gt = gt_ref[...]
    for g in range(gqa):
        rs = slice(g * tq, (g + 1) * tq)
        o = (gt[:, 3 * g:3 * g + 1] * oc_ref[rs, :] + gt[:, 3 * g + 1:3 * g + 2] * os_ref[rs, :]
             + gt[:, 3 * g + 2:3 * g + 3] * ow_ref[rs, :])
        o_ref[:, g * hd:(g + 1) * hd] = o.astype(BF16)


def _nsa_prompt(q, gates, kc, vc, k_s, v_s, k_w, v_w, slopes, nb, tq):
    t, dq = q.shape
    s_len = t // nb
    gqa = dq // (N_KV * HEAD_DIM)
    nq = s_len // tq
    ncp = kc.shape[2]
    assert WINDOW % tq == 0 and s_len % tq == 0 and s_len // L_SEL <= V7X_LANES
    expand = np.zeros((V7X_LANES, s_len), np.float32)
    expand[np.arange(s_len) // L_SEL, np.arange(s_len)] = 1.0
    assert ncp == V7X_LANES and tq % ATTN_ROW_CHUNK == 0 and (tq // ATTN_ROW_CHUNK) & (tq // ATTN_ROW_CHUNK - 1) == 0
    rows = gqa * tq
    hd = HEAD_DIM
    qspec = pl.BlockSpec((tq, gqa * hd), lambda b, h, i, sl: (b * nq + i, h))
    gspec = pl.BlockSpec((tq, V7X_LANES), lambda b, h, i, sl: (b * nq + i, h))

    def cspec(w):
        return pl.BlockSpec((None, None, ncp, w), lambda b, h, i, sl: (b, h, 0, 0))
    kvspec = pl.BlockSpec((s_len, hd), lambda b, h, i, sl: (b, h))
    espec = pl.BlockSpec((V7X_LANES, s_len), lambda b, h, i, sl: (0, 0))
    ka_scr = pltpu.VMEM((s_len, hd + V7X_LANES), BF16)
    v_scr = pltpu.VMEM((s_len, hd + V7X_LANES), BF16)
    row_scr = pltpu.VMEM((rows, hd), F32)
    return pl.pallas_call(
        functools.partial(_nsa_prompt_kernel, tq=tq, gqa=gqa, s_len=s_len),
        grid_spec=pltpu.PrefetchScalarGridSpec(
            num_scalar_prefetch=1, grid=(nb, N_KV, nq),
            in_specs=[qspec, gspec, cspec(hd + V7X_LANES), cspec(hd), kvspec, kvspec, kvspec, kvspec, espec],
            out_specs=qspec,
            scratch_shapes=[ka_scr, v_scr, ka_scr, v_scr,
                            pltpu.VMEM((rows, hd + V7X_LANES), BF16),
                            pltpu.VMEM((rows, WINDOW + tq), F32), pltpu.VMEM((rows, WINDOW + tq), BF16),
                            pltpu.VMEM((nq, tq, tq), F32),
                            pltpu.VMEM((WINDOW // tq + 1, tq, WINDOW + tq), F32),
                            pltpu.VMEM((tq, ncp), F32),
                            pltpu.VMEM((rows, V7X_LANES), F32),
                            pltpu.VMEM((rows, hd + V7X_LANES), F32),
                            row_scr, row_scr, row_scr,
                            pltpu.VMEM((tq, ncp), F32)]),
        out_shape=jax.ShapeDtypeStruct((t, dq), BF16),
        compiler_params=_params(("arbitrary", "arbitrary", "arbitrary"), 48 << 20),
        name="nsa_prompt",
    )(slopes, q, gates, kc, vc, k_s, v_s, k_w, v_w, jnp.asarray(expand, BF16))


def _page_chunks_kernel(pt_ref, *refs, pages):
    k_pages = refs[:pages]
    v_pages = refs[pages:2 * pages]
    wk_ref, wv_ref, fk_ref, sk_ref, fv_ref, sv_ref = refs[2 * pages:]
    per = PAGE_SIZE // D_CMP

    def sums(x, w):
        x4 = x.reshape((per, D_CMP) + x.shape[1:])
        return jnp.sum(x4 * w[None, :D_CMP], axis=1), jnp.sum(x4 * w[None, D_CMP:], axis=1)

    for p in range(pages):
        rs = slice(p * per, (p + 1) * per)
        fk_ref[rs], sk_ref[rs] = sums(k_pages[p][...], wk_ref[...])
        fv_ref[rs], sv_ref[rs] = sums(v_pages[p][...], wv_ref[...])


def _page_chunks(page_table, pool_k, pool_v, wk, wv, pages=8):
    bd, n_pages = page_table.shape
    n_pool, _, n_kv, hd = pool_k.shape
    per = PAGE_SIZE // D_CMP
    nch = n_pages * per
    assert n_pages % pages == 0

    def page_spec(p):
        return pl.BlockSpec((None, PAGE_SIZE, n_kv, hd), lambda b, t, pt: (pt[b, t * pages + p], 0, 0, 0))
    wspec = pl.BlockSpec((L_CMP, n_kv, hd), lambda b, t, pt: (0, 0, 0))
    ospec = pl.BlockSpec((None, pages * per, n_kv, hd), lambda b, t, pt: (b, t, 0, 0))
    osh = jax.ShapeDtypeStruct((bd, nch, n_kv, hd), F32)
    return pl.pallas_call(
        functools.partial(_page_chunks_kernel, pages=pages),
        grid_spec=pltpu.PrefetchScalarGridSpec(
            num_scalar_prefetch=1, grid=(bd, n_pages // pages),
            in_specs=[page_spec(p) for p in range(pages)] * 2 + [wspec, wspec],
            out_specs=[ospec] * 4),
        out_shape=[osh] * 4,
        compiler_params=_params(("arbitrary", "arbitrary"), 40 << 20),
        name="page_chunks",
    )(page_table, *([pool_k] * pages), *([pool_v] * pages), wk, wv)


def _nsa_sample_cmp_kernel(q_ref, sl_ref, fk_ref, sk_ref, fv_ref, sv_ref, cover_ref, oc_ref, idx_ref, ok_ref,
                           *, past, n_sel_blocks):
    gqa = q_ref.shape[0]
    nch = fk_ref.shape[0]
    n_cmp = nch - 1
    nj = cover_ref.shape[1]
    q_pos = past
    h = pl.program_id(1)
    head = lambda ref: ref[:, pl.ds(h, 1), :].reshape(nch, ref.shape[-1])
    kc = _blocks_from_sums(head(fk_ref), head(sk_ref)).astype(BF16)
    vc = _blocks_from_sums(head(fv_ref), head(sv_ref)).astype(BF16)
    n_io = lax.broadcasted_iota(jnp.int32, (gqa, nch), 1)
    dist = (q_pos - ((n_io + 2) * D_CMP - 1)).astype(F32)
    ok = (dist >= 0.0) & (n_io < n_cmp)
    s = jnp.where(ok, _nt_dot(q_ref[...], kc) * ATTN_SCALE - sl_ref[...] * dist, NEG)
    p = jnp.where(ok, jnp.exp(s - jnp.max(s, axis=1, keepdims=True)), 0.0)
    p = p / jnp.maximum(jnp.sum(p, axis=1, keepdims=True), 1e-30)
    oc_ref[...] = jnp.dot(p.astype(BF16), vc, preferred_element_type=F32)

    psum = jnp.broadcast_to(jnp.sum(p, axis=0, keepdims=True), (gqa, nch))
    imp = jnp.dot(psum, cover_ref[...], precision=lax.Precision.HIGHEST, preferred_element_type=F32)
    j_io = lax.broadcasted_iota(jnp.int32, (gqa, nj), 1)
    cur = q_pos // L_SEL
    valid = (j_io <= cur) & (j_io < n_sel_blocks)
    forced = (j_io == 0) | (j_io == cur) | (j_io == cur - 1)
    score = jnp.where(valid, imp + jnp.where(forced, FORCE_BONUS, 0.0), NEG)
    score = jnp.where(j_io < n_sel_blocks, score, -jnp.inf)
    lane = lax.broadcasted_iota(jnp.int32, (gqa, V7X_LANES), 1)
    j_f = j_io.astype(F32)
    idx = jnp.zeros((gqa, V7X_LANES), F32)
    okv = jnp.zeros((gqa, V7X_LANES), jnp.int32)
    for t in range(min(N_SEL, n_sel_blocks)):
        mx = jnp.max(score, axis=1, keepdims=True)
        am = jnp.min(jnp.where(score == mx, j_f, float(nj)), axis=1, keepdims=True)
        idx = jnp.where(lane == t, am, idx)
        okv = jnp.where(lane == t, jnp.where(mx > 0.5 * NEG, 1, 0), okv)
        score = jnp.where(j_f == am, -jnp.inf, score)
    idx_ref[...] = idx.astype(jnp.int32)
    ok_ref[...] = okv


def _nsa_sample_cmp(q, slopes_col, sums, past):
    bh, gqa, hd = q.shape
    fk = sums[0]
    bd, nch, n_kv, _ = fk.shape
    n_sel_blocks = past // L_SEL + 1
    nj = -(-n_sel_blocks // V7X_LANES) * V7X_LANES
    n_io = np.arange(nch)[:, None]
    j_io = np.arange(nj)[None, :]
    cover = ((n_io * D_CMP < j_io * L_SEL + L_SEL) & (n_io * D_CMP + L_CMP > j_io * L_SEL)
             & (n_io < nch - 1) & (j_io < n_sel_blocks)).astype(np.float32)
    hspec = pl.BlockSpec((None, gqa, hd), lambda b, h: (b * N_KV + h, 0, 0))
    sspec = pl.BlockSpec((None, nch, n_kv, hd), lambda b, h: (b, 0, 0, 0))
    ispec = pl.BlockSpec((None, gqa, V7X_LANES), lambda b, h: (b * N_KV + h, 0, 0))
    return pl.pallas_call(
        functools.partial(_nsa_sample_cmp_kernel, past=past, n_sel_blocks=n_sel_blocks),
        grid=(bd, N_KV),
        in_specs=[hspec, pl.BlockSpec((None, gqa, 1), lambda b, h: (h, 0, 0)), sspec, sspec, sspec, sspec,
                  pl.BlockSpec((nch, nj), lambda b, h: (0, 0))],
        out_specs=[hspec, ispec, ispec],
        out_shape=[jax.ShapeDtypeStruct((bh, gqa, hd), F32),
                   jax.ShapeDtypeStruct((bh, gqa, V7X_LANES), jnp.int32),
                   jax.ShapeDtypeStruct((bh, gqa, V7X_LANES), jnp.int32)],
        compiler_params=_params(("arbitrary", "arbitrary"), 48 << 20),
        name="nsa_sample_cmp",
    )(q, slopes_col, *sums, jnp.asarray(cover))


def _nsa_sample_sel_kernel(pt_ref, idx_ref, okb_ref, *refs, past, nsb):
    k_blks = refs[:nsb]
    v_blks = refs[nsb:2 * nsb]
    (q_ref, sl_ref, gt_ref, oc_ref, ksn_ref, vsn_ref, kwc_ref, vwc_ref, kwn_ref, vwn_ref, o_ref) = refs[2 * nsb:]
    b = pl.program_id(0)
    h = pl.program_id(1)
    gqa, hd = q_ref.shape
    base = (b * N_KV + h) * nsb
    nb_past = past // L_SEL
    q_pos = past
    q = q_ref[...]
    slope = sl_ref[...]
    head = lambda ref: ref[:, pl.ds(h, 1), :].reshape(ref.shape[0], hd)

    row_io = lax.broadcasted_iota(jnp.int32, (L_SEL, hd), 0)
    new_k = jnp.where(row_io == 0, jnp.broadcast_to(ksn_ref[...], (L_SEL, hd)), 0.0)
    new_v = jnp.where(row_io == 0, jnp.broadcast_to(vsn_ref[...], (L_SEL, hd)), 0.0)
    lane = lax.broadcasted_iota(jnp.int32, (gqa, nsb * L_SEL), 1)
    kpos = lane & (L_SEL - 1)
    okl = jnp.zeros((gqa, nsb * L_SEL), jnp.int32)
    kg, vg = [], []
    for t in range(nsb):
        bid = idx_ref[base + t]
        is_new = bid >= nb_past
        kg.append(jnp.where(is_new, new_k, head(k_blks[t])).astype(BF16))
        vg.append(jnp.where(is_new, new_v, head(v_blks[t])).astype(BF16))
        in_t = (lane >> SEL_SHIFT) == t
        kpos = kpos + jnp.where(in_t, bid * L_SEL, 0)
        okl = okl + jnp.where(in_t, okb_ref[base + t], 0)
    kg = jnp.concatenate(kg, axis=0)
    vg = jnp.concatenate(vg, axis=0)
    dist = (q_pos - kpos).astype(F32)
    ok = (dist >= 0.0) & (okl > 0)
    s = jnp.where(ok, _nt_dot(q, kg) * ATTN_SCALE - slope * dist, NEG)
    p = jnp.where(ok, jnp.exp(s - jnp.max(s, axis=1, keepdims=True)), 0.0)
    p = p / jnp.maximum(jnp.sum(p, axis=1, keepdims=True), 1e-30)
    o_s = jnp.dot(p.astype(BF16), vg, preferred_element_type=F32)

    wbuf = kwc_ref.shape[0]
    i_io = lax.broadcasted_iota(jnp.int32, (gqa, wbuf), 1)
    kpos_w = past - wbuf + i_io
    dist_w = (q_pos - kpos_w).astype(F32)
    ok_w = (dist_w >= 0.0) & (dist_w <= float(WINDOW)) & (kpos_w >= 0)
    s_w = jnp.where(ok_w, _nt_dot(q, head(kwc_ref).astype(BF16)) * ATTN_SCALE - slope * dist_w, NEG)
    kn = jnp.broadcast_to(kwn_ref[...], (V7X_SUBLANES, hd)).astype(BF16)
    s_n = (_nt_dot(q, kn) * ATTN_SCALE)[:, 0:1]
    m = jnp.maximum(jnp.max(s_w, axis=1, keepdims=True), s_n)
    p_w = jnp.where(ok_w, jnp.exp(s_w - m), 0.0)
    p_n = jnp.exp(s_n - m)
    den = jnp.maximum(jnp.sum(p_w, axis=1, keepdims=True) + p_n, 1e-30)
    p_w = p_w / den
    p_n = p_n / den
    o_w = (jnp.dot(p_w.astype(BF16), head(vwc_ref).astype(BF16), preferred_element_type=F32)
           + p_n.astype(BF16).astype(F32) * vwn_ref[...].astype(BF16).astype(F32))

    gt = gt_ref[...]
    o_ref[...] = (gt[:, 0:1] * oc_ref[...] + gt[:, 1:2] * o_s + gt[:, 2:3] * o_w).astype(BF16)


def _nsa_sample_sel(page_table, idx_flat, ok_flat, pool_k, pool_v, q, slopes_col, gates, o_c,
                    ks_new, vs_new, cache_kw, cache_vw, kw_new, vw_new, past):
    bh, gqa, hd = q.shape
    bd = bh // N_KV
    nsb = idx_flat.shape[0] // bh
    c = N_KV * hd
    halves = PAGE_SIZE // L_SEL
    nb_past = past // L_SEL
    wbuf = cache_kw.shape[1]

    def blk_spec(t):
        def imap(b, h, pt, idx, okb):
            bid = jnp.minimum(idx[(b * N_KV + h) * nsb + t], nb_past - 1)
            return (pt[b, bid // halves], bid % halves, 0, 0)
        return pl.BlockSpec((None, L_SEL, N_KV, hd), imap)
    hspec = pl.BlockSpec((None, gqa, hd), lambda b, h, *_: (b * N_KV + h, 0, 0))
    nspec = pl.BlockSpec((None, 1, hd), lambda b, h, *_: (b, 0, h))
    wspec = pl.BlockSpec((None, wbuf, N_KV, hd), lambda b, h, *_: (b, 0, 0, 0))
    return pl.pallas_call(
        functools.partial(_nsa_sample_sel_kernel, past=past, nsb=nsb),
        grid_spec=pltpu.PrefetchScalarGridSpec(
            num_scalar_prefetch=3, grid=(bd, N_KV),
            in_specs=[blk_spec(t) for t in range(nsb)] * 2
            + [hspec, pl.BlockSpec((None, gqa, 1), lambda b, h, *_: (h, 0, 0)),
               pl.BlockSpec((None, gqa, 3), lambda b, h, *_: (b * N_KV + h, 0, 0)), hspec,
               nspec, nspec, wspec, wspec, nspec, nspec],
            out_specs=hspec),
        out_shape=jax.ShapeDtypeStruct((bh, gqa, hd), BF16),
        compiler_params=_params(("arbitrary", "arbitrary"), 40 << 20),
        name="nsa_sample_sel",
    )(page_table, idx_flat, ok_flat, *([pool_k] * nsb), *([pool_v] * nsb), q, slopes_col, gates, o_c,
      ks_new.reshape(bd, 1, c), vs_new.reshape(bd, 1, c), cache_kw, cache_vw,
      kw_new.reshape(bd, 1, c), vw_new.reshape(bd, 1, c))


TM_MM = 1024
TM_ROWS = 256
TQ = 256


def kernel(x_prompt, x_sample, c_prompt, c_sample, page_table, cache_k_cmp, cache_v_cmp, cache_k_sel, cache_v_sel, cache_k_win, cache_v_win, state_h, state_conv, w_ada, b_ada, ln_g, ln_b, a_w_in, a_conv_w, a_conv_b, a_w_r, a_b_r, a_w_i, a_b_i, a_lambda, a_w_out, w_kv, w_cmp_k, w_cmp_v, b_w_qg, b_w_o, f_w_up, f_w_down):
    nb, s_len, d = x_prompt.shape
    bd, sd, _ = x_sample.shape
    assert sd == 1
    depth = w_ada.shape[0]
    n_a = a_w_in.shape[0]
    assert depth - n_a == 1 or depth == n_a, "one shared-KV NSA layer stack"
    d_ff = f_w_down.shape[1]
    lru = a_w_out.shape[1]
    n_heads = d // HEAD_DIM
    gqa = n_heads // N_KV
    ckv = N_KV * HEAD_DIM
    past = page_table.shape[1] * PAGE_SIZE
    alpha = (2.0 * depth) ** 0.25
    t = nb * s_len
    tm_mm = min(TM_MM, s_len)
    tm_rows = min(TM_ROWS, s_len)
    tn = 512

    rows = -(-(nb + bd) // V7X_SUBLANES) * V7X_SUBLANES
    c_all = jnp.concatenate([c_prompt, c_sample, jnp.zeros((rows - nb - bd, d), F32)], axis=0)
    mods = _ada(c_all, w_ada, b_ada)

    def mod_p(l, k):
        return mods[l, :nb, k * d:(k + 1) * d].reshape(nb, 1, d)

    def mod_s(l, k):
        return mods[l, nb:nb + bd, k * d:(k + 1) * d].reshape(1, bd, d)

    slopes = jnp.exp2(-8.0 * jnp.arange(1, n_heads + 1, dtype=F32) / n_heads)
    slopes_col = slopes.reshape(N_KV, gqa, 1)

    xp = x_prompt.reshape(t, d)
    xs = x_sample.reshape(bd, d)
    up = _modulate(xp, mod_p(0, 0), mod_p(0, 1), tm_rows)
    us = _modulate(xs, mod_s(0, 0), mod_s(0, 1), bd)

    def ffn(x_p, x_s, u_p, u_s, l, nxt):
        w_down = f_w_down[l].astype(BF16)
        outs = []
        for x, u, tm_m, tm_r, mod in ((x_p, u_p, tm_mm, tm_rows, mod_p), (x_s, u_s, bd, bd, mod_s)):
            hmid = _matmul(u, f_w_up, d_ff, layer=l, tm=tm_m, tn=256, out_dtype=BF16, epilogue=_swiglu_epilogue,
                           col_off=0, col_off2=d_ff, name="ffn_up")
            y = _matmul(hmid, w_down, d, tm=min(tm_m, 512), tn=tn, out_dtype=F32, name="ffn_down")
            if nxt is None:
                outs.append(_ln_mod(x, y, mod(l, 5), ln_g[l, 1], ln_b[l, 1], alpha, tm_r))
            else:
                outs.append(_ln_mod(x, y, mod(l, 5), ln_g[l, 1], ln_b[l, 1], alpha, tm_r,
                                    shift=mod(nxt[0], 0), scale=mod(nxt[0], 1), want_xb=nxt[1]))
        return outs

    h_p, h_s, cb_p, cb_s = [], [], [], []
    kv_p = kv_s = None
    for l in range(depth):
        if l < n_a:
            wr = a_w_r[l].astype(BF16)
            wi = a_w_i[l].astype(BF16)
            gelu_bf = lambda a: _gelu_tanh(a)
            ggx_p = _matmul(up, a_w_in, lru, layer=l, tm=tm_mm, tn=tn, out_dtype=BF16, epilogue=gelu_bf, name="lru_in_g")
            rx_p = _matmul(up, a_w_in, lru, layer=l, tm=tm_mm, tn=tn, out_dtype=F32, col_off=lru, name="lru_in_r")
            yin_p, hl_p, ct_p = _rglru_prompt(
                rx_p, ggx_p, jnp.zeros((nb, CONV_W - 1, lru), F32), jnp.zeros((nb, lru), F32),
                a_conv_w[l], a_conv_b[l], wr, a_b_r[l], wi, a_b_i[l], a_lambda[l], nb, min(TM_ROWS, s_len))
            ggx_s = _matmul(us, a_w_in, lru, layer=l, tm=bd, tn=tn, out_dtype=BF16, epilogue=gelu_bf, name="lru_in_g")
            rx_s = _matmul(us, a_w_in, lru, layer=l, tm=bd, tn=tn, out_dtype=F32, col_off=lru, name="lru_in_r")
            yin_s, hl_s, ct_s = _rglru_step(rx_s, ggx_s, jnp.swapaxes(state_conv[l], 0, 1), state_h[l],
                                            a_conv_w[l], a_conv_b[l], wr, a_b_r[l], wi, a_b_i[l], a_lambda[l])
            h_p.append(hl_p.reshape(nb, lru))
            h_s.append(hl_s)
            cb_p.append(ct_p)
            cb_s.append(jnp.swapaxes(ct_s, 0, 1))
            y_p = _matmul(yin_p, a_w_out, d, layer=l, tm=tm_mm, tn=tn, out_dtype=F32, name="lru_out")
            y_s = _matmul(yin_s, a_w_out, d, layer=l, tm=bd, tn=tn, out_dtype=F32, name="lru_out")
        else:
            j = l - n_a
            w_g = b_w_qg[j][:, n_heads * HEAD_DIM:].reshape(d, N_KV, 3 * gqa)
            w_g = jnp.pad(w_g, ((0, 0), (0, 0), (0, V7X_LANES - 3 * gqa))).reshape(d, N_KV * V7X_LANES)
            q_p = _matmul(up, b_w_qg, d, layer=j, tm=tm_mm, tn=tn, out_dtype=BF16, name="nsa_q")
            g_p = _matmul(up, w_g, N_KV * V7X_LANES, tm=tm_mm, tn=N_KV * V7X_LANES, out_dtype=F32,
                          epilogue=jax.nn.sigmoid, name="nsa_gate")
            o_p = _nsa_prompt(q_p, g_p, ctx_p[0], ctx_p[1], kv_p[2], kv_p[3], kv_p[4], kv_p[5], slopes, nb,
                              min(TQ, s_len))
            q_s = _matmul(us, b_w_qg, d, layer=j, tm=bd, tn=tn, out_dtype=BF16, name="nsa_q")
            g_s = _matmul(us, w_g, N_KV * V7X_LANES, tm=bd, tn=N_KV * V7X_LANES, out_dtype=F32,
                          epilogue=jax.nn.sigmoid, name="nsa_gate")
            q_s = q_s.reshape(bd * N_KV, gqa, HEAD_DIM)
            g_s = g_s.reshape(bd * N_KV, V7X_LANES)[:, :3 * gqa].reshape(bd * N_KV, gqa, 3)
            o_c, idx, okb = _nsa_sample_cmp(q_s, slopes_col, ctx_s, past)
            nsb = min(N_SEL, past // L_SEL + 1)
            o_s = _nsa_sample_sel(page_table, idx[:, 0, :nsb].reshape(-1), okb[:, 0, :nsb].reshape(-1),
                                  cache_k_sel, cache_v_sel, q_s, slopes_col, g_s, o_c,
                                  kv_s[2], kv_s[3], cache_k_win, cache_v_win, kv_s[4], kv_s[5], past)
            y_p = _matmul(o_p, b_w_o, d, layer=j, tm=tm_mm, tn=tn, out_dtype=F32, name="nsa_out")
            y_s = _matmul(o_s.reshape(bd, d), b_w_o, d, layer=j, tm=bd, tn=tn, out_dtype=F32, name="nsa_out")

        xp, up = _ln_mod(xp, y_p, mod_p(l, 2), ln_g[l, 0], ln_b[l, 0], alpha, tm_rows,
                         shift=mod_p(l, 3), scale=mod_p(l, 4))
        xs, us = _ln_mod(xs, y_s, mod_s(l, 2), ln_g[l, 0], ln_b[l, 0], alpha, bd,
                         shift=mod_s(l, 3), scale=mod_s(l, 4))
        last = l == depth - 1
        res_p, res_s = ffn(xp, xs, up, us, l, None if last else (l + 1, l == n_a - 1))
        if last:
            xp, xs = res_p[0], res_s[0]
        elif l == n_a - 1:
            xp, up, xbp = res_p
            xs, us, xbs = res_s
            kv_p = [_matmul(xbp, w_kv, ckv, tm=tm_mm, tn=ckv, out_dtype=F32, col_off=jj * ckv, name="kv_proj")
                    for jj in range(6)]
            kv_s = [_matmul(xbs, w_kv, ckv, tm=bd, tn=ckv, out_dtype=F32, col_off=jj * ckv, name="kv_proj")
                    for jj in range(6)]
            ctx_p = _cmp_blocks_prompt(kv_p[0], kv_p[1], w_cmp_k, w_cmp_v, nb)
            ctx_s = _page_chunks(page_table, cache_k_cmp, cache_v_cmp, w_cmp_k, w_cmp_v)
        else:
            xp, up = res_p
            xs, us = res_s

    wwin = min(WINDOW, s_len)
    shp_p = (nb, s_len, N_KV, HEAD_DIM)
    shp_s = (bd, sd, N_KV, HEAD_DIM)
    kvp = [a.reshape(shp_p) for a in kv_p]
    kvs = [a.reshape(shp_s) for a in kv_s]
    wbuf = cache_k_win.shape[1]
    k_win_s = jnp.concatenate([cache_k_win, kvs[4]], axis=1)[:, -wbuf:]
    v_win_s = jnp.concatenate([cache_v_win, kvs[5]], axis=1)[:, -wbuf:]
    return (xp.reshape(nb, s_len, d), xs.reshape(bd, sd, d),
            kvp[0], kvs[0], kvp[1], kvs[1], kvp[2], kvs[2], kvp[3], kvs[3],
            kvp[4][:, -wwin:], k_win_s, kvp[5][:, -wwin:], v_win_s,
            jnp.stack(h_p), jnp.stack(h_s), jnp.stack(cb_p), jnp.stack(cb_s))
```

```python
import functools

import numpy as np
import jax
import jax.numpy as jnp
from jax import lax
from jax.experimental import pallas as pl
from jax.experimental.pallas import tpu as pltpu

F32 = jnp.float32
BF16 = jnp.bfloat16

HEAD_DIM = 128
N_KV = 4
L_CMP = 32
D_CMP = 16
L_SEL = 64
N_SEL = 16
WINDOW = 512
PAGE_SIZE = 128
CONV_W = 4
LRU_BLOCKS = 16
LRU_C = 8.0
NEG = -1e30
FORCE_BONUS = 1e3
ATTN_SCALE = HEAD_DIM ** -0.5
LN_EPS = 1e-5
EXP2_SCALE = ATTN_SCALE * 1.4426950408889634
ATTN_ROW_CHUNK = 32
SEL_SHIFT = L_SEL.bit_length() - 1
assert 1 << SEL_SHIFT == L_SEL

V7X_VMEM_BYTES = 64 * 1024 * 1024
V7X_LANES = 128
V7X_SUBLANES = 8
VMEM_CAP = V7X_VMEM_BYTES - 8 * 1024 * 1024


def _params(sem, vmem_bytes):
    return pltpu.CompilerParams(dimension_semantics=sem,
                                vmem_limit_bytes=int(min(max(vmem_bytes, 16 << 20), VMEM_CAP)))


def _nt_dot(a, b, **kw):
    return lax.dot_general(a, b, (((1,), (1,)), ((), ())), preferred_element_type=F32, **kw)


def _ada_kernel(c_ref, w_ref, b_ref, o_ref):
    c = c_ref[...]
    s = (c * jax.nn.sigmoid(c)).astype(BF16)
    o_ref[...] = jnp.dot(s, w_ref[...].astype(BF16), preferred_element_type=F32) + b_ref[...]


def _ada(c_all, w_ada, b_ada, tn=1024):
    depth, d, n6 = w_ada.shape
    r = c_all.shape[0]
    return pl.pallas_call(
        _ada_kernel,
        grid=(depth, n6 // tn),
        in_specs=[pl.BlockSpec((r, d), lambda l, n: (0, 0)),
                  pl.BlockSpec((None, d, tn), lambda l, n: (l, 0, n)),
                  pl.BlockSpec((None, 1, tn), lambda l, n: (l, 0, n))],
        out_specs=pl.BlockSpec((None, r, tn), lambda l, n: (l, 0, n)),
        out_shape=jax.ShapeDtypeStruct((depth, r, n6), F32),
        compiler_params=_params(("arbitrary", "arbitrary"), 2 * d * tn * 4 + d * tn * 2 + (8 << 20)),
        name="ada",
    )(c_all, w_ada, b_ada.reshape(depth, 1, n6))


def _modulate_kernel(x_ref, sh_ref, sc_ref, u_ref):
    u_ref[...] = (x_ref[...] * (1.0 + sc_ref[...]) + sh_ref[...]).astype(u_ref.dtype)


def _row_specs(t, d, tm, nb):
    tiles_per_batch = (t // nb) // tm
    xspec = pl.BlockSpec((tm, d), lambda i: (i, 0))

    def vspec(r):
        return pl.BlockSpec((None, r, d), lambda i: (i // tiles_per_batch, 0, 0))
    return xspec, vspec


def _modulate(x, shift, scale, tm):
    t, d = x.shape
    nb, r, _ = shift.shape
    xspec, vspec = _row_specs(t, d, tm, nb)
    return pl.pallas_call(
        _modulate_kernel,
        grid=(t // tm,),
        in_specs=[xspec, vspec(r), vspec(r)],
        out_specs=xspec,
        out_shape=jax.ShapeDtypeStruct((t, d), BF16),
        compiler_params=_params(("arbitrary",), 6 * tm * d * 4),
        name="modulate",
    )(x, shift, scale)


def _ln_mod_kernel(*refs, alpha, want_u, want_xb):
    x_ref, y_ref, gate_ref, g_ref, b_ref = refs[:5]
    pos = 5
    if want_u:
        sh_ref, sc_ref = refs[pos:pos + 2]
        pos += 2
    xo_ref = refs[pos]
    pos += 1
    v = alpha * x_ref[...] + (1.0 + gate_ref[...]) * y_ref[...]
    mu = jnp.mean(v, axis=-1, keepdims=True)
    dlt = v - mu
    var = jnp.mean(dlt * dlt, axis=-1, keepdims=True)
    xn = dlt * lax.rsqrt(var + LN_EPS) * g_ref[...] + b_ref[...]
    xo_ref[...] = xn
    if want_u:
        refs[pos][...] = (xn * (1.0 + sc_ref[...]) + sh_ref[...]).astype(BF16)
        pos += 1
    if want_xb:
        refs[pos][...] = xn.astype(BF16)


def _ln_mod(x, y, gate, ln_g, ln_b, alpha, tm, shift=None, scale=None, want_xb=False):
    t, d = x.shape
    nb, r, _ = gate.shape
    xspec, vspec = _row_specs(t, d, tm, nb)
    pspec = pl.BlockSpec((1, d), lambda i: (0, 0))
    want_u = shift is not None
    ins = [x, y, gate, ln_g.reshape(1, d), ln_b.reshape(1, d)]
    in_specs = [xspec, xspec, vspec(r), pspec, pspec]
    out_shape = [jax.ShapeDtypeStruct((t, d), F32)]
    out_specs = [xspec]
    if want_u:
        ins += [shift, scale]
        in_specs += [vspec(r), vspec(r)]
        out_shape.append(jax.ShapeDtypeStruct((t, d), BF16))
        out_specs.append(xspec)
    if want_xb:
        out_shape.append(jax.ShapeDtypeStruct((t, d), BF16))
        out_specs.append(xspec)
    return pl.pallas_call(
        functools.partial(_ln_mod_kernel, alpha=alpha, want_u=want_u, want_xb=want_xb),
        grid=(t // tm,),
        in_specs=in_specs, out_specs=out_specs, out_shape=out_shape,
        compiler_params=_params(("arbitrary",), 12 * tm * d * 4),
        name="ln_mod",
    )(*ins)


def _gelu_tanh(x):
    return jax.nn.gelu(x, approximate=True)


def _mm_kernel(*refs, n_w, cast_w, epilogue):
    x_ref, xs_ref = refs[:2]
    w_refs = refs[2:2 + n_w]
    o_ref, os_ref = refs[2 + n_w:4 + n_w]
    scr = refs[4 + n_w:]
    first_row_tile = pl.program_id(1) == 0
    if cast_w:
        @pl.when(first_row_tile)
        def _():
            for w_ref, s_ref in zip(w_refs, scr):
                s_ref[...] = w_ref[...].astype(BF16)
        w_refs = scr

    def product(rows_ref, out_ref):
        rows = rows_ref[...]
        accs = [jnp.dot(rows, w_ref[...], preferred_element_type=F32) for w_ref in w_refs]
        out_ref[...] = epilogue(*accs).astype(out_ref.dtype)

    product(x_ref, o_ref)
    pl.when(first_row_tile)(lambda: product(xs_ref, os_ref))


def _matmul(x, xs, w, n_out, *, tm, tn, out_dtype, layer=None, epilogue=None, col_off=0, col_off2=None,
            name="mm"):
    m, k = x.shape
    ms = xs.shape[0]
    assert m % tm == 0 and n_out % tn == 0 and col_off % tn == 0
    assert (w.ndim == 3) == (layer is not None) and w.shape[-2] == k and xs.shape[1] == k
    offs = [col_off // tn] + ([] if col_off2 is None else [col_off2 // tn])
    n_w = len(offs)
    if epilogue is None:
        epilogue = lambda a: a
    cast_w = w.dtype != BF16
    scratch = [pltpu.VMEM((k, tn), BF16) for _ in range(n_w)] if cast_w else []
    w_bytes = k * tn * w.dtype.itemsize
    vmem = (2 * tm * k * 2 + n_w * (2 * w_bytes + len(scratch) * k * tn * 2)
            + 2 * tm * tn * 4 * (n_w + 1) + (4 << 20))
    if layer is None:
        w_specs = [pl.BlockSpec((k, tn), functools.partial(lambda n, i, o: (0, n + o), o=o)) for o in offs]
    else:
        w_specs = [pl.BlockSpec((None, k, tn), functools.partial(lambda n, i, o: (layer, 0, n + o), o=o))
                   for o in offs]
    return pl.pallas_call(
        functools.partial(_mm_kernel, n_w=n_w, cast_w=cast_w, epilogue=epilogue),
        grid=(n_out // tn, m // tm),
        in_specs=[pl.BlockSpec((tm, k), lambda n, i: (i, 0)), pl.BlockSpec((ms, k), lambda n, i: (0, 0))]
        + w_specs,
        out_specs=[pl.BlockSpec((tm, tn), lambda n, i: (i, n)), pl.BlockSpec((ms, tn), lambda n, i: (0, n))],
        out_shape=[jax.ShapeDtypeStruct((m, n_out), out_dtype), jax.ShapeDtypeStruct((ms, n_out), out_dtype)],
        scratch_shapes=scratch,
        compiler_params=_params(("arbitrary", "arbitrary"), vmem),
        name=name,
    )(x, xs, *([w] * n_w))


def _swiglu_epilogue(g, v):
    return g * jax.nn.sigmoid(g) * v


def _softplus_neg(lam):
    return jnp.maximum(-lam, 0.0) + jnp.log1p(jnp.exp(-jnp.abs(lam)))


def _lru_gates(xc, wr_ref, br_ref, wi_ref, bi_ref, sp, a_ref, b_ref, nblk, bw):
    def put(ref, k, val):
        ref[:, k * bw:(k + 1) * bw] = val

    nsp = (-LRU_C) * sp
    xcb = xc.astype(BF16)
    for k in range(nblk):
        cs = slice(k * bw, (k + 1) * bw)
        xb = xcb[:, cs]
        r = jax.nn.sigmoid(jnp.dot(xb, wr_ref[k], preferred_element_type=F32) + br_ref[:, cs])
        ig = jax.nn.sigmoid(jnp.dot(xb, wi_ref[k], preferred_element_type=F32) + bi_ref[:, cs])
        log_a = r * nsp[:, cs]
        put(a_ref, k, jnp.exp(log_a))
        th = jnp.tanh(log_a)
        put(b_ref, k, jnp.sqrt(-2.0 * th / (1.0 - th)) * (ig * xc[:, cs]))


SCAN_TILES = 8


def _rglru_kernel(rx_ref, ggx_ref, cb0_ref, h0_ref, cw_ref, cbias_ref, wr_ref, br_ref, wi_ref, bi_ref,
                  lam_ref, y_ref, hlast_ref, ctail_ref, xpad_ref, a_ref, b_ref, hcar_ref, *, tm, nblk, bw):
    s = pl.program_id(1)
    width = nblk * bw
    pad = V7X_SUBLANES
    tail = CONV_W - 1

    @pl.when(s == 0)
    def _():
        xpad_ref[pad - tail:pad, :] = cb0_ref[...]
        hcar_ref[...] = h0_ref[...]

    rx = rx_ref[...]
    xpad_ref[pad:pad + tm, :] = rx
    cw = cw_ref[...]
    xc = cbias_ref[...] + rx * cw[tail:tail + 1]
    for j in range(tail):
        xc = xc + xpad_ref[pad - tail + j:pad - tail + j + tm, :] * cw[j:j + 1]
    xpad_ref[pad - tail:pad, :] = rx[tm - tail:tm, :]

    _lru_gates(xc, wr_ref, br_ref, wi_ref, bi_ref, _softplus_neg(lam_ref[...]), a_ref, b_ref, nblk, bw)

    scan_lanes = min(SCAN_TILES * V7X_LANES, width)
    row = lax.broadcasted_iota(jnp.int32, (V7X_SUBLANES, scan_lanes), 0)
    for c in range(width // scan_lanes):
        cs = slice(c * scan_lanes, (c + 1) * scan_lanes)

        def group(gi, h, cs=cs):
            r0 = pl.multiple_of(gi * V7X_SUBLANES, V7X_SUBLANES)
            a = a_ref[pl.ds(r0, V7X_SUBLANES), cs]
            b = b_ref[pl.ds(r0, V7X_SUBLANES), cs]
            for sft in (1, 2, 4):
                keep = row >= sft
                a_s = jnp.where(keep, pltpu.roll(a, sft, 0), 1.0)
                b_s = jnp.where(keep, pltpu.roll(b, sft, 0), 0.0)
                b = a * b_s + b
                a = a * a_s
            hs = a * h + b
            b_ref[pl.ds(r0, V7X_SUBLANES), cs] = hs
            return jnp.broadcast_to(hs[V7X_SUBLANES - 1:V7X_SUBLANES, :], hs.shape)

        h_in = jnp.broadcast_to(hcar_ref[:, cs], (V7X_SUBLANES, scan_lanes))
        h_out = lax.fori_loop(0, tm // V7X_SUBLANES, group, h_in)
        hcar_ref[:, cs] = h_out[0:1, :]

    y_ref[...] = (ggx_ref[...].astype(F32) * b_ref[...]).astype(BF16)

    @pl.when(s == pl.num_programs(1) - 1)
    def _():
        hlast_ref[...] = hcar_ref[...]
        ctail_ref[...] = rx[tm - tail:tm, :]


def _rglru_prompt(rx, ggx, conv_buf, h0, cw, cbias, wr, br, wi, bi, lam, nb, tm):
    t, width = rx.shape
    s_len = t // nb
    nblk, bw, _ = wr.shape
    tail = CONV_W - 1
    ns = s_len // tm
    row = lambda a: a.reshape(1, width)
    xspec = pl.BlockSpec((tm, width), lambda b, s: (b * ns + s, 0))
    pspec = pl.BlockSpec((1, width), lambda b, s: (0, 0))
    wspec = pl.BlockSpec((nblk, bw, bw), lambda b, s: (0, 0, 0))
    vmem = (2 * tm * width * (4 + 2 + 2) + 3 * (tm + 8) * width * 4 + 4 * nblk * bw * bw * 2 + (6 << 20))
    return pl.pallas_call(
        functools.partial(_rglru_kernel, tm=tm, nblk=nblk, bw=bw),
        grid=(nb, ns),
        in_specs=[xspec, xspec,
                  pl.BlockSpec((None, tail, width), lambda b, s: (b, 0, 0)),
                  pl.BlockSpec((None, 1, width), lambda b, s: (b, 0, 0)),
                  pl.BlockSpec((CONV_W, width), lambda b, s: (0, 0)), pspec,
                  wspec, pspec, wspec, pspec, pspec],
        out_specs=[xspec,
                   pl.BlockSpec((None, 1, width), lambda b, s: (b, 0, 0)),
                   pl.BlockSpec((None, tail, width), lambda b, s: (b, 0, 0))],
        out_shape=[jax.ShapeDtypeStruct((t, width), BF16),
                   jax.ShapeDtypeStruct((nb, 1, width), F32),
                   jax.ShapeDtypeStruct((nb, tail, width), F32)],
        scratch_shapes=[pltpu.VMEM((tm + V7X_SUBLANES, width), F32), pltpu.VMEM((tm, width), F32),
                        pltpu.VMEM((tm, width), F32), pltpu.VMEM((1, width), F32)],
        compiler_params=_params(("arbitrary", "arbitrary"), vmem),
        name="rglru",
    )(rx, ggx, conv_buf, h0.reshape(nb, 1, width), cw, row(cbias), wr, row(br), wi, row(bi), row(lam))


def _rglru_step_kernel(rx_ref, ggx_ref, cb_ref, h0_ref, cw_ref, cbias_ref, wr_ref, br_ref, wi_ref, bi_ref,
                       lam_ref, y_ref, h_ref, cnew_ref, a_ref, b_ref, *, nblk, bw):
    tail = CONV_W - 1
    rx = rx_ref[...]
    cw = cw_ref[...]
    xc = cbias_ref[...] + rx * cw[tail:tail + 1]
    for j in range(tail):
        xc = xc + cb_ref[j] * cw[j:j + 1]
    _lru_gates(xc, wr_ref, br_ref, wi_ref, bi_ref, _softplus_neg(lam_ref[...]), a_ref, b_ref, nblk, bw)
    h = a_ref[...] * h0_ref[...] + b_ref[...]
    h_ref[...] = h
    y_ref[...] = (ggx_ref[...].astype(F32) * h).astype(BF16)
    for j in range(tail - 1):
        cnew_ref[j] = cb_ref[j + 1]
    cnew_ref[tail - 1] = rx


def _rglru_step(rx, ggx, conv_t, h0, cw, cbias, wr, br, wi, bi, lam):
    bd, width = rx.shape
    nblk, bw, _ = wr.shape
    row = lambda a: a.reshape(1, width)
    return pl.pallas_call(
        functools.partial(_rglru_step_kernel, nblk=nblk, bw=bw),
        out_shape=[jax.ShapeDtypeStruct((bd, width), BF16), jax.ShapeDtypeStruct((bd, width), F32),
                   jax.ShapeDtypeStruct(conv_t.shape, F32)],
        scratch_shapes=[pltpu.VMEM((bd, width), F32), pltpu.VMEM((bd, width), F32)],
        compiler_params=pltpu.CompilerParams(vmem_limit_bytes=32 << 20),
        name="rglru_step",
    )(rx, ggx, conv_t, h0, cw, row(cbias), wr, row(br), wi, row(bi), row(lam))


def _chunk_sums(x, w):
    n = x.shape[0] // D_CMP
    x3 = x.reshape(n, D_CMP, x.shape[1])
    first = jnp.sum(x3 * w[None, :D_CMP, :], axis=1)
    second = jnp.sum(x3 * w[None, D_CMP:, :], axis=1)
    return first, second


def _blocks_from_sums(first, second):
    n = first.shape[0]
    nxt = pltpu.roll(second, n - 1, 0)
    rows = lax.broadcasted_iota(jnp.int32, first.shape, 0)
    return jnp.where(rows < n - 1, first + nxt, 0.0)


COEF_PIECES = 3
POS_LOW_BITS = 7


def _pos_lanes(pos):
    lane = lax.broadcasted_iota(jnp.int32, pos.shape, 1)
    hi = ((pos >> POS_LOW_BITS) << POS_LOW_BITS).astype(F32)
    lo = (pos & ((1 << POS_LOW_BITS) - 1)).astype(F32)
    return jnp.where(lane < COEF_PIECES, hi, jnp.where(lane < 2 * COEF_PIECES, lo, 0.0)).astype(BF16)


def _coef_lanes(coef, shape):
    v = jnp.full(shape, coef, F32)
    c1 = v.astype(BF16).astype(F32)
    c2 = (v - c1).astype(BF16).astype(F32)
    c3 = v - c1 - c2
    k = lax.broadcasted_iota(jnp.int32, shape, 1)
    k = jnp.where(k >= COEF_PIECES, k - COEF_PIECES, k)
    live = lax.broadcasted_iota(jnp.int32, shape, 1) < 2 * COEF_PIECES
    out = jnp.where(k == 0, c1, jnp.where(k == 1, c2, c3))
    return jnp.where(live, out, 0.0).astype(BF16)


def _cmp_blocks_kernel(k_ref, v_ref, wk_ref, wv_ref, kc_ref, vc_ref):
    hd = k_ref.shape[1]
    nch = kc_ref.shape[0]
    kc_ref[:, :hd] = _blocks_from_sums(*_chunk_sums(k_ref[...], wk_ref[...])).astype(BF16)
    ends = (lax.broadcasted_iota(jnp.int32, (nch, V7X_LANES), 0) + 2) * D_CMP - 1
    kc_ref[:, hd:] = _pos_lanes(ends)
    vc_ref[...] = _blocks_from_sums(*_chunk_sums(v_ref[...], wv_ref[...])).astype(BF16)


def _cmp_blocks_prompt(k_c, v_c, wk, wv, nb):
    t, c = k_c.shape
    s_len = t // nb
    nch = s_len // D_CMP
    kvspec = pl.BlockSpec((s_len, HEAD_DIM), lambda b, h: (b, h))
    wspec = pl.BlockSpec((L_CMP, HEAD_DIM), lambda b, h: (0, h))

    def ospec(w):
        return pl.BlockSpec((None, None, nch, w), lambda b, h: (b, h, 0, 0))
    return pl.pallas_call(
        _cmp_blocks_kernel, grid=(nb, N_KV),
        in_specs=[kvspec, kvspec, wspec, wspec],
        out_specs=[ospec(HEAD_DIM + V7X_LANES), ospec(HEAD_DIM)],
        out_shape=[jax.ShapeDtypeStruct((nb, N_KV, nch, HEAD_DIM + V7X_LANES), BF16),
                   jax.ShapeDtypeStruct((nb, N_KV, nch, HEAD_DIM), BF16)],
        compiler_params=_params(("arbitrary", "arbitrary"), 24 << 20),
        name="cmp_blocks",
    )(k_c, v_c, wk.reshape(L_CMP, c), wv.reshape(L_CMP, c))


def _block_scores_topk_t(imp_t, q0, nsel, tq):
    nsp = -(-nsel // V7X_SUBLANES) * V7X_SUBLANES
    jt = lax.broadcasted_iota(jnp.int32, (nsp, tq), 0)
    qpos = q0 + lax.broadcasted_iota(jnp.int32, (nsp, tq), 1)
    cur = qpos >> SEL_SHIFT
    valid = (jt <= cur) & (jt < nsel)
    forced = (jt == 0) | (jt == cur) | (jt == cur - 1)
    score = jnp.where(valid, imp_t[:nsp] + jnp.where(forced, FORCE_BONUS, 0.0), NEG)
    rank = jnp.zeros((nsp, tq), F32)
    for i in range(nsel):
        ri = score[i:i + 1, :]
        beats = (ri > score) | ((ri == score) & (jt > i))
        rank = rank + jnp.where(beats, 1.0, 0.0)
    return jnp.where((rank < float(min(N_SEL, nsel))) & valid, 1.0, 0.0)


def _nsa_prompt_kernel(slopes_ref, q_ref, gt_ref, kc_ref, vc_ref, ks_ref, vs_ref, kw_ref, vw_ref, e_ref,
                       o_ref, ksa, vsb, kwa, vwb, qa_ref, s_ref, p_ref, bsel_ref, bwin_ref, bcmp_ref,
                       m_ref, acc_ref, oc_ref, os_ref, ow_ref, psum_ref, *, tq, gqa, s_len):
    h = pl.program_id(1)
    i = pl.program_id(2)
    q0 = i * tq
    hd = HEAD_DIM
    ncp = kc_ref.shape[0]
    n_cmp = ncp - 1
    nsel = s_len // L_SEL
    nkt = s_len // tq
    rows = gqa * tq
    rc = ATTN_ROW_CHUNK
    per_g = tq // rc
    lane_tiles = tq // V7X_LANES
    r_io = lax.broadcasted_iota(jnp.int32, (tq, tq), 0)
    c_io = lax.broadcasted_iota(jnp.int32, (tq, tq), 1)

    @pl.when(i == 0)
    def _():
        pcols = _pos_lanes(lax.broadcasted_iota(jnp.int32, (s_len, V7X_LANES), 0))
        ksa[:, :hd] = ks_ref[...].astype(BF16)
        ksa[:, hd:] = pcols
        kwa[:, :hd] = kw_ref[...].astype(BF16)
        kwa[:, hd:] = pcols
        ones = jnp.ones((s_len, V7X_LANES), BF16)
        vsb[:, :hd] = vs_ref[...].astype(BF16)
        vsb[:, hd:] = ones
        vwb[:, :hd] = vw_ref[...].astype(BF16)
        vwb[:, hd:] = ones
        wshape = bwin_ref.shape[1:]
        rw = lax.broadcasted_iota(jnp.int32, wshape, 0)
        cw = lax.broadcasted_iota(jnp.int32, wshape, 1)
        for v in range(bwin_ref.shape[0]):
            dist = min(v * tq, WINDOW) + rw - cw
            bwin_ref[v] = jnp.where((dist >= 0) & (dist <= WINDOW), 0.0, NEG)

    for g in range(gqa):
        rs = slice(g * tq, (g + 1) * tq)
        qa_ref[rs, :hd] = q_ref[:, g * hd:(g + 1) * hd]
        qa_ref[rs, hd:] = _coef_lanes(slopes_ref[h * gqa + g] * (1.0 / ATTN_SCALE), (tq, V7X_LANES))

    def head_rows(c, g):
        if isinstance(c, int):
            return slice(g * tq + c * rc, g * tq + (c + 1) * rc)
        return pl.ds(pl.multiple_of(g * tq + c * rc, rc), rc)

    def tile_rows(c):
        if isinstance(c, int):
            return slice(c * rc, (c + 1) * rc)
        return pl.ds(pl.multiple_of(c * rc, rc), rc)

    lanes = [slice(j * V7X_LANES, (j + 1) * V7X_LANES) for j in range(lane_tiles)]

    s_ref[:, :ncp] = _nt_dot(qa_ref[...], kc_ref[...])
    n_io = lax.broadcasted_iota(jnp.int32, (tq, ncp), 1)
    qpos_c = q0 + lax.broadcasted_iota(jnp.int32, (tq, ncp), 0)
    bcmp_ref[...] = jnp.where((qpos_c >= (n_io + 2) * D_CMP - 1) & (n_io < n_cmp), 0.0, NEG)

    def cmp_chunk(c, carry):
        br = tile_rows(c)
        bias = bcmp_ref[br, :]
        live = bias == 0.0
        tot = jnp.zeros((rc, ncp), F32)
        for g in range(gqa):
            rr = head_rows(c, g)
            s = s_ref[rr, :ncp] + bias
            p = jnp.where(live, jnp.exp2((s - jnp.max(s, axis=1, keepdims=True)) * EXP2_SCALE), 0.0)
            p = p / jnp.maximum(jnp.sum(p, axis=1, keepdims=True), 1e-30)
            tot = tot + p
            p_ref[rr, :ncp] = p.astype(BF16)
        psum_ref[br, :] = tot
        return carry

    for c in range(per_g):
        cmp_chunk(c, 0)
    oc_ref[...] = jnp.dot(p_ref[:, :ncp], vc_ref[...], preferred_element_type=F32)
    psum = psum_ref[...]

    j_io = lax.broadcasted_iota(jnp.int32, (V7X_LANES, ncp), 0)
    n_io2 = lax.broadcasted_iota(jnp.int32, (V7X_LANES, ncp), 1)
    cover_t = jnp.where((n_io2 * D_CMP < j_io * L_SEL + L_SEL) & (n_io2 * D_CMP + L_CMP > j_io * L_SEL)
                        & (n_io2 < n_cmp) & (j_io < nsel), 1.0, 0.0)
    imp_t = _nt_dot(cover_t, psum, precision=lax.Precision.HIGHEST)
    sel_t = _block_scores_topk_t(imp_t, q0, nsel, tq)
    sel_t = jnp.concatenate([sel_t, jnp.zeros((V7X_LANES - sel_t.shape[0], tq), F32)], axis=0)
    sel = sel_t.T.astype(BF16)
    for kt in range(nkt):
        @pl.when(kt <= i)
        def _(kt=kt):
            picked = jnp.dot(sel, e_ref[:, kt * tq:(kt + 1) * tq], preferred_element_type=F32)
            causal_slack = jnp.where(kt < i, tq, 0)
            bsel_ref[kt] = jnp.where((picked > 0.5) & (c_io <= r_io + causal_slack), 0.0, NEG)

    def softmax_passes(width, bias_ref, first):
        lanes = [slice(j * V7X_LANES, (j + 1) * V7X_LANES) for j in range(width // V7X_LANES)]
        for c in range(per_g):
            br = tile_rows(c)
            bias = [bias_ref[br, ls] for ls in lanes]
            for g in range(gqa):
                rr = head_rows(c, g)
                top = functools.reduce(jnp.maximum, [s_ref[rr, ls] + b for ls, b in zip(lanes, bias)])
                m_new = jnp.max(top, axis=1, keepdims=True)
                if first:
                    m_ref[rr, :] = jnp.broadcast_to(m_new, (rc, V7X_LANES))
                else:
                    m_prev = m_ref[rr, :]
                    m_new = jnp.maximum(m_prev, m_new)
                    alpha = jnp.exp2((m_prev - m_new) * EXP2_SCALE)
                    m_ref[rr, :] = m_new
                    acc_ref[rr, :hd] = alpha * acc_ref[rr, :hd]
                    acc_ref[rr, hd:] = alpha * acc_ref[rr, hd:]
        for c in range(per_g):
            br = tile_rows(c)
            bias = [bias_ref[br, ls] for ls in lanes]
            for g in range(gqa):
                rr = head_rows(c, g)
                m_new = m_ref[rr, :]
                for ls, b in zip(lanes, bias):
                    p_ref[rr, ls] = jnp.exp2((s_ref[rr, ls] + b - m_new) * EXP2_SCALE).astype(BF16)

    def sel_tile(kt, first):
        k0 = kt * tq if first else pl.multiple_of(kt * tq, tq)
        s_ref[:, :tq] = _nt_dot(qa_ref[...], ksa[pl.ds(k0, tq), :])
        softmax_passes(tq, bsel_ref.at[kt], first)
        pv = jnp.dot(p_ref[:, :tq], vsb[pl.ds(k0, tq), :], preferred_element_type=F32)
        if first:
            acc_ref[...] = pv
        else:
            acc_ref[...] += pv

    sel_tile(0, True)

    def sel_rest(kt, carry):
        sel_tile(kt, False)
        return carry

    lax.fori_loop(1, i + 1, sel_rest, 0)
    os_ref[...] = acc_ref[:, :hd] / jnp.maximum(acc_ref[:, hd:], 1e-30)

    wk = bwin_ref.shape[2]
    k0w = pl.multiple_of(jnp.maximum(q0 - WINDOW, 0), tq)
    s_ref[...] = _nt_dot(qa_ref[...], kwa[pl.ds(k0w, wk), :])
    softmax_passes(wk, bwin_ref.at[jnp.minimum(i, bwin_ref.shape[0] - 1)], True)
    pv = jnp.dot(p_ref[...], vwb[pl.ds(k0w, wk), :], preferred_element_type=F32)
    ow_ref[...] = pv[:, :hd] / jnp.maximum(pv[:, hd:], 1e-30)

    gt = gt_ref[...]
    for g in range(gqa):
        rs = slice(g * tq, (g + 1) * tq)
        o = (gt[:, 3 * g:3 * g + 1] * oc_ref[rs, :] + gt[:, 3 * g + 1:3 * g + 2] * os_ref[rs, :]
             + gt[:, 3 * g + 2:3 * g + 3] * ow_ref[rs, :])
        o_ref[:, g * hd:(g + 1) * hd] = o.astype(BF16)


def _nsa_prompt(q, gates, kc, vc, k_s, v_s, k_w, v_w, slopes, nb, tq):
    t, dq = q.shape
    s_len = t // nb
    gqa = dq // (N_KV * HEAD_DIM)
    nq = s_len // tq
    ncp = kc.shape[2]
    assert WINDOW % tq == 0 and s_len % tq == 0 and s_len // L_SEL <= V7X_LANES
    expand = np.zeros((V7X_LANES, s_len), np.float32)
    expand[np.arange(s_len) // L_SEL, np.arange(s_len)] = 1.0
    assert ncp == V7X_LANES and tq % ATTN_ROW_CHUNK == 0 and (tq // ATTN_ROW_CHUNK) & (tq // ATTN_ROW_CHUNK - 1) == 0
    rows = gqa * tq
    hd = HEAD_DIM
    qspec = pl.BlockSpec((tq, gqa * hd), lambda b, h, i, sl: (b * nq + i, h))
    gspec = pl.BlockSpec((tq, V7X_LANES), lambda b, h, i, sl: (b * nq + i, h))

    def cspec(w):
        return pl.BlockSpec((None, None, ncp, w), lambda b, h, i, sl: (b, h, 0, 0))
    kvspec = pl.BlockSpec((s_len, hd), lambda b, h, i, sl: (b, h))
    espec = pl.BlockSpec((V7X_LANES, s_len), lambda b, h, i, sl: (0, 0))
    ka_scr = pltpu.VMEM((s_len, hd + V7X_LANES), BF16)
    v_scr = pltpu.VMEM((s_len, hd + V7X_LANES), BF16)
    row_scr = pltpu.VMEM((rows, hd), F32)
    return pl.pallas_call(
        functools.partial(_nsa_prompt_kernel, tq=tq, gqa=gqa, s_len=s_len),
        grid_spec=pltpu.PrefetchScalarGridSpec(
            num_scalar_prefetch=1, grid=(nb, N_KV, nq),
            in_specs=[qspec, gspec, cspec(hd + V7X_LANES), cspec(hd), kvspec, kvspec, kvspec, kvspec, espec],
            out_specs=qspec,
            scratch_shapes=[ka_scr, v_scr, ka_scr, v_scr,
                            pltpu.VMEM((rows, hd + V7X_LANES), BF16),
                            pltpu.VMEM((rows, WINDOW + tq), F32), pltpu.VMEM((rows, WINDOW + tq), BF16),
                            pltpu.VMEM((nq, tq, tq), F32),
                            pltpu.VMEM((WINDOW // tq + 1, tq, WINDOW + tq), F32),
                            pltpu.VMEM((tq, ncp), F32),
                            pltpu.VMEM((rows, V7X_LANES), F32),
                            pltpu.VMEM((rows, hd + V7X_LANES), F32),
                            row_scr, row_scr, row_scr,
                            pltpu.VMEM((tq, ncp), F32)]),
        out_shape=jax.ShapeDtypeStruct((t, dq), BF16),
        compiler_params=_params(("arbitrary", "arbitrary", "arbitrary"), 48 << 20),
        name="nsa_prompt",
    )(slopes, q, gates, kc, vc, k_s, v_s, k_w, v_w, jnp.asarray(expand, BF16))


def _page_chunks_kernel(pt_ref, *refs, pages):
    k_pages = refs[:pages]
    v_pages = refs[pages:2 * pages]
    wk_ref, wv_ref, fk_ref, sk_ref, fv_ref, sv_ref = refs[2 * pages:]
    per = PAGE_SIZE // D_CMP

    def sums(x, w):
        x4 = x.reshape((per, D_CMP) + x.shape[1:])
        return jnp.sum(x4 * w[None, :D_CMP], axis=1), jnp.sum(x4 * w[None, D_CMP:], axis=1)

    for p in range(pages):
        rs = slice(p * per, (p + 1) * per)
        fk_ref[rs], sk_ref[rs] = sums(k_pages[p][...], wk_ref[...])
        fv_ref[rs], sv_ref[rs] = sums(v_pages[p][...], wv_ref[...])


def _page_chunks(page_table, pool_k, pool_v, wk, wv, pages=16):
    bd, n_pages = page_table.shape
    n_pool, _, n_kv, hd = pool_k.shape
    per = PAGE_SIZE // D_CMP
    nch = n_pages * per
    assert n_pages % pages == 0

    def page_spec(p):
        return pl.BlockSpec((None, PAGE_SIZE, n_kv, hd), lambda b, t, pt: (pt[b, t * pages + p], 0, 0, 0))
    wspec = pl.BlockSpec((L_CMP, n_kv, hd), lambda b, t, pt: (0, 0, 0))
    ospec = pl.BlockSpec((None, pages * per, n_kv, hd), lambda b, t, pt: (b, t, 0, 0))
    osh = jax.ShapeDtypeStruct((bd, nch, n_kv, hd), F32)
    return pl.pallas_call(
        functools.partial(_page_chunks_kernel, pages=pages),
        grid_spec=pltpu.PrefetchScalarGridSpec(
            num_scalar_prefetch=1, grid=(bd, n_pages // pages),
            in_specs=[page_spec(p) for p in range(pages)] * 2 + [wspec, wspec],
            out_specs=[ospec] * 4),
        out_shape=[osh] * 4,
        compiler_params=_params(("arbitrary", "arbitrary"), 52 << 20),
        name="page_chunks",
    )(page_table, *([pool_k] * pages), *([pool_v] * pages), wk, wv)


def _nsa_sample_cmp_kernel(q_ref, sl_ref, fk_ref, sk_ref, fv_ref, sv_ref, cover_ref, oc_ref, idx_ref, ok_ref,
                           *, past, n_sel_blocks):
    n_kv, gqa, _ = q_ref.shape
    nch = fk_ref.shape[0]
    n_cmp = nch - 1
    nj = cover_ref.shape[1]
    q_pos = past
    n_io = lax.broadcasted_iota(jnp.int32, (gqa, nch), 1)
    dist = (q_pos - ((n_io + 2) * D_CMP - 1)).astype(F32)
    ok = (dist >= 0.0) & (n_io < n_cmp)
    j_io = lax.broadcasted_iota(jnp.int32, (gqa, nj), 1)
    cur = q_pos // L_SEL
    valid = (j_io <= cur) & (j_io < n_sel_blocks)
    forced = (j_io == 0) | (j_io == cur) | (j_io == cur - 1)
    lane = lax.broadcasted_iota(jnp.int32, (gqa, V7X_LANES), 1)
    j_f = j_io.astype(F32)
    for h in range(n_kv):
        kc = _blocks_from_sums(fk_ref[:, h, :], sk_ref[:, h, :]).astype(BF16)
        vc = _blocks_from_sums(fv_ref[:, h, :], sv_ref[:, h, :]).astype(BF16)
        s = jnp.where(ok, _nt_dot(q_ref[h], kc) * ATTN_SCALE - sl_ref[h] * dist, NEG)
        p = jnp.where(ok, jnp.exp(s - jnp.max(s, axis=1, keepdims=True)), 0.0)
        p = p / jnp.maximum(jnp.sum(p, axis=1, keepdims=True), 1e-30)
        oc_ref[h] = jnp.dot(p.astype(BF16), vc, preferred_element_type=F32)

        psum = jnp.broadcast_to(jnp.sum(p, axis=0, keepdims=True), (gqa, nch))
        imp = jnp.dot(psum, cover_ref[...], precision=lax.Precision.HIGHEST, preferred_element_type=F32)
        score = jnp.where(valid, imp + jnp.where(forced, FORCE_BONUS, 0.0), NEG)
        score = jnp.where(j_io < n_sel_blocks, score, -jnp.inf)
        idx = jnp.zeros((gqa, V7X_LANES), F32)
        okv = jnp.zeros((gqa, V7X_LANES), jnp.int32)
        for t in range(min(N_SEL, n_sel_blocks)):
            mx = jnp.max(score, axis=1, keepdims=True)
            am = jnp.min(jnp.where(score == mx, j_f, float(nj)), axis=1, keepdims=True)
            idx = jnp.where(lane == t, am, idx)
            okv = jnp.where(lane == t, jnp.where(mx > 0.5 * NEG, 1, 0), okv)
            score = jnp.where(j_f == am, -jnp.inf, score)
        idx_ref[h] = idx.astype(jnp.int32)
        ok_ref[h] = okv


def _nsa_sample_cmp(q, slopes_col, sums, past):
    bh, gqa, hd = q.shape
    fk = sums[0]
    bd, nch, n_kv, _ = fk.shape
    n_sel_blocks = past // L_SEL + 1
    nj = -(-n_sel_blocks // V7X_LANES) * V7X_LANES
    n_io = np.arange(nch)[:, None]
    j_io = np.arange(nj)[None, :]
    cover = ((n_io * D_CMP < j_io * L_SEL + L_SEL) & (n_io * D_CMP + L_CMP > j_io * L_SEL)
             & (n_io < nch - 1) & (j_io < n_sel_blocks)).astype(np.float32)
    hspec = pl.BlockSpec((n_kv, gqa, hd), lambda b: (b, 0, 0))
    sspec = pl.BlockSpec((None, nch, n_kv, hd), lambda b: (b, 0, 0, 0))
    ispec = pl.BlockSpec((n_kv, gqa, V7X_LANES), lambda b: (b, 0, 0))
    return pl.pallas_call(
        functools.partial(_nsa_sample_cmp_kernel, past=past, n_sel_blocks=n_sel_blocks),
        grid=(bd,),
        in_specs=[hspec, pl.BlockSpec((n_kv, gqa, 1), lambda b: (0, 0, 0)), sspec, sspec, sspec, sspec,
                  pl.BlockSpec((nch, nj), lambda b: (0, 0))],
        out_specs=[hspec, ispec, ispec],
        out_shape=[jax.ShapeDtypeStruct((bh, gqa, hd), F32),
                   jax.ShapeDtypeStruct((bh, gqa, V7X_LANES), jnp.int32),
                   jax.ShapeDtypeStruct((bh, gqa, V7X_LANES), jnp.int32)],
        compiler_params=_params(("arbitrary",), 48 << 20),
        name="nsa_sample_cmp",
    )(q, slopes_col, *sums, jnp.asarray(cover))


def _nsa_sample_sel_kernel(pt_ref, idx_ref, okb_ref, *refs, past, nsb):
    k_blks = refs[:nsb]
    v_blks = refs[nsb:2 * nsb]
    (q_ref, sl_ref, gt_ref, oc_ref, ksn_ref, vsn_ref, kwc_ref, vwc_ref, kwn_ref, vwn_ref, o_ref) = refs[2 * nsb:]
    b = pl.program_id(0)
    h = pl.program_id(1)
    gqa, hd = q_ref.shape
    base = (b * N_KV + h) * nsb
    nb_past = past // L_SEL
    q_pos = past
    q = q_ref[...]
    slope = sl_ref[...]
    head = lambda ref: ref[:, pl.ds(h, 1), :].reshape(ref.shape[0], hd)

    row_io = lax.broadcasted_iota(jnp.int32, (L_SEL, hd), 0)
    new_k = jnp.where(row_io == 0, jnp.broadcast_to(ksn_ref[...], (L_SEL, hd)), 0.0)
    new_v = jnp.where(row_io == 0, jnp.broadcast_to(vsn_ref[...], (L_SEL, hd)), 0.0)
    lane = lax.broadcasted_iota(jnp.int32, (gqa, nsb * L_SEL), 1)
    kpos = lane & (L_SEL - 1)
    okl = jnp.zeros((gqa, nsb * L_SEL), jnp.int32)
    kg, vg = [], []
    for t in range(nsb):
        bid = idx_ref[base + t]
        is_new = bid >= nb_past
        kg.append(jnp.where(is_new, new_k, head(k_blks[t])).astype(BF16))
        vg.append(jnp.where(is_new, new_v, head(v_blks[t])).astype(BF16))
        in_t = (lane >> SEL_SHIFT) == t
        kpos = kpos + jnp.where(in_t, bid * L_SEL, 0)
        okl = okl + jnp.where(in_t, okb_ref[base + t], 0)
    kg = jnp.concatenate(kg, axis=0)
    vg = jnp.concatenate(vg, axis=0)
    dist = (q_pos - kpos).astype(F32)
    ok = (dist >= 0.0) & (okl > 0)
    s = jnp.where(ok, _nt_dot(q, kg) * ATTN_SCALE - slope * dist, NEG)
    p = jnp.where(ok, jnp.exp(s - jnp.max(s, axis=1, keepdims=True)), 0.0)
    p = p / jnp.maximum(jnp.sum(p, axis=1, keepdims=True), 1e-30)
    o_s = jnp.dot(p.astype(BF16), vg, preferred_element_type=F32)

    wbuf = kwc_ref.shape[0]
    i_io = lax.broadcasted_iota(jnp.int32, (gqa, wbuf), 1)
    kpos_w = past - wbuf + i_io
    dist_w = (q_pos - kpos_w).astype(F32)
    ok_w = (dist_w >= 0.0) & (dist_w <= float(WINDOW)) & (kpos_w >= 0)
    s_w = jnp.where(ok_w, _nt_dot(q, head(kwc_ref).astype(BF16)) * ATTN_SCALE - slope * dist_w, NEG)
    kn = jnp.broadcast_to(kwn_ref[...], (V7X_SUBLANES, hd)).astype(BF16)
    s_n = (_nt_dot(q, kn) * ATTN_SCALE)[:, 0:1]
    m = jnp.maximum(jnp.max(s_w, axis=1, keepdims=True), s_n)
    p_w = jnp.where(ok_w, jnp.exp(s_w - m), 0.0)
    p_n = jnp.exp(s_n - m)
    den = jnp.maximum(jnp.sum(p_w, axis=1, keepdims=True) + p_n, 1e-30)
    p_w = p_w / den
    p_n = p_n / den
    o_w = (jnp.dot(p_w.astype(BF16), head(vwc_ref).astype(BF16), preferred_element_type=F32)
           + p_n.astype(BF16).astype(F32) * vwn_ref[...].astype(BF16).astype(F32))

    gt = gt_ref[...]
    o_ref[...] = (gt[:, 0:1] * oc_ref[...] + gt[:, 1:2] * o_s + gt[:, 2:3] * o_w).astype(BF16)


def _nsa_sample_sel(page_table, idx_flat, ok_flat, pool_k, pool_v, q, slopes_col, gates, o_c,
                    ks_new, vs_new, cache_kw, cache_vw, kw_new, vw_new, past):
    bh, gqa, hd = q.shape
    bd = bh // N_KV
    nsb = idx_flat.shape[0] // bh
    c = N_KV * hd
    halves = PAGE_SIZE // L_SEL
    nb_past = past // L_SEL
    wbuf = cache_kw.shape[1]

    def blk_spec(t):
        def imap(b, h, pt, idx, okb):
            bid = jnp.minimum(idx[(b * N_KV + h) * nsb + t], nb_past - 1)
            return (pt[b, bid // halves], bid % halves, 0, 0)
        return pl.BlockSpec((None, L_SEL, N_KV, hd), imap)
    hspec = pl.BlockSpec((None, gqa, hd), lambda b, h, *_: (b * N_KV + h, 0, 0))
    nspec = pl.BlockSpec((None, 1, hd), lambda b, h, *_: (b, 0, h))
    wspec = pl.BlockSpec((None, wbuf, N_KV, hd), lambda b, h, *_: (b, 0, 0, 0))
    return pl.pallas_call(
        functools.partial(_nsa_sample_sel_kernel, past=past, nsb=nsb),
        grid_spec=pltpu.PrefetchScalarGridSpec(
            num_scalar_prefetch=3, grid=(bd, N_KV),
            in_specs=[blk_spec(t) for t in range(nsb)] * 2
            + [hspec, pl.BlockSpec((None, gqa, 1), lambda b, h, *_: (h, 0, 0)),
               pl.BlockSpec((None, gqa, 3), lambda b, h, *_: (b * N_KV + h, 0, 0)), hspec,
               nspec, nspec, wspec, wspec, nspec, nspec],
            out_specs=hspec),
        out_shape=jax.ShapeDtypeStruct((bh, gqa, hd), BF16),
        compiler_params=_params(("arbitrary", "arbitrary"), 40 << 20),
        name="nsa_sample_sel",
    )(page_table, idx_flat, ok_flat, *([pool_k] * nsb), *([pool_v] * nsb), q, slopes_col, gates, o_c,
      ks_new.reshape(bd, 1, c), vs_new.reshape(bd, 1, c), cache_kw, cache_vw,
      kw_new.reshape(bd, 1, c), vw_new.reshape(bd, 1, c))


TM_MM = 1024
TM_ROWS = 256
TQ = 256


def kernel(x_prompt, x_sample, c_prompt, c_sample, page_table, cache_k_cmp, cache_v_cmp, cache_k_sel, cache_v_sel, cache_k_win, cache_v_win, state_h, state_conv, w_ada, b_ada, ln_g, ln_b, a_w_in, a_conv_w, a_conv_b, a_w_r, a_b_r, a_w_i, a_b_i, a_lambda, a_w_out, w_kv, w_cmp_k, w_cmp_v, b_w_qg, b_w_o, f_w_up, f_w_down):
    nb, s_len, d = x_prompt.shape
    bd, sd, _ = x_sample.shape
    assert sd == 1
    depth = w_ada.shape[0]
    n_a = a_w_in.shape[0]
    assert depth - n_a == 1 or depth == n_a, "one shared-KV NSA layer stack"
    d_ff = f_w_down.shape[1]
    lru = a_w_out.shape[1]
    n_heads = d // HEAD_DIM
    gqa = n_heads // N_KV
    ckv = N_KV * HEAD_DIM
    past = page_table.shape[1] * PAGE_SIZE
    alpha = (2.0 * depth) ** 0.25
    t = nb * s_len
    tm_mm = min(TM_MM, s_len)
    tm_rows = min(TM_ROWS, s_len)
    tn = 512

    rows = -(-(nb + bd) // V7X_SUBLANES) * V7X_SUBLANES
    c_all = jnp.concatenate([c_prompt, c_sample, jnp.zeros((rows - nb - bd, d), F32)], axis=0)
    mods = _ada(c_all, w_ada, b_ada)

    def mod_p(l, k):
        return mods[l, :nb, k * d:(k + 1) * d].reshape(nb, 1, d)

    def mod_s(l, k):
        return mods[l, nb:nb + bd, k * d:(k + 1) * d].reshape(1, bd, d)

    slopes = jnp.exp2(-8.0 * jnp.arange(1, n_heads + 1, dtype=F32) / n_heads)
    slopes_col = slopes.reshape(N_KV, gqa, 1)

    xp = x_prompt.reshape(t, d)
    xs = x_sample.reshape(bd, d)
    up = _modulate(xp, mod_p(0, 0), mod_p(0, 1), tm_rows)
    us = _modulate(xs, mod_s(0, 0), mod_s(0, 1), bd)

    w_down_bf = f_w_down.astype(BF16)

    def ffn(x_p, x_s, u_p, u_s, l, nxt):
        hmid = _matmul(u_p, u_s, f_w_up, d_ff, layer=l, tm=tm_mm, tn=256, out_dtype=BF16,
                       epilogue=_swiglu_epilogue, col_off=0, col_off2=d_ff, name="ffn_up")
        ys = _matmul(hmid[0], hmid[1], w_down_bf, d, layer=l, tm=min(tm_mm, 512), tn=tn, out_dtype=F32,
                     name="ffn_down")
        outs = []
        for x, y, tm_r, mod in ((x_p, ys[0], tm_rows, mod_p), (x_s, ys[1], bd, mod_s)):
            if nxt is None:
                outs.append(_ln_mod(x, y, mod(l, 5), ln_g[l, 1], ln_b[l, 1], alpha, tm_r))
            else:
                outs.append(_ln_mod(x, y, mod(l, 5), ln_g[l, 1], ln_b[l, 1], alpha, tm_r,
                                    shift=mod(nxt[0], 0), scale=mod(nxt[0], 1), want_xb=nxt[1]))
        return outs

    h_p, h_s, cb_p, cb_s = [], [], [], []
    kv_p = kv_s = None
    for l in range(depth):
        if l < n_a:
            wr = a_w_r[l].astype(BF16)
            wi = a_w_i[l].astype(BF16)
            ggx_p, ggx_s = _matmul(up, us, a_w_in, lru, layer=l, tm=tm_mm, tn=tn, out_dtype=BF16,
                                   epilogue=_gelu_tanh, name="lru_in_g")
            rx_p, rx_s = _matmul(up, us, a_w_in, lru, layer=l, tm=tm_mm, tn=tn, out_dtype=F32, col_off=lru,
                                 name="lru_in_r")
            yin_p, hl_p, ct_p = _rglru_prompt(
                rx_p, ggx_p, jnp.zeros((nb, CONV_W - 1, lru), F32), jnp.zeros((nb, lru), F32),
                a_conv_w[l], a_conv_b[l], wr, a_b_r[l], wi, a_b_i[l], a_lambda[l], nb, min(TM_ROWS, s_len))
            yin_s, hl_s, ct_s = _rglru_step(rx_s, ggx_s, jnp.swapaxes(state_conv[l], 0, 1), state_h[l],
                                            a_conv_w[l], a_conv_b[l], wr, a_b_r[l], wi, a_b_i[l], a_lambda[l])
            h_p.append(hl_p.reshape(nb, lru))
            h_s.append(hl_s)
            cb_p.append(ct_p)
            cb_s.append(jnp.swapaxes(ct_s, 0, 1))
            y_p, y_s = _matmul(yin_p, yin_s, a_w_out, d, layer=l, tm=tm_mm, tn=tn, out_dtype=F32, name="lru_out")
        else:
            j = l - n_a
            w_g = b_w_qg[j][:, n_heads * HEAD_DIM:].reshape(d, N_KV, 3 * gqa)
            w_g = jnp.pad(w_g, ((0, 0), (0, 0), (0, V7X_LANES - 3 * gqa))).reshape(d, N_KV * V7X_LANES)
            q_p, q_s = _matmul(up, us, b_w_qg, d, layer=j, tm=tm_mm, tn=tn, out_dtype=BF16, name="nsa_q")
            g_p, g_s = _matmul(up, us, w_g, N_KV * V7X_LANES, tm=tm_mm, tn=N_KV * V7X_LANES, out_dtype=F32,
                               epilogue=jax.nn.sigmoid, name="nsa_gate")
            o_p = _nsa_prompt(q_p, g_p, ctx_p[0], ctx_p[1], kv_p[2], kv_p[3], kv_p[4], kv_p[5], slopes, nb,
                              min(TQ, s_len))
            q_s = q_s.reshape(bd * N_KV, gqa, HEAD_DIM)
            g_s = g_s.reshape(bd * N_KV, V7X_LANES)[:, :3 * gqa].reshape(bd * N_KV, gqa, 3)
            o_c, idx, okb = _nsa_sample_cmp(q_s, slopes_col, ctx_s, past)
            nsb = min(N_SEL, past // L_SEL + 1)
            o_s = _nsa_sample_sel(page_table, idx[:, 0, :nsb].reshape(-1), okb[:, 0, :nsb].reshape(-1),
                                  cache_k_sel, cache_v_sel, q_s, slopes_col, g_s, o_c,
                                  kv_s[2], kv_s[3], cache_k_win, cache_v_win, kv_s[4], kv_s[5], past)
            y_p, y_s = _matmul(o_p, o_s.reshape(bd, d), b_w_o, d, layer=j, tm=tm_mm, tn=tn, out_dtype=F32,
                               name="nsa_out")

        xp, up = _ln_mod(xp, y_p, mod_p(l, 2), ln_g[l, 0], ln_b[l, 0], alpha, tm_rows,
                         shift=mod_p(l, 3), scale=mod_p(l, 4))
        xs, us = _ln_mod(xs, y_s, mod_s(l, 2), ln_g[l, 0], ln_b[l, 0], alpha, bd,
                         shift=mod_s(l, 3), scale=mod_s(l, 4))
        last = l == depth - 1
        res_p, res_s = ffn(xp, xs, up, us, l, None if last else (l + 1, l == n_a - 1))
        if last:
            xp, xs = res_p[0], res_s[0]
        elif l == n_a - 1:
            xp, up, xbp = res_p
            xs, us, xbs = res_s
            kv = [_matmul(xbp, xbs, w_kv, ckv, tm=tm_mm, tn=ckv, out_dtype=F32, col_off=jj * ckv, name="kv_proj")
                  for jj in range(6)]
            kv_p = [a for a, _ in kv]
            kv_s = [a for _, a in kv]
            ctx_p = _cmp_blocks_prompt(kv_p[0], kv_p[1], w_cmp_k, w_cmp_v, nb)
            ctx_s = _page_chunks(page_table, cache_k_cmp, cache_v_cmp, w_cmp_k, w_cmp_v)
        else:
            xp, up = res_p
            xs, us = res_s

    wwin = min(WINDOW, s_len)
    shp_p = (nb, s_len, N_KV, HEAD_DIM)
    shp_s = (bd, sd, N_KV, HEAD_DIM)
    kvp = [a.reshape(shp_p) for a in kv_p]
    kvs = [a.reshape(shp_s) for a in kv_s]
    wbuf = cache_k_win.shape[1]
    k_win_s = jnp.concatenate([cache_k_win, kvs[4]], axis=1)[:, -wbuf:]
    v_win_s = jnp.concatenate([cache_v_win, kvs[5]], axis=1)[:, -wbuf:]
    return (xp.reshape(nb, s_len, d), xs.reshape(bd, sd, d),
            kvp[0], kvs[0], kvp[1], kvs[1], kvp[2], kvs[2], kvp[3], kvs[3],
            kvp[4][:, -wwin:], k_win_s, kvp[5][:, -wwin:], v_win_s,
            jnp.stack(h_p), jnp.stack(h_s), jnp.stack(cb_p), jnp.stack(cb_s))
```

```python
import functools

import numpy as np
import jax
import jax.numpy as jnp
from jax import lax
from jax.experimental import pallas as pl
from jax.experimental.pallas import tpu as pltpu

F32 = jnp.float32
BF16 = jnp.bfloat16

HEAD_DIM = 128
N_KV = 4
L_CMP = 32
D_CMP = 16
L_SEL = 64
N_SEL = 16
WINDOW = 512
PAGE_SIZE = 128
CONV_W = 4
LRU_BLOCKS = 16
LRU_C = 8.0
NEG = -1e30
FORCE_BONUS = 1e3
ATTN_SCALE = HEAD_DIM ** -0.5
LN_EPS = 1e-5
EXP2_SCALE = ATTN_SCALE * 1.4426950408889634
ATTN_ROW_CHUNK = 32
SEL_SHIFT = L_SEL.bit_length() - 1
assert 1 << SEL_SHIFT == L_SEL

V7X_VMEM_BYTES = 64 * 1024 * 1024
V7X_LANES = 128
V7X_SUBLANES = 8
VMEM_CAP = V7X_VMEM_BYTES - 3 * 1024 * 1024


def _params(sem, vmem_bytes):
    return pltpu.CompilerParams(dimension_semantics=sem,
                                vmem_limit_bytes=int(min(max(vmem_bytes, 16 << 20), VMEM_CAP)))


def _nt_dot(a, b, **kw):
    return lax.dot_general(a, b, (((1,), (1,)), ((), ())), preferred_element_type=F32, **kw)


def _ada_kernel(c_ref, w_ref, b_ref, o_ref):
    c = c_ref[...]
    s = (c * jax.nn.sigmoid(c)).astype(BF16)
    o_ref[...] = jnp.dot(s, w_ref[...].astype(BF16), preferred_element_type=F32) + b_ref[...]


def _ada(c_all, w_ada, b_ada, tn=1024):
    depth, d, n6 = w_ada.shape
    r = c_all.shape[0]
    return pl.pallas_call(
        _ada_kernel,
        grid=(depth, n6 // tn),
        in_specs=[pl.BlockSpec((r, d), lambda l, n: (0, 0)),
                  pl.BlockSpec((None, d, tn), lambda l, n: (l, 0, n)),
                  pl.BlockSpec((None, 1, tn), lambda l, n: (l, 0, n))],
        out_specs=pl.BlockSpec((None, r, tn), lambda l, n: (l, 0, n)),
        out_shape=jax.ShapeDtypeStruct((depth, r, n6), F32),
        compiler_params=_params(("arbitrary", "arbitrary"), 2 * d * tn * 4 + d * tn * 2 + (8 << 20)),
        name="ada",
    )(c_all, w_ada, b_ada.reshape(depth, 1, n6))


def _modulate_kernel(x_ref, sh_ref, sc_ref, u_ref):
    u_ref[...] = (x_ref[...] * (1.0 + sc_ref[...]) + sh_ref[...]).astype(u_ref.dtype)


def _row_specs(t, d, tm, nb):
    tiles_per_batch = (t // nb) // tm
    xspec = pl.BlockSpec((tm, d), lambda i: (i, 0))

    def vspec(r):
        return pl.BlockSpec((None, r, d), lambda i: (i // tiles_per_batch, 0, 0))
    return xspec, vspec


def _modulate(x, shift, scale, tm):
    t, d = x.shape
    nb, r, _ = shift.shape
    xspec, vspec = _row_specs(t, d, tm, nb)
    return pl.pallas_call(
        _modulate_kernel,
        grid=(t // tm,),
        in_specs=[xspec, vspec(r), vspec(r)],
        out_specs=xspec,
        out_shape=jax.ShapeDtypeStruct((t, d), BF16),
        compiler_params=_params(("arbitrary",), 6 * tm * d * 4),
        name="modulate",
    )(x, shift, scale)


def _ln_mod_kernel(*refs, alpha, want_u, want_xb):
    x_ref, y_ref, gate_ref, g_ref, b_ref = refs[:5]
    pos = 5
    if want_u:
        sh_ref, sc_ref = refs[pos:pos + 2]
        pos += 2
    xo_ref = refs[pos]
    pos += 1
    v = alpha * x_ref[...] + (1.0 + gate_ref[...]) * y_ref[...]
    mu = jnp.mean(v, axis=-1, keepdims=True)
    dlt = v - mu
    var = jnp.mean(dlt * dlt, axis=-1, keepdims=True)
    xn = dlt * lax.rsqrt(var + LN_EPS) * g_ref[...] + b_ref[...]
    xo_ref[...] = xn
    if want_u:
        refs[pos][...] = (xn * (1.0 + sc_ref[...]) + sh_ref[...]).astype(BF16)
        pos += 1
    if want_xb:
        refs[pos][...] = xn.astype(BF16)


def _ln_mod(x, y, gate, ln_g, ln_b, alpha, tm, shift=None, scale=None, want_xb=False):
    t, d = x.shape
    nb, r, _ = gate.shape
    xspec, vspec = _row_specs(t, d, tm, nb)
    pspec = pl.BlockSpec((1, d), lambda i: (0, 0))
    want_u = shift is not None
    ins = [x, y, gate, ln_g.reshape(1, d), ln_b.reshape(1, d)]
    in_specs = [xspec, xspec, vspec(r), pspec, pspec]
    out_shape = [jax.ShapeDtypeStruct((t, d), F32)]
    out_specs = [xspec]
    if want_u:
        ins += [shift, scale]
        in_specs += [vspec(r), vspec(r)]
        out_shape.append(jax.ShapeDtypeStruct((t, d), BF16))
        out_specs.append(xspec)
    if want_xb:
        out_shape.append(jax.ShapeDtypeStruct((t, d), BF16))
        out_specs.append(xspec)
    return pl.pallas_call(
        functools.partial(_ln_mod_kernel, alpha=alpha, want_u=want_u, want_xb=want_xb),
        grid=(t // tm,),
        in_specs=in_specs, out_specs=out_specs, out_shape=out_shape,
        compiler_params=_params(("arbitrary",), 12 * tm * d * 4),
        name="ln_mod",
    )(*ins)


def _gelu_tanh(x):
    return jax.nn.gelu(x, approximate=True)


def _mm_kernel(*refs, n_w, cast_w, epilogue, heads):
    x_ref, xs_ref = refs[:2]
    w_refs = refs[2:2 + n_w]
    n_out = 4 if heads else 2
    outs = refs[2 + n_w:2 + n_w + n_out]
    scr = refs[2 + n_w + n_out:]
    first_row_tile = pl.program_id(1) == 0
    if cast_w:
        @pl.when(first_row_tile)
        def _():
            for w_ref, s_ref in zip(w_refs, scr):
                s_ref[...] = w_ref[...].astype(BF16)
        w_refs = scr

    def product(rows_ref, out_ref, heads_ref):
        rows = rows_ref[...]
        accs = [jnp.dot(rows, w_ref[...], preferred_element_type=F32) for w_ref in w_refs]
        val = epilogue(*accs).astype(out_ref.dtype)
        out_ref[...] = val
        if heads_ref is not None:
            hd = heads_ref.shape[-1]
            for h in range(heads_ref.shape[-2]):
                heads_ref[:, h, :] = val[:, h * hd:(h + 1) * hd]

    product(x_ref, outs[0], outs[2] if heads else None)
    pl.when(first_row_tile)(lambda: product(xs_ref, outs[1], outs[3] if heads else None))


def _matmul(x, xs, w, n_out, *, tm, tn, out_dtype, layer=None, epilogue=None, col_off=0, col_off2=None,
            heads=None, name="mm"):
    m, k = x.shape
    ms = xs.shape[0]
    assert m % tm == 0 and n_out % tn == 0 and col_off % tn == 0
    assert (w.ndim == 3) == (layer is not None) and w.shape[-2] == k and xs.shape[1] == k
    offs = [col_off // tn] + ([] if col_off2 is None else [col_off2 // tn])
    n_w = len(offs)
    if epilogue is None:
        epilogue = lambda a: a
    cast_w = w.dtype != BF16
    scratch = [pltpu.VMEM((k, tn), BF16) for _ in range(n_w)] if cast_w else []
    w_bytes = k * tn * w.dtype.itemsize
    vmem = (2 * tm * k * 2 + n_w * (2 * w_bytes + len(scratch) * k * tn * 2)
            + 2 * tm * tn * 4 * (n_w + 1) + (4 << 20))
    if layer is None:
        w_specs = [pl.BlockSpec((k, tn), functools.partial(lambda n, i, o: (0, n + o), o=o)) for o in offs]
    else:
        w_specs = [pl.BlockSpec((None, k, tn), functools.partial(lambda n, i, o: (layer, 0, n + o), o=o))
                   for o in offs]
    out_specs = [pl.BlockSpec((tm, tn), lambda n, i: (i, n)), pl.BlockSpec((ms, tn), lambda n, i: (0, n))]
    out_shape = [jax.ShapeDtypeStruct((m, n_out), out_dtype), jax.ShapeDtypeStruct((ms, n_out), out_dtype)]
    if heads is not None:
        assert n_out == tn == heads[0] * heads[1]
        out_specs += [pl.BlockSpec((tm,) + heads, lambda n, i: (i, 0, 0)),
                      pl.BlockSpec((ms,) + heads, lambda n, i: (0, 0, 0))]
        out_shape += [jax.ShapeDtypeStruct((m,) + heads, out_dtype), jax.ShapeDtypeStruct((ms,) + heads, out_dtype)]
        vmem += 4 * tm * tn * 4
    return pl.pallas_call(
        functools.partial(_mm_kernel, n_w=n_w, cast_w=cast_w, epilogue=epilogue, heads=heads is not None),
        grid=(n_out // tn, m // tm),
        in_specs=[pl.BlockSpec((tm, k), lambda n, i: (i, 0)), pl.BlockSpec((ms, k), lambda n, i: (0, 0))]
        + w_specs,
        out_specs=out_specs, out_shape=out_shape,
        scratch_shapes=scratch,
        compiler_params=_params(("arbitrary", "arbitrary"), vmem),
        name=name,
    )(x, xs, *([w] * n_w))


def _swiglu_epilogue(g, v):
    return g * jax.nn.sigmoid(g) * v


def _softplus_neg(lam):
    return jnp.maximum(-lam, 0.0) + jnp.log1p(jnp.exp(-jnp.abs(lam)))


def _lru_gates(xc, wr_ref, br_ref, wi_ref, bi_ref, sp, a_ref, b_ref, nblk, bw):
    def put(ref, k, val):
        ref[:, k * bw:(k + 1) * bw] = val

    nsp = (-LRU_C) * sp
    xcb = xc.astype(BF16)
    for k in range(nblk):
        cs = slice(k * bw, (k + 1) * bw)
        xb = xcb[:, cs]
        r = jax.nn.sigmoid(jnp.dot(xb, wr_ref[k], preferred_element_type=F32) + br_ref[:, cs])
        ig = jax.nn.sigmoid(jnp.dot(xb, wi_ref[k], preferred_element_type=F32) + bi_ref[:, cs])
        log_a = r * nsp[:, cs]
        put(a_ref, k, jnp.exp(log_a))
        th = jnp.tanh(log_a)
        put(b_ref, k, jnp.sqrt(-2.0 * th / (1.0 - th)) * (ig * xc[:, cs]))


SCAN_TILES = 8


def _rglru_kernel(rx_ref, ggx_ref, cb0_ref, h0_ref, cw_ref, cbias_ref, wr_ref, br_ref, wi_ref, bi_ref,
                  lam_ref, y_ref, hlast_ref, ctail_ref, xpad_ref, a_ref, b_ref, hcar_ref, *, tm, nblk, bw):
    s = pl.program_id(1)
    width = nblk * bw
    pad = V7X_SUBLANES
    tail = CONV_W - 1

    @pl.when(s == 0)
    def _():
        xpad_ref[pad - tail:pad, :] = cb0_ref[...]
        hcar_ref[...] = h0_ref[...]

    rx = rx_ref[...]
    xpad_ref[pad:pad + tm, :] = rx
    cw = cw_ref[...]
    xc = cbias_ref[...] + rx * cw[tail:tail + 1]
    for j in range(tail):
        xc = xc + xpad_ref[pad - tail + j:pad - tail + j + tm, :] * cw[j:j + 1]
    xpad_ref[pad - tail:pad, :] = rx[tm - tail:tm, :]

    _lru_gates(xc, wr_ref, br_ref, wi_ref, bi_ref, _softplus_neg(lam_ref[...]), a_ref, b_ref, nblk, bw)

    scan_lanes = min(SCAN_TILES * V7X_LANES, width)
    row = lax.broadcasted_iota(jnp.int32, (V7X_SUBLANES, scan_lanes), 0)
    for c in range(width // scan_lanes):
        cs = slice(c * scan_lanes, (c + 1) * scan_lanes)

        def group(gi, h, cs=cs):
            r0 = pl.multiple_of(gi * V7X_SUBLANES, V7X_SUBLANES)
            a = a_ref[pl.ds(r0, V7X_SUBLANES), cs]
            b = b_ref[pl.ds(r0, V7X_SUBLANES), cs]
            for sft in (1, 2, 4):
                keep = row >= sft
                a_s = jnp.where(keep, pltpu.roll(a, sft, 0), 1.0)
                b_s = jnp.where(keep, pltpu.roll(b, sft, 0), 0.0)
                b = a * b_s + b
                a = a * a_s
            hs = a * h + b
            b_ref[pl.ds(r0, V7X_SUBLANES), cs] = hs
            return jnp.broadcast_to(hs[V7X_SUBLANES - 1:V7X_SUBLANES, :], hs.shape)

        h_in = jnp.broadcast_to(hcar_ref[:, cs], (V7X_SUBLANES, scan_lanes))
        h_out = lax.fori_loop(0, tm // V7X_SUBLANES, group, h_in)
        hcar_ref[:, cs] = h_out[0:1, :]

    y_ref[...] = (ggx_ref[...].astype(F32) * b_ref[...]).astype(BF16)

    @pl.when(s == pl.num_programs(1) - 1)
    def _():
        hlast_ref[...] = hcar_ref[...]
        ctail_ref[...] = rx[tm - tail:tm, :]


def _rglru_prompt(rx, ggx, conv_buf, h0, cw, cbias, wr, br, wi, bi, lam, nb, tm):
    t, width = rx.shape
    s_len = t // nb
    nblk, bw, _ = wr.shape
    tail = CONV_W - 1
    ns = s_len // tm
    row = lambda a: a.reshape(1, width)
    xspec = pl.BlockSpec((tm, width), lambda b, s: (b * ns + s, 0))
    pspec = pl.BlockSpec((1, width), lambda b, s: (0, 0))
    wspec = pl.BlockSpec((nblk, bw, bw), lambda b, s: (0, 0, 0))
    vmem = (2 * tm * width * (4 + 2 + 2) + 3 * (tm + 8) * width * 4 + 4 * nblk * bw * bw * 2 + (6 << 20))
    return pl.pallas_call(
        functools.partial(_rglru_kernel, tm=tm, nblk=nblk, bw=bw),
        grid=(nb, ns),
        in_specs=[xspec, xspec,
                  pl.BlockSpec((None, tail, width), lambda b, s: (b, 0, 0)),
                  pl.BlockSpec((None, 1, width), lambda b, s: (b, 0, 0)),
                  pl.BlockSpec((CONV_W, width), lambda b, s: (0, 0)), pspec,
                  wspec, pspec, wspec, pspec, pspec],
        out_specs=[xspec,
                   pl.BlockSpec((None, 1, width), lambda b, s: (b, 0, 0)),
                   pl.BlockSpec((None, tail, width), lambda b, s: (b, 0, 0))],
        out_shape=[jax.ShapeDtypeStruct((t, width), BF16),
                   jax.ShapeDtypeStruct((nb, 1, width), F32),
                   jax.ShapeDtypeStruct((nb, tail, width), F32)],
        scratch_shapes=[pltpu.VMEM((tm + V7X_SUBLANES, width), F32), pltpu.VMEM((tm, width), F32),
                        pltpu.VMEM((tm, width), F32), pltpu.VMEM((1, width), F32)],
        compiler_params=_params(("arbitrary", "arbitrary"), vmem),
        name="rglru",
    )(rx, ggx, conv_buf, h0.reshape(nb, 1, width), cw, row(cbias), wr, row(br), wi, row(bi), row(lam))


def _rglru_step_kernel(rx_ref, ggx_ref, cb_ref, h0_ref, cw_ref, cbias_ref, wr_ref, br_ref, wi_ref, bi_ref,
                       lam_ref, y_ref, h_ref, cnew_ref, a_ref, b_ref, *, nblk, bw):
    tail = CONV_W - 1
    rx = rx_ref[...]
    cw = cw_ref[...]
    xc = cbias_ref[...] + rx * cw[tail:tail + 1]
    for j in range(tail):
        xc = xc + cb_ref[j] * cw[j:j + 1]
    _lru_gates(xc, wr_ref, br_ref, wi_ref, bi_ref, _softplus_neg(lam_ref[...]), a_ref, b_ref, nblk, bw)
    h = a_ref[...] * h0_ref[...] + b_ref[...]
    h_ref[...] = h
    y_ref[...] = (ggx_ref[...].astype(F32) * h).astype(BF16)
    for j in range(tail - 1):
        cnew_ref[j] = cb_ref[j + 1]
    cnew_ref[tail - 1] = rx


def _rglru_step(rx, ggx, conv_t, h0, cw, cbias, wr, br, wi, bi, lam):
    bd, width = rx.shape
    nblk, bw, _ = wr.shape
    row = lambda a: a.reshape(1, width)
    return pl.pallas_call(
        functools.partial(_rglru_step_kernel, nblk=nblk, bw=bw),
        out_shape=[jax.ShapeDtypeStruct((bd, width), BF16), jax.ShapeDtypeStruct((bd, width), F32),
                   jax.ShapeDtypeStruct(conv_t.shape, F32)],
        scratch_shapes=[pltpu.VMEM((bd, width), F32), pltpu.VMEM((bd, width), F32)],
        compiler_params=pltpu.CompilerParams(vmem_limit_bytes=32 << 20),
        name="rglru_step",
    )(rx, ggx, conv_t, h0, cw, row(cbias), wr, row(br), wi, row(bi), row(lam))


def _chunk_sums(x, w):
    n = x.shape[0] // D_CMP
    x3 = x.reshape(n, D_CMP, x.shape[1])
    first = jnp.sum(x3 * w[None, :D_CMP, :], axis=1)
    second = jnp.sum(x3 * w[None, D_CMP:, :], axis=1)
    return first, second


def _blocks_from_sums(first, second):
    n = first.shape[0]
    nxt = pltpu.roll(second, n - 1, 0)
    rows = lax.broadcasted_iota(jnp.int32, first.shape, 0)
    return jnp.where(rows < n - 1, first + nxt, 0.0)


COEF_PIECES = 3
POS_LOW_BITS = 7


def _pos_lanes(pos):
    lane = lax.broadcasted_iota(jnp.int32, pos.shape, 1)
    hi = ((pos >> POS_LOW_BITS) << POS_LOW_BITS).astype(F32)
    lo = (pos & ((1 << POS_LOW_BITS) - 1)).astype(F32)
    return jnp.where(lane < COEF_PIECES, hi, jnp.where(lane < 2 * COEF_PIECES, lo, 0.0)).astype(BF16)


def _coef_lanes(coef, shape):
    v = jnp.full(shape, coef, F32)
    c1 = v.astype(BF16).astype(F32)
    c2 = (v - c1).astype(BF16).astype(F32)
    c3 = v - c1 - c2
    k = lax.broadcasted_iota(jnp.int32, shape, 1)
    k = jnp.where(k >= COEF_PIECES, k - COEF_PIECES, k)
    live = lax.broadcasted_iota(jnp.int32, shape, 1) < 2 * COEF_PIECES
    out = jnp.where(k == 0, c1, jnp.where(k == 1, c2, c3))
    return jnp.where(live, out, 0.0).astype(BF16)


def _cmp_blocks_kernel(k_ref, v_ref, wk_ref, wv_ref, kc_ref, vc_ref):
    hd = k_ref.shape[1]
    nch = kc_ref.shape[0]
    kc_ref[:, :hd] = _blocks_from_sums(*_chunk_sums(k_ref[...], wk_ref[...])).astype(BF16)
    ends = (lax.broadcasted_iota(jnp.int32, (nch, V7X_LANES), 0) + 2) * D_CMP - 1
    kc_ref[:, hd:] = _pos_lanes(ends)
    vc_ref[...] = _blocks_from_sums(*_chunk_sums(v_ref[...], wv_ref[...])).astype(BF16)


def _cmp_blocks_prompt(k_c, v_c, wk, wv, nb):
    t, c = k_c.shape
    s_len = t // nb
    nch = s_len // D_CMP
    kvspec = pl.BlockSpec((s_len, HEAD_DIM), lambda b, h: (b, h))
    wspec = pl.BlockSpec((L_CMP, HEAD_DIM), lambda b, h: (0, h))

    def ospec(w):
        return pl.BlockSpec((None, None, nch, w), lambda b, h: (b, h, 0, 0))
    return pl.pallas_call(
        _cmp_blocks_kernel, grid=(nb, N_KV),
        in_specs=[kvspec, kvspec, wspec, wspec],
        out_specs=[ospec(HEAD_DIM + V7X_LANES), ospec(HEAD_DIM)],
        out_shape=[jax.ShapeDtypeStruct((nb, N_KV, nch, HEAD_DIM + V7X_LANES), BF16),
                   jax.ShapeDtypeStruct((nb, N_KV, nch, HEAD_DIM), BF16)],
        compiler_params=_params(("arbitrary", "arbitrary"), 24 << 20),
        name="cmp_blocks",
    )(k_c, v_c, wk.reshape(L_CMP, c), wv.reshape(L_CMP, c))


def _block_scores_topk_t(imp_t, q0, nsel, tq):
    nsp = -(-nsel // V7X_SUBLANES) * V7X_SUBLANES
    jt = lax.broadcasted_iota(jnp.int32, (nsp, tq), 0)
    qpos = q0 + lax.broadcasted_iota(jnp.int32, (nsp, tq), 1)
    cur = qpos >> SEL_SHIFT
    valid = (jt <= cur) & (jt < nsel)
    forced = (jt == 0) | (jt == cur) | (jt == cur - 1)
    score = jnp.where(valid, imp_t[:nsp] + jnp.where(forced, FORCE_BONUS, 0.0), NEG)
    rank = jnp.zeros((nsp, tq), F32)
    for i in range(nsel):
        ri = score[i:i + 1, :]
        beats = (ri > score) | ((ri == score) & (jt > i))
        rank = rank + jnp.where(beats, 1.0, 0.0)
    return jnp.where((rank < float(min(N_SEL, nsel))) & valid, 1.0, 0.0)


def _nsa_prompt_kernel(slopes_ref, q_ref, gt_ref, kc_ref, vc_ref, ks_ref, vs_ref, kw_ref, vw_ref, e_ref,
                       o_ref, ksa, vsb, kwa, vwb, qa_ref, s_ref, p_ref, bsel_ref, bwin_ref, bcmp_ref,
                       m_ref, acc_ref, oc_ref, os_ref, ow_ref, psum_ref, *, tq, gqa, s_len):
    h = pl.program_id(1)
    i = pl.program_id(2)
    q0 = i * tq
    hd = HEAD_DIM
    ncp = kc_ref.shape[0]
    n_cmp = ncp - 1
    nsel = s_len // L_SEL
    nkt = s_len // tq
    rows = gqa * tq
    rc = ATTN_ROW_CHUNK
    per_g = tq // rc
    lane_tiles = tq // V7X_LANES
    r_io = lax.broadcasted_iota(jnp.int32, (tq, tq), 0)
    c_io = lax.broadcasted_iota(jnp.int32, (tq, tq), 1)

    @pl.when(i == 0)
    def _():
        pcols = _pos_lanes(lax.broadcasted_iota(jnp.int32, (s_len, V7X_LANES), 0))
        ksa[:, :hd] = ks_ref[...].astype(BF16)
        ksa[:, hd:] = pcols
        kwa[:, :hd] = kw_ref[...].astype(BF16)
        kwa[:, hd:] = pcols
        ones = jnp.ones((s_len, V7X_LANES), BF16)
        vsb[:, :hd] = vs_ref[...].astype(BF16)
        vsb[:, hd:] = ones
        vwb[:, :hd] = vw_ref[...].astype(BF16)
        vwb[:, hd:] = ones
        wshape = bwin_ref.shape[1:]
        rw = lax.broadcasted_iota(jnp.int32, wshape, 0)
        cw = lax.broadcasted_iota(jnp.int32, wshape, 1)
        for v in range(bwin_ref.shape[0]):
            dist = min(v * tq, WINDOW) + rw - cw
            bwin_ref[v] = jnp.where((dist >= 0) & (dist <= WINDOW), 0.0, NEG)

    for g in range(gqa):
        rs = slice(g * tq, (g + 1) * tq)
        qa_ref[rs, :hd] = q_ref[:, g * hd:(g + 1) * hd]
        qa_ref[rs, hd:] = _coef_lanes(slopes_ref[h * gqa + g] * (1.0 / ATTN_SCALE), (tq, V7X_LANES))

    def head_rows(c, g):
        if isinstance(c, int):
            return slice(g * tq + c * rc, g * tq + (c + 1) * rc)
        return pl.ds(pl.multiple_of(g * tq + c * rc, rc), rc)

    def tile_rows(c):
        if isinstance(c, int):
            return slice(c * rc, (c + 1) * rc)
        return pl.ds(pl.multiple_of(c * rc, rc), rc)

    lanes = [slice(j * V7X_LANES, (j + 1) * V7X_LANES) for j in range(lane_tiles)]

    s_ref[:, :ncp] = _nt_dot(qa_ref[...], kc_ref[...])
    n_io = lax.broadcasted_iota(jnp.int32, (tq, ncp), 1)
    qpos_c = q0 + lax.broadcasted_iota(jnp.int32, (tq, ncp), 0)
    bcmp_ref[...] = jnp.where((qpos_c >= (n_io + 2) * D_CMP - 1) & (n_io < n_cmp), 0.0, NEG)

    def cmp_chunk(c, carry):
        br = tile_rows(c)
        bias = bcmp_ref[br, :]
        live = bias == 0.0
        tot = jnp.zeros((rc, ncp), F32)
        for g in range(gqa):
            rr = head_rows(c, g)
            s = s_ref[rr, :ncp] + bias
            p = jnp.where(live, jnp.exp2((s - jnp.max(s, axis=1, keepdims=True)) * EXP2_SCALE), 0.0)
            p = p / jnp.maximum(jnp.sum(p, axis=1, keepdims=True), 1e-30)
            tot = tot + p
            p_ref[rr, :ncp] = p.astype(BF16)
        psum_ref[br, :] = tot
        return carry

    for c in range(per_g):
        cmp_chunk(c, 0)
    oc_ref[...] = jnp.dot(p_ref[:, :ncp], vc_ref[...], preferred_element_type=F32)
    psum = psum_ref[...]

    j_io = lax.broadcasted_iota(jnp.int32, (V7X_LANES, ncp), 0)
    n_io2 = lax.broadcasted_iota(jnp.int32, (V7X_LANES, ncp), 1)
    cover_t = jnp.where((n_io2 * D_CMP < j_io * L_SEL + L_SEL) & (n_io2 * D_CMP + L_CMP > j_io * L_SEL)
                        & (n_io2 < n_cmp) & (j_io < nsel), 1.0, 0.0)
    imp_t = _nt_dot(cover_t, psum, precision=lax.Precision.HIGHEST)
    sel_t = _block_scores_topk_t(imp_t, q0, nsel, tq)
    sel_t = jnp.concatenate([sel_t, jnp.zeros((V7X_LANES - sel_t.shape[0], tq), F32)], axis=0)
    sel = sel_t.T.astype(BF16)
    for kt in range(nkt):
        @pl.when(kt <= i)
        def _(kt=kt):
            picked = jnp.dot(sel, e_ref[:, kt * tq:(kt + 1) * tq], preferred_element_type=F32)
            causal_slack = jnp.where(kt < i, tq, 0)
            bsel_ref[kt] = jnp.where((picked > 0.5) & (c_io <= r_io + causal_slack), 0.0, NEG)

    def softmax_passes(width, bias_ref, first):
        lanes = [slice(j * V7X_LANES, (j + 1) * V7X_LANES) for j in range(width // V7X_LANES)]
        for c in range(per_g):
            br = tile_rows(c)
            bias = [bias_ref[br, ls] for ls in lanes]
            for g in range(gqa):
                rr = head_rows(c, g)
                top = functools.reduce(jnp.maximum, [s_ref[rr, ls] + b for ls, b in zip(lanes, bias)])
                m_new = jnp.max(top, axis=1, keepdims=True)
                if first:
                    m_ref[rr, :] = jnp.broadcast_to(m_new, (rc, V7X_LANES))
                else:
                    m_prev = m_ref[rr, :]
                    m_new = jnp.maximum(m_prev, m_new)
                    alpha = jnp.exp2((m_prev - m_new) * EXP2_SCALE)
                    m_ref[rr, :] = m_new
                    acc_ref[rr, :hd] = alpha * acc_ref[rr, :hd]
                    acc_ref[rr, hd:] = alpha * acc_ref[rr, hd:]
        for c in range(per_g):
            br = tile_rows(c)
            bias = [bias_ref[br, ls] for ls in lanes]
            for g in range(gqa):
                rr = head_rows(c, g)
                m_new = m_ref[rr, :]
                for ls, b in zip(lanes, bias):
                    p_ref[rr, ls] = jnp.exp2((s_ref[rr, ls] + b - m_new) * EXP2_SCALE).astype(BF16)

    def sel_tile(kt, first):
        k0 = kt * tq if first else pl.multiple_of(kt * tq, tq)
        s_ref[:, :tq] = _nt_dot(qa_ref[...], ksa[pl.ds(k0, tq), :])
        softmax_passes(tq, bsel_ref.at[kt], first)
        pv = jnp.dot(p_ref[:, :tq], vsb[pl.ds(k0, tq), :], preferred_element_type=F32)
        if first:
            acc_ref[...] = pv
        else:
            acc_ref[...] += pv

    sel_tile(0, True)

    def sel_rest(kt, carry):
        sel_tile(kt, False)
        return carry

    lax.fori_loop(1, i + 1, sel_rest, 0)
    os_ref[...] = acc_ref[:, :hd] / jnp.maximum(acc_ref[:, hd:], 1e-30)

    wk = bwin_ref.shape[2]
    k0w = pl.multiple_of(jnp.maximum(q0 - WINDOW, 0), tq)
    s_ref[...] = _nt_dot(qa_ref[...], kwa[pl.ds(k0w, wk), :])
    softmax_passes(wk, bwin_ref.at[jnp.minimum(i, bwin_ref.shape[0] - 1)], True)
    pv = jnp.dot(p_ref[...], vwb[pl.ds(k0w, wk), :], preferred_element_type=F32)
    ow_ref[...] = pv[:, :hd] / jnp.maximum(pv[:, hd:], 1e-30)

    gt = gt_ref[...]
    for g in range(gqa):
        rs = slice(g * tq, (g + 1) * tq)
        o = (gt[:, 3 * g:3 * g + 1] * oc_ref[rs, :] + gt[:, 3 * g + 1:3 * g + 2] * os_ref[rs, :]
             + gt[:, 3 * g + 2:3 * g + 3] * ow_ref[rs, :])
        o_ref[:, g * hd:(g + 1) * hd] = o.astype(BF16)


def _nsa_prompt(q, gates, kc, vc, k_s, v_s, k_w, v_w, slopes, nb, tq):
    t, dq = q.shape
    s_len = t // nb
    gqa = dq // (N_KV * HEAD_DIM)
    nq = s_len // tq
    ncp = kc.shape[2]
    assert WINDOW % tq == 0 and s_len % tq == 0 and s_len // L_SEL <= V7X_LANES
    expand = np.zeros((V7X_LANES, s_len), np.float32)
    expand[np.arange(s_len) // L_SEL, np.arange(s_len)] = 1.0
    assert ncp == V7X_LANES and tq % ATTN_ROW_CHUNK == 0 and (tq // ATTN_ROW_CHUNK) & (tq // ATTN_ROW_CHUNK - 1) == 0
    rows = gqa * tq
    hd = HEAD_DIM
    qspec = pl.BlockSpec((tq, gqa * hd), lambda b, h, i, sl: (b * nq + i, h))
    gspec = pl.BlockSpec((tq, V7X_LANES), lambda b, h, i, sl: (b * nq + i, h))

    def cspec(w):
        return pl.BlockSpec((None, None, ncp, w), lambda b, h, i, sl: (b, h, 0, 0))
    kvspec = pl.BlockSpec((s_len, hd), lambda b, h, i, sl: (b, h))
    espec = pl.BlockSpec((V7X_LANES, s_len), lambda b, h, i, sl: (0, 0))
    ka_scr = pltpu.VMEM((s_len, hd + V7X_LANES), BF16)
    v_scr = pltpu.VMEM((s_len, hd + V7X_LANES), BF16)
    row_scr = pltpu.VMEM((rows, hd), F32)
    return pl.pallas_call(
        functools.partial(_nsa_prompt_kernel, tq=tq, gqa=gqa, s_len=s_len),
        grid_spec=pltpu.PrefetchScalarGridSpec(
            num_scalar_prefetch=1, grid=(nb, N_KV, nq),
            in_specs=[qspec, gspec, cspec(hd + V7X_LANES), cspec(hd), kvspec, kvspec, kvspec, kvspec, espec],
            out_specs=qspec,
            scratch_shapes=[ka_scr, v_scr, ka_scr, v_scr,
                            pltpu.VMEM((rows, hd + V7X_LANES), BF16),
                            pltpu.VMEM((rows, WINDOW + tq), F32), pltpu.VMEM((rows, WINDOW + tq), BF16),
                            pltpu.VMEM((nq, tq, tq), F32),
                            pltpu.VMEM((WINDOW // tq + 1, tq, WINDOW + tq), F32),
                            pltpu.VMEM((tq, ncp), F32),
                            pltpu.VMEM((rows, V7X_LANES), F32),
                            pltpu.VMEM((rows, hd + V7X_LANES), F32),
                            row_scr, row_scr, row_scr,
                            pltpu.VMEM((tq, ncp), F32)]),
        out_shape=jax.ShapeDtypeStruct((t, dq), BF16),
        compiler_params=_params(("arbitrary", "arbitrary", "arbitrary"), 48 << 20),
        name="nsa_prompt",
    )(slopes, q, gates, kc, vc, k_s, v_s, k_w, v_w, jnp.asarray(expand, BF16))


def _page_chunks_kernel(pt_ref, *refs, pages):
    k_pages = refs[:pages]
    v_pages = refs[pages:2 * pages]
    wk_ref, wv_ref, fk_ref, sk_ref, fv_ref, sv_ref = refs[2 * pages:]
    per = PAGE_SIZE // D_CMP

    def sums(x, w):
        x4 = x.reshape((per, D_CMP) + x.shape[1:])
        return jnp.sum(x4 * w[None, :D_CMP], axis=1), jnp.sum(x4 * w[None, D_CMP:], axis=1)

    for p in range(pages):
        rs = slice(p * per, (p + 1) * per)
        fk_ref[rs], sk_ref[rs] = sums(k_pages[p][...], wk_ref[...])
        fv_ref[rs], sv_ref[rs] = sums(v_pages[p][...], wv_ref[...])


def _page_chunks(page_table, pool_k, pool_v, wk, wv, pages=16):
    bd, n_pages = page_table.shape
    n_pool, _, n_kv, hd = pool_k.shape
    per = PAGE_SIZE // D_CMP
    nch = n_pages * per
    assert n_pages % pages == 0

    def page_spec(p):
        return pl.BlockSpec((None, PAGE_SIZE, n_kv, hd), lambda b, t, pt: (pt[b, t * pages + p], 0, 0, 0))
    wspec = pl.BlockSpec((L_CMP, n_kv, hd), lambda b, t, pt: (0, 0, 0))
    ospec = pl.BlockSpec((None, pages * per, n_kv, hd), lambda b, t, pt: (b, t, 0, 0))
    osh = jax.ShapeDtypeStruct((bd, nch, n_kv, hd), F32)
    return pl.pallas_call(
        functools.partial(_page_chunks_kernel, pages=pages),
        grid_spec=pltpu.PrefetchScalarGridSpec(
            num_scalar_prefetch=1, grid=(bd, n_pages // pages),
            in_specs=[page_spec(p) for p in range(pages)] * 2 + [wspec, wspec],
            out_specs=[ospec] * 4),
        out_shape=[osh] * 4,
        compiler_params=_params(("arbitrary", "arbitrary"), 52 << 20),
        name="page_chunks",
    )(page_table, *([pool_k] * pages), *([pool_v] * pages), wk, wv)


def _nsa_sample_cmp_kernel(q_ref, sl_ref, fk_ref, sk_ref, fv_ref, sv_ref, cover_ref, oc_ref, idx_ref, ok_ref,
                           *, past, n_sel_blocks):
    n_kv, gqa, _ = q_ref.shape
    nch = fk_ref.shape[0]
    n_cmp = nch - 1
    nj = cover_ref.shape[1]
    q_pos = past
    n_io = lax.broadcasted_iota(jnp.int32, (gqa, nch), 1)
    dist = (q_pos - ((n_io + 2) * D_CMP - 1)).astype(F32)
    ok = (dist >= 0.0) & (n_io < n_cmp)
    j_io = lax.broadcasted_iota(jnp.int32, (gqa, nj), 1)
    cur = q_pos // L_SEL
    valid = (j_io <= cur) & (j_io < n_sel_blocks)
    forced = (j_io == 0) | (j_io == cur) | (j_io == cur - 1)
    lane = lax.broadcasted_iota(jnp.int32, (gqa, V7X_LANES), 1)
    j_f = j_io.astype(F32)
    for h in range(n_kv):
        kc = _blocks_from_sums(fk_ref[:, h, :], sk_ref[:, h, :]).astype(BF16)
        vc = _blocks_from_sums(fv_ref[:, h, :], sv_ref[:, h, :]).astype(BF16)
        s = jnp.where(ok, _nt_dot(q_ref[h], kc) * ATTN_SCALE - sl_ref[h] * dist, NEG)
        p = jnp.where(ok, jnp.exp(s - jnp.max(s, axis=1, keepdims=True)), 0.0)
        p = p / jnp.maximum(jnp.sum(p, axis=1, keepdims=True), 1e-30)
        oc_ref[h] = jnp.dot(p.astype(BF16), vc, preferred_element_type=F32)

        psum = jnp.broadcast_to(jnp.sum(p, axis=0, keepdims=True), (gqa, nch))
        imp = jnp.dot(psum, cover_ref[...], precision=lax.Precision.HIGHEST, preferred_element_type=F32)
        score = jnp.where(valid, imp + jnp.where(forced, FORCE_BONUS, 0.0), NEG)
        score = jnp.where(j_io < n_sel_blocks, score, -jnp.inf)
        idx = jnp.zeros((gqa, V7X_LANES), F32)
        okv = jnp.zeros((gqa, V7X_LANES), jnp.int32)
        for t in range(min(N_SEL, n_sel_blocks)):
            mx = jnp.max(score, axis=1, keepdims=True)
            am = jnp.min(jnp.where(score == mx, j_f, float(nj)), axis=1, keepdims=True)
            idx = jnp.where(lane == t, am, idx)
            okv = jnp.where(lane == t, jnp.where(mx > 0.5 * NEG, 1, 0), okv)
            score = jnp.where(j_f == am, -jnp.inf, score)
        idx_ref[h] = idx.astype(jnp.int32)
        ok_ref[h] = okv


def _nsa_sample_cmp(q, slopes_col, sums, past):
    bh, gqa, hd = q.shape
    fk = sums[0]
    bd, nch, n_kv, _ = fk.shape
    n_sel_blocks = past // L_SEL + 1
    nj = -(-n_sel_blocks // V7X_LANES) * V7X_LANES
    n_io = np.arange(nch)[:, None]
    j_io = np.arange(nj)[None, :]
    cover = ((n_io * D_CMP < j_io * L_SEL + L_SEL) & (n_io * D_CMP + L_CMP > j_io * L_SEL)
             & (n_io < nch - 1) & (j_io < n_sel_blocks)).astype(np.float32)
    hspec = pl.BlockSpec((n_kv, gqa, hd), lambda b: (b, 0, 0))
    sspec = pl.BlockSpec((None, nch, n_kv, hd), lambda b: (b, 0, 0, 0))
    ispec = pl.BlockSpec((n_kv, gqa, V7X_LANES), lambda b: (b, 0, 0))
    return pl.pallas_call(
        functools.partial(_nsa_sample_cmp_kernel, past=past, n_sel_blocks=n_sel_blocks),
        grid=(bd,),
        in_specs=[hspec, pl.BlockSpec((n_kv, gqa, 1), lambda b: (0, 0, 0)), sspec, sspec, sspec, sspec,
                  pl.BlockSpec((nch, nj), lambda b: (0, 0))],
        out_specs=[hspec, ispec, ispec],
        out_shape=[jax.ShapeDtypeStruct((bh, gqa, hd), F32),
                   jax.ShapeDtypeStruct((bh, gqa, V7X_LANES), jnp.int32),
                   jax.ShapeDtypeStruct((bh, gqa, V7X_LANES), jnp.int32)],
        compiler_params=_params(("arbitrary",), 48 << 20),
        name="nsa_sample_cmp",
    )(q, slopes_col, *sums, jnp.asarray(cover))


def _nsa_sample_sel_kernel(pt_ref, idx_ref, okb_ref, *refs, past, nsb):
    k_blks = refs[:nsb]
    v_blks = refs[nsb:2 * nsb]
    (q_ref, sl_ref, gt_ref, oc_ref, ksn_ref, vsn_ref, kwc_ref, vwc_ref, kwn_ref, vwn_ref, o_ref) = refs[2 * nsb:]
    b = pl.program_id(0)
    h = pl.program_id(1)
    gqa, hd = q_ref.shape
    base = (b * N_KV + h) * nsb
    nb_past = past // L_SEL
    q_pos = past
    q = q_ref[...]
    slope = sl_ref[...]
    head = lambda ref: ref[:, pl.ds(h, 1), :].reshape(ref.shape[0], hd)

    row_io = lax.broadcasted_iota(jnp.int32, (L_SEL, hd), 0)
    new_k = jnp.where(row_io == 0, jnp.broadcast_to(ksn_ref[...], (L_SEL, hd)), 0.0)
    new_v = jnp.where(row_io == 0, jnp.broadcast_to(vsn_ref[...], (L_SEL, hd)), 0.0)
    lane = lax.broadcasted_iota(jnp.int32, (gqa, nsb * L_SEL), 1)
    kpos = lane & (L_SEL - 1)
    okl = jnp.zeros((gqa, nsb * L_SEL), jnp.int32)
    kg, vg = [], []
    for t in range(nsb):
        bid = idx_ref[base + t]
        is_new = bid >= nb_past
        kg.append(jnp.where(is_new, new_k, head(k_blks[t])).astype(BF16))
        vg.append(jnp.where(is_new, new_v, head(v_blks[t])).astype(BF16))
        in_t = (lane >> SEL_SHIFT) == t
        kpos = kpos + jnp.where(in_t, bid * L_SEL, 0)
        okl = okl + jnp.where(in_t, okb_ref[base + t], 0)
    kg = jnp.concatenate(kg, axis=0)
    vg = jnp.concatenate(vg, axis=0)
    dist = (q_pos - kpos).astype(F32)
    ok = (dist >= 0.0) & (okl > 0)
    s = jnp.where(ok, _nt_dot(q, kg) * ATTN_SCALE - slope * dist, NEG)
    p = jnp.where(ok, jnp.exp(s - jnp.max(s, axis=1, keepdims=True)), 0.0)
    p = p / jnp.maximum(jnp.sum(p, axis=1, keepdims=True), 1e-30)
    o_s = jnp.dot(p.astype(BF16), vg, preferred_element_type=F32)

    wbuf = kwc_ref.shape[0]
    i_io = lax.broadcasted_iota(jnp.int32, (gqa, wbuf), 1)
    kpos_w = past - wbuf + i_io
    dist_w = (q_pos - kpos_w).astype(F32)
    ok_w = (dist_w >= 0.0) & (dist_w <= float(WINDOW)) & (kpos_w >= 0)
    s_w = jnp.where(ok_w, _nt_dot(q, head(kwc_ref).astype(BF16)) * ATTN_SCALE - slope * dist_w, NEG)
    kn = jnp.broadcast_to(kwn_ref[...], (V7X_SUBLANES, hd)).astype(BF16)
    s_n = (_nt_dot(q, kn) * ATTN_SCALE)[:, 0:1]
    m = jnp.maximum(jnp.max(s_w, axis=1, keepdims=True), s_n)
    p_w = jnp.where(ok_w, jnp.exp(s_w - m), 0.0)
    p_n = jnp.exp(s_n - m)
    den = jnp.maximum(jnp.sum(p_w, axis=1, keepdims=True) + p_n, 1e-30)
    p_w = p_w / den
    p_n = p_n / den
    o_w = (jnp.dot(p_w.astype(BF16), head(vwc_ref).astype(BF16), preferred_element_type=F32)
           + p_n.astype(BF16).astype(F32) * vwn_ref[...].astype(BF16).astype(F32))

    gt = gt_ref[...]
    o_ref[...] = (gt[:, 0:1] * oc_ref[...] + gt[:, 1:2] * o_s + gt[:, 2:3] * o_w).astype(BF16)


def _nsa_sample_sel(page_table, idx_flat, ok_flat, pool_k, pool_v, q, slopes_col, gates, o_c,
                    ks_new, vs_new, cache_kw, cache_vw, kw_new, vw_new, past):
    bh, gqa, hd = q.shape
    bd = bh // N_KV
    nsb = idx_flat.shape[0] // bh
    c = N_KV * hd
    halves = PAGE_SIZE // L_SEL
    nb_past = past // L_SEL
    wbuf = cache_kw.shape[1]

    def blk_spec(t):
        def imap(b, h, pt, idx, okb):
            bid = jnp.minimum(idx[(b * N_KV + h) * nsb + t], nb_past - 1)
            return (pt[b, bid // halves], bid % halves, 0, 0)
        return pl.BlockSpec((None, L_SEL, N_KV, hd), imap)
    hspec = pl.BlockSpec((None, gqa, hd), lambda b, h, *_: (b * N_KV + h, 0, 0))
    nspec = pl.BlockSpec((None, 1, hd), lambda b, h, *_: (b, 0, h))
    wspec = pl.BlockSpec((None, wbuf, N_KV, hd), lambda b, h, *_: (b, 0, 0, 0))
    return pl.pallas_call(
        functools.partial(_nsa_sample_sel_kernel, past=past, nsb=nsb),
        grid_spec=pltpu.PrefetchScalarGridSpec(
            num_scalar_prefetch=3, grid=(bd, N_KV),
            in_specs=[blk_spec(t) for t in range(nsb)] * 2
            + [hspec, pl.BlockSpec((None, gqa, 1), lambda b, h, *_: (h, 0, 0)),
               pl.BlockSpec((None, gqa, 3), lambda b, h, *_: (b * N_KV + h, 0, 0)), hspec,
               nspec, nspec, wspec, wspec, nspec, nspec],
            out_specs=hspec),
        out_shape=jax.ShapeDtypeStruct((bh, gqa, hd), BF16),
        compiler_params=_params(("arbitrary", "arbitrary"), 40 << 20),
        name="nsa_sample_sel",
    )(page_table, idx_flat, ok_flat, *([pool_k] * nsb), *([pool_v] * nsb), q, slopes_col, gates, o_c,
      ks_new.reshape(bd, 1, c), vs_new.reshape(bd, 1, c), cache_kw, cache_vw,
      kw_new.reshape(bd, 1, c), vw_new.reshape(bd, 1, c))


TM_MM = 1024
TM_ROWS = 256
TQ = 256


def kernel(x_prompt, x_sample, c_prompt, c_sample, page_table, cache_k_cmp, cache_v_cmp, cache_k_sel, cache_v_sel, cache_k_win, cache_v_win, state_h, state_conv, w_ada, b_ada, ln_g, ln_b, a_w_in, a_conv_w, a_conv_b, a_w_r, a_b_r, a_w_i, a_b_i, a_lambda, a_w_out, w_kv, w_cmp_k, w_cmp_v, b_w_qg, b_w_o, f_w_up, f_w_down):
    nb, s_len, d = x_prompt.shape
    bd, sd, _ = x_sample.shape
    assert sd == 1
    depth = w_ada.shape[0]
    n_a = a_w_in.shape[0]
    assert depth - n_a == 1 or depth == n_a, "one shared-KV NSA layer stack"
    d_ff = f_w_down.shape[1]
    lru = a_w_out.shape[1]
    n_heads = d // HEAD_DIM
    gqa = n_heads // N_KV
    ckv = N_KV * HEAD_DIM
    past = page_table.shape[1] * PAGE_SIZE
    alpha = (2.0 * depth) ** 0.25
    t = nb * s_len
    tm_mm = min(TM_MM, s_len)
    tm_rows = min(TM_ROWS, s_len)
    tn = 512

    rows = -(-(nb + bd) // V7X_SUBLANES) * V7X_SUBLANES
    c_all = jnp.concatenate([c_prompt, c_sample, jnp.zeros((rows - nb - bd, d), F32)], axis=0)
    mods = _ada(c_all, w_ada, b_ada)

    def mod_p(l, k):
        return mods[l, :nb, k * d:(k + 1) * d].reshape(nb, 1, d)

    def mod_s(l, k):
        return mods[l, nb:nb + bd, k * d:(k + 1) * d].reshape(1, bd, d)

    slopes = jnp.exp2(-8.0 * jnp.arange(1, n_heads + 1, dtype=F32) / n_heads)
    slopes_col = slopes.reshape(N_KV, gqa, 1)

    xp = x_prompt.reshape(t, d)
    xs = x_sample.reshape(bd, d)
    up = _modulate(xp, mod_p(0, 0), mod_p(0, 1), tm_rows)
    us = _modulate(xs, mod_s(0, 0), mod_s(0, 1), bd)

    w_down_bf = f_w_down.astype(BF16)

    def ffn(x_p, x_s, u_p, u_s, l, nxt):
        hmid = _matmul(u_p, u_s, f_w_up, d_ff, layer=l, tm=min(2 * tm_mm, s_len), tn=256, out_dtype=BF16,
                       epilogue=_swiglu_epilogue, col_off=0, col_off2=d_ff, name="ffn_up")
        ys = _matmul(hmid[0], hmid[1], w_down_bf, d, layer=l, tm=min(tm_mm, 512), tn=tn, out_dtype=F32,
                     name="ffn_down")
        outs = []
        for x, y, tm_r, mod in ((x_p, ys[0], tm_rows, mod_p), (x_s, ys[1], bd, mod_s)):
            if nxt is None:
                outs.append(_ln_mod(x, y, mod(l, 5), ln_g[l, 1], ln_b[l, 1], alpha, tm_r))
            else:
                outs.append(_ln_mod(x, y, mod(l, 5), ln_g[l, 1], ln_b[l, 1], alpha, tm_r,
                                    shift=mod(nxt[0], 0), scale=mod(nxt[0], 1), want_xb=nxt[1]))
        return outs

    h_p, h_s, cb_p, cb_s = [], [], [], []
    kv_p = kv_s = None
    for l in range(depth):
        if l < n_a:
            wr = a_w_r[l].astype(BF16)
            wi = a_w_i[l].astype(BF16)
            ggx_p, ggx_s = _matmul(up, us, a_w_in, lru, layer=l, tm=tm_mm, tn=tn, out_dtype=BF16,
                                   epilogue=_gelu_tanh, name="lru_in_g")
            rx_p, rx_s = _matmul(up, us, a_w_in, lru, layer=l, tm=tm_mm, tn=tn, out_dtype=F32, col_off=lru,
                                 name="lru_in_r")
            yin_p, hl_p, ct_p = _rglru_prompt(
                rx_p, ggx_p, jnp.zeros((nb, CONV_W - 1, lru), F32), jnp.zeros((nb, lru), F32),
                a_conv_w[l], a_conv_b[l], wr, a_b_r[l], wi, a_b_i[l], a_lambda[l], nb, min(TM_ROWS, s_len))
            yin_s, hl_s, ct_s = _rglru_step(rx_s, ggx_s, jnp.swapaxes(state_conv[l], 0, 1), state_h[l],
                                            a_conv_w[l], a_conv_b[l], wr, a_b_r[l], wi, a_b_i[l], a_lambda[l])
            h_p.append(hl_p.reshape(nb, lru))
            h_s.append(hl_s)
            cb_p.append(ct_p)
            cb_s.append(jnp.swapaxes(ct_s, 0, 1))
            y_p, y_s = _matmul(yin_p, yin_s, a_w_out, d, layer=l, tm=tm_mm, tn=tn, out_dtype=F32, name="lru_out")
        else:
            j = l - n_a
            w_g = b_w_qg[j][:, n_heads * HEAD_DIM:].reshape(d, N_KV, 3 * gqa)
            w_g = jnp.pad(w_g, ((0, 0), (0, 0), (0, V7X_LANES - 3 * gqa))).reshape(d, N_KV * V7X_LANES)
            q_p, q_s = _matmul(up, us, b_w_qg, d, layer=j, tm=tm_mm, tn=tn, out_dtype=BF16, name="nsa_q")
            g_p, g_s = _matmul(up, us, w_g, N_KV * V7X_LANES, tm=tm_mm, tn=N_KV * V7X_LANES, out_dtype=F32,
                               epilogue=jax.nn.sigmoid, name="nsa_gate")
            o_p = _nsa_prompt(q_p, g_p, ctx_p[0], ctx_p[1], kv_p[2], kv_p[3], kv_p[4], kv_p[5], slopes, nb,
                              min(TQ, s_len))
            q_s = q_s.reshape(bd * N_KV, gqa, HEAD_DIM)
            g_s = g_s.reshape(bd * N_KV, V7X_LANES)[:, :3 * gqa].reshape(bd * N_KV, gqa, 3)
            o_c, idx, okb = _nsa_sample_cmp(q_s, slopes_col, ctx_s, past)
            nsb = min(N_SEL, past // L_SEL + 1)
            o_s = _nsa_sample_sel(page_table, idx[:, 0, :nsb].reshape(-1), okb[:, 0, :nsb].reshape(-1),
                                  cache_k_sel, cache_v_sel, q_s, slopes_col, g_s, o_c,
                                  kv_s[2], kv_s[3], cache_k_win, cache_v_win, kv_s[4], kv_s[5], past)
            y_p, y_s = _matmul(o_p, o_s.reshape(bd, d), b_w_o, d, layer=j, tm=tm_mm, tn=tn, out_dtype=F32,
                               name="nsa_out")

        xp, up = _ln_mod(xp, y_p, mod_p(l, 2), ln_g[l, 0], ln_b[l, 0], alpha, tm_rows,
                         shift=mod_p(l, 3), scale=mod_p(l, 4))
        xs, us = _ln_mod(xs, y_s, mod_s(l, 2), ln_g[l, 0], ln_b[l, 0], alpha, bd,
                         shift=mod_s(l, 3), scale=mod_s(l, 4))
        last = l == depth - 1
        res_p, res_s = ffn(xp, xs, up, us, l, None if last else (l + 1, l == n_a - 1))
        if last:
            xp, xs = res_p[0], res_s[0]
        elif l == n_a - 1:
            xp, up, xbp = res_p
            xs, us, xbs = res_s
            kv = [_matmul(xbp, xbs, w_kv, ckv, tm=tm_mm, tn=ckv, out_dtype=F32, col_off=jj * ckv,
                          heads=(N_KV, HEAD_DIM), name="kv_proj") for jj in range(6)]
            kv_p, kv_s, kv_p4, kv_s4 = zip(*kv)
            ctx_p = _cmp_blocks_prompt(kv_p[0], kv_p[1], w_cmp_k, w_cmp_v, nb)
            ctx_s = _page_chunks(page_table, cache_k_cmp, cache_v_cmp, w_cmp_k, w_cmp_v)
        else:
            xp, up = res_p
            xs, us = res_s

    wwin = min(WINDOW, s_len)
    shp_p = (nb, s_len, N_KV, HEAD_DIM)
    shp_s = (bd, sd, N_KV, HEAD_DIM)
    kvp = [a.reshape(shp_p) for a in kv_p4]
    kvs = [a.reshape(shp_s) for a in kv_s4]
    wbuf = cache_k_win.shape[1]
    keep = wbuf - sd
    k_win_s = jnp.concatenate([cache_k_win[:, wbuf - keep:], kvs[4]], axis=1)
    v_win_s = jnp.concatenate([cache_v_win[:, wbuf - keep:], kvs[5]], axis=1)
    return (xp.reshape(nb, s_len, d), xs.reshape(bd, sd, d),
            kvp[0], kvs[0], kvp[1], kvs[1], kvp[2], kvs[2], kvp[3], kvs[3],
            kvp[4][:, -wwin:], k_win_s, kvp[5][:, -wwin:], v_win_s,
            jnp.stack(h_p), jnp.stack(h_s), jnp.stack(cb_p), jnp.stack(cb_s))
```

```python
import functools

import numpy as np
import jax
import jax.numpy as jnp
from jax import lax
from jax.experimental import pallas as pl
from jax.experimental.pallas import tpu as pltpu

F32 = jnp.float32
BF16 = jnp.bfloat16

HEAD_DIM = 128
N_KV = 4
L_CMP = 32
D_CMP = 16
L_SEL = 64
N_SEL = 16
WINDOW = 512
PAGE_SIZE = 128
CONV_W = 4
LRU_BLOCKS = 16
LRU_C = 8.0
NEG = -1e30
FORCE_BONUS = 1e3
ATTN_SCALE = HEAD_DIM ** -0.5
LN_EPS = 1e-5
EXP2_SCALE = ATTN_SCALE * 1.4426950408889634
ATTN_ROW_CHUNK = 32
SEL_SHIFT = L_SEL.bit_length() - 1
assert 1 << SEL_SHIFT == L_SEL

V7X_VMEM_BYTES = 64 * 1024 * 1024
V7X_LANES = 128
V7X_SUBLANES = 8
VMEM_CAP = V7X_VMEM_BYTES - 3 * 1024 * 1024


def _params(sem, vmem_bytes):
    return pltpu.CompilerParams(dimension_semantics=sem,
                                vmem_limit_bytes=int(min(max(vmem_bytes, 16 << 20), VMEM_CAP)))


def _nt_dot(a, b, **kw):
    return lax.dot_general(a, b, (((1,), (1,)), ((), ())), preferred_element_type=F32, **kw)


def _ada_kernel(c_ref, w_ref, b_ref, o_ref):
    c = c_ref[...]
    s = (c * jax.nn.sigmoid(c)).astype(BF16)
    o_ref[...] = jnp.dot(s, w_ref[...].astype(BF16), preferred_element_type=F32) + b_ref[...]


def _ada(c_all, w_ada, b_ada, tn=1024):
    depth, d, n6 = w_ada.shape
    r = c_all.shape[0]
    return pl.pallas_call(
        _ada_kernel,
        grid=(depth, n6 // tn),
        in_specs=[pl.BlockSpec((r, d), lambda l, n: (0, 0)),
                  pl.BlockSpec((None, d, tn), lambda l, n: (l, 0, n)),
                  pl.BlockSpec((None, 1, tn), lambda l, n: (l, 0, n))],
        out_specs=pl.BlockSpec((None, r, tn), lambda l, n: (l, 0, n)),
        out_shape=jax.ShapeDtypeStruct((depth, r, n6), F32),
        compiler_params=_params(("arbitrary", "arbitrary"), 2 * d * tn * 4 + d * tn * 2 + (8 << 20)),
        name="ada",
    )(c_all, w_ada, b_ada.reshape(depth, 1, n6))


def _modulate_kernel(x_ref, sh_ref, sc_ref, u_ref):
    u_ref[...] = (x_ref[...] * (1.0 + sc_ref[...]) + sh_ref[...]).astype(u_ref.dtype)


def _row_specs(t, d, tm, nb):
    tiles_per_batch = (t // nb) // tm
    xspec = pl.BlockSpec((tm, d), lambda i: (i, 0))

    def vspec(r):
        return pl.BlockSpec((None, r, d), lambda i: (i // tiles_per_batch, 0, 0))
    return xspec, vspec


def _modulate(x, shift, scale, tm):
    t, d = x.shape
    nb, r, _ = shift.shape
    xspec, vspec = _row_specs(t, d, tm, nb)
    return pl.pallas_call(
        _modulate_kernel,
        grid=(t // tm,),
        in_specs=[xspec, vspec(r), vspec(r)],
        out_specs=xspec,
        out_shape=jax.ShapeDtypeStruct((t, d), BF16),
        compiler_params=_params(("arbitrary",), 6 * tm * d * 4),
        name="modulate",
    )(x, shift, scale)


def _ln_mod_kernel(*refs, alpha, want_u, want_xb):
    x_ref, y_ref, gate_ref, g_ref, b_ref = refs[:5]
    pos = 5
    if want_u:
        sh_ref, sc_ref = refs[pos:pos + 2]
        pos += 2
    xo_ref = refs[pos]
    pos += 1
    v = alpha * x_ref[...] + (1.0 + gate_ref[...]) * y_ref[...]
    mu = jnp.mean(v, axis=-1, keepdims=True)
    dlt = v - mu
    var = jnp.mean(dlt * dlt, axis=-1, keepdims=True)
    xn = dlt * lax.rsqrt(var + LN_EPS) * g_ref[...] + b_ref[...]
    xo_ref[...] = xn
    if want_u:
        refs[pos][...] = (xn * (1.0 + sc_ref[...]) + sh_ref[...]).astype(BF16)
        pos += 1
    if want_xb:
        refs[pos][...] = xn.astype(BF16)


def _ln_mod(x, y, gate, ln_g, ln_b, alpha, tm, shift=None, scale=None, want_xb=False):
    t, d = x.shape
    nb, r, _ = gate.shape
    xspec, vspec = _row_specs(t, d, tm, nb)
    pspec = pl.BlockSpec((1, d), lambda i: (0, 0))
    want_u = shift is not None
    ins = [x, y, gate, ln_g.reshape(1, d), ln_b.reshape(1, d)]
    in_specs = [xspec, xspec, vspec(r), pspec, pspec]
    out_shape = [jax.ShapeDtypeStruct((t, d), F32)]
    out_specs = [xspec]
    if want_u:
        ins += [shift, scale]
        in_specs += [vspec(r), vspec(r)]
        out_shape.append(jax.ShapeDtypeStruct((t, d), BF16))
        out_specs.append(xspec)
    if want_xb:
        out_shape.append(jax.ShapeDtypeStruct((t, d), BF16))
        out_specs.append(xspec)
    return pl.pallas_call(
        functools.partial(_ln_mod_kernel, alpha=alpha, want_u=want_u, want_xb=want_xb),
        grid=(t // tm,),
        in_specs=in_specs, out_specs=out_specs, out_shape=out_shape,
        compiler_params=_params(("arbitrary",), 12 * tm * d * 4),
        name="ln_mod",
    )(*ins)


def _gelu_tanh(x):
    return jax.nn.gelu(x, approximate=True)


def _mm_kernel(*refs, n_w, cast_w, epilogue, heads):
    x_ref, xs_ref = refs[:2]
    w_refs = refs[2:2 + n_w]
    n_out = 4 if heads else 2
    outs = refs[2 + n_w:2 + n_w + n_out]
    scr = refs[2 + n_w + n_out:]
    first_row_tile = pl.program_id(1) == 0
    if cast_w:
        @pl.when(first_row_tile)
        def _():
            for w_ref, s_ref in zip(w_refs, scr):
                s_ref[...] = w_ref[...].astype(BF16)
        w_refs = scr

    def product(rows_ref, out_ref, heads_ref):
        rows = rows_ref[...]
        accs = [jnp.dot(rows, w_ref[...], preferred_element_type=F32) for w_ref in w_refs]
        val = epilogue(*accs).astype(out_ref.dtype)
        out_ref[...] = val
        if heads_ref is not None:
            hd = heads_ref.shape[-1]
            for h in range(heads_ref.shape[-2]):
                heads_ref[:, h, :] = val[:, h * hd:(h + 1) * hd]

    product(x_ref, outs[0], outs[2] if heads else None)
    pl.when(first_row_tile)(lambda: product(xs_ref, outs[1], outs[3] if heads else None))


def _matmul(x, xs, w, n_out, *, tm, tn, out_dtype, layer=None, epilogue=None, col_off=0, col_off2=None,
            heads=None, name="mm"):
    m, k = x.shape
    ms = xs.shape[0]
    assert m % tm == 0 and n_out % tn == 0 and col_off % tn == 0
    assert (w.ndim == 3) == (layer is not None) and w.shape[-2] == k and xs.shape[1] == k
    offs = [col_off // tn] + ([] if col_off2 is None else [col_off2 // tn])
    n_w = len(offs)
    if epilogue is None:
        epilogue = lambda a: a
    cast_w = w.dtype != BF16
    scratch = [pltpu.VMEM((k, tn), BF16) for _ in range(n_w)] if cast_w else []
    w_bytes = k * tn * w.dtype.itemsize
    vmem = (2 * tm * k * 2 + n_w * (2 * w_bytes + len(scratch) * k * tn * 2)
            + 2 * tm * tn * 4 * (n_w + 1) + (4 << 20))
    if layer is None:
        w_specs = [pl.BlockSpec((k, tn), functools.partial(lambda n, i, o: (0, n + o), o=o)) for o in offs]
    else:
        w_specs = [pl.BlockSpec((None, k, tn), functools.partial(lambda n, i, o: (layer, 0, n + o), o=o))
                   for o in offs]
    out_specs = [pl.BlockSpec((tm, tn), lambda n, i: (i, n)), pl.BlockSpec((ms, tn), lambda n, i: (0, n))]
    out_shape = [jax.ShapeDtypeStruct((m, n_out), out_dtype), jax.ShapeDtypeStruct((ms, n_out), out_dtype)]
    if heads is not None:
        assert n_out == tn == heads[0] * heads[1]
        out_specs += [pl.BlockSpec((tm,) + heads, lambda n, i: (i, 0, 0)),
                      pl.BlockSpec((ms,) + heads, lambda n, i: (0, 0, 0))]
        out_shape += [jax.ShapeDtypeStruct((m,) + heads, out_dtype), jax.ShapeDtypeStruct((ms,) + heads, out_dtype)]
        vmem += 4 * tm * tn * 4
    return pl.pallas_call(
        functools.partial(_mm_kernel, n_w=n_w, cast_w=cast_w, epilogue=epilogue, heads=heads is not None),
        grid=(n_out // tn, m // tm),
        in_specs=[pl.BlockSpec((tm, k), lambda n, i: (i, 0)), pl.BlockSpec((ms, k), lambda n, i: (0, 0))]
        + w_specs,
        out_specs=out_specs, out_shape=out_shape,
        scratch_shapes=scratch,
        compiler_params=_params(("arbitrary", "arbitrary"), vmem),
        name=name,
    )(x, xs, *([w] * n_w))


def _swiglu_epilogue(g, v):
    return g * jax.nn.sigmoid(g) * v


def _softplus_neg(lam):
    return jnp.maximum(-lam, 0.0) + jnp.log1p(jnp.exp(-jnp.abs(lam)))


def _lru_gates(xc, wr_ref, br_ref, wi_ref, bi_ref, sp, a_ref, b_ref, nblk, bw):
    def put(ref, k, val):
        ref[:, k * bw:(k + 1) * bw] = val

    nsp = (-LRU_C) * sp
    xcb = xc.astype(BF16)
    for k in range(nblk):
        cs = slice(k * bw, (k + 1) * bw)
        xb = xcb[:, cs]
        r = jax.nn.sigmoid(jnp.dot(xb, wr_ref[k], preferred_element_type=F32) + br_ref[:, cs])
        ig = jax.nn.sigmoid(jnp.dot(xb, wi_ref[k], preferred_element_type=F32) + bi_ref[:, cs])
        log_a = r * nsp[:, cs]
        put(a_ref, k, jnp.exp(log_a))
        th = jnp.tanh(log_a)
        put(b_ref, k, jnp.sqrt(-2.0 * th / (1.0 - th)) * (ig * xc[:, cs]))


SCAN_TILES = 8


def _rglru_kernel(rx_ref, ggx_ref, cb0_ref, h0_ref, cw_ref, cbias_ref, wr_ref, br_ref, wi_ref, bi_ref,
                  lam_ref, y_ref, hlast_ref, ctail_ref, xpad_ref, a_ref, b_ref, hcar_ref, *, tm, nblk, bw):
    s = pl.program_id(1)
    width = nblk * bw
    pad = V7X_SUBLANES
    tail = CONV_W - 1

    @pl.when(s == 0)
    def _():
        xpad_ref[pad - tail:pad, :] = cb0_ref[...]
        hcar_ref[...] = h0_ref[...]

    rx = rx_ref[...]
    xpad_ref[pad:pad + tm, :] = rx
    cw = cw_ref[...]
    xc = cbias_ref[...] + rx * cw[tail:tail + 1]
    for j in range(tail):
        xc = xc + xpad_ref[pad - tail + j:pad - tail + j + tm, :] * cw[j:j + 1]
    xpad_ref[pad - tail:pad, :] = rx[tm - tail:tm, :]

    _lru_gates(xc, wr_ref, br_ref, wi_ref, bi_ref, _softplus_neg(lam_ref[...]), a_ref, b_ref, nblk, bw)

    scan_lanes = min(SCAN_TILES * V7X_LANES, width)
    row = lax.broadcasted_iota(jnp.int32, (V7X_SUBLANES, scan_lanes), 0)
    for c in range(width // scan_lanes):
        cs = slice(c * scan_lanes, (c + 1) * scan_lanes)

        def group(gi, h, cs=cs):
            r0 = pl.multiple_of(gi * V7X_SUBLANES, V7X_SUBLANES)
            a = a_ref[pl.ds(r0, V7X_SUBLANES), cs]
            b = b_ref[pl.ds(r0, V7X_SUBLANES), cs]
            for sft in (1, 2, 4):
                keep = row >= sft
                a_s = jnp.where(keep, pltpu.roll(a, sft, 0), 1.0)
                b_s = jnp.where(keep, pltpu.roll(b, sft, 0), 0.0)
                b = a * b_s + b
                a = a * a_s
            hs = a * h + b
            b_ref[pl.ds(r0, V7X_SUBLANES), cs] = hs
            return jnp.broadcast_to(hs[V7X_SUBLANES - 1:V7X_SUBLANES, :], hs.shape)

        h_in = jnp.broadcast_to(hcar_ref[:, cs], (V7X_SUBLANES, scan_lanes))
        h_out = lax.fori_loop(0, tm // V7X_SUBLANES, group, h_in)
        hcar_ref[:, cs] = h_out[0:1, :]

    y_ref[...] = (ggx_ref[...].astype(F32) * b_ref[...]).astype(BF16)

    @pl.when(s == pl.num_programs(1) - 1)
    def _():
        hlast_ref[...] = hcar_ref[...]
        ctail_ref[...] = rx[tm - tail:tm, :]


def _rglru_prompt(rx, ggx, conv_buf, h0, cw, cbias, wr, br, wi, bi, lam, nb, tm):
    t, width = rx.shape
    s_len = t // nb
    nblk, bw, _ = wr.shape
    tail = CONV_W - 1
    ns = s_len // tm
    row = lambda a: a.reshape(1, width)
    xspec = pl.BlockSpec((tm, width), lambda b, s: (b * ns + s, 0))
    pspec = pl.BlockSpec((1, width), lambda b, s: (0, 0))
    wspec = pl.BlockSpec((nblk, bw, bw), lambda b, s: (0, 0, 0))
    vmem = (2 * tm * width * (4 + 2 + 2) + 3 * (tm + 8) * width * 4 + 4 * nblk * bw * bw * 2 + (6 << 20))
    return pl.pallas_call(
        functools.partial(_rglru_kernel, tm=tm, nblk=nblk, bw=bw),
        grid=(nb, ns),
        in_specs=[xspec, xspec,
                  pl.BlockSpec((None, tail, width), lambda b, s: (b, 0, 0)),
                  pl.BlockSpec((None, 1, width), lambda b, s: (b, 0, 0)),
                  pl.BlockSpec((CONV_W, width), lambda b, s: (0, 0)), pspec,
                  wspec, pspec, wspec, pspec, pspec],
        out_specs=[xspec,
                   pl.BlockSpec((None, 1, width), lambda b, s: (b, 0, 0)),
                   pl.BlockSpec((None, tail, width), lambda b, s: (b, 0, 0))],
        out_shape=[jax.ShapeDtypeStruct((t, width), BF16),
                   jax.ShapeDtypeStruct((nb, 1, width), F32),
                   jax.ShapeDtypeStruct((nb, tail, width), F32)],
        scratch_shapes=[pltpu.VMEM((tm + V7X_SUBLANES, width), F32), pltpu.VMEM((tm, width), F32),
                        pltpu.VMEM((tm, width), F32), pltpu.VMEM((1, width), F32)],
        compiler_params=_params(("arbitrary", "arbitrary"), vmem),
        name="rglru",
    )(rx, ggx, conv_buf, h0.reshape(nb, 1, width), cw, row(cbias), wr, row(br), wi, row(bi), row(lam))


def _rglru_step_kernel(rx_ref, ggx_ref, cb_ref, h0_ref, cw_ref, cbias_ref, wr_ref, br_ref, wi_ref, bi_ref,
                       lam_ref, y_ref, h_ref, cnew_ref, a_ref, b_ref, *, nblk, bw):
    tail = CONV_W - 1
    rx = rx_ref[...]
    cw = cw_ref[...]
    xc = cbias_ref[...] + rx * cw[tail:tail + 1]
    for j in range(tail):
        xc = xc + cb_ref[j] * cw[j:j + 1]
    _lru_gates(xc, wr_ref, br_ref, wi_ref, bi_ref, _softplus_neg(lam_ref[...]), a_ref, b_ref, nblk, bw)
    h = a_ref[...] * h0_ref[...] + b_ref[...]
    h_ref[...] = h
    y_ref[...] = (ggx_ref[...].astype(F32) * h).astype(BF16)
    for j in range(tail - 1):
        cnew_ref[j] = cb_ref[j + 1]
    cnew_ref[tail - 1] = rx


def _rglru_step(rx, ggx, conv_t, h0, cw, cbias, wr, br, wi, bi, lam):
    bd, width = rx.shape
    nblk, bw, _ = wr.shape
    row = lambda a: a.reshape(1, width)
    return pl.pallas_call(
        functools.partial(_rglru_step_kernel, nblk=nblk, bw=bw),
        out_shape=[jax.ShapeDtypeStruct((bd, width), BF16), jax.ShapeDtypeStruct((bd, width), F32),
                   jax.ShapeDtypeStruct(conv_t.shape, F32)],
        scratch_shapes=[pltpu.VMEM((bd, width), F32), pltpu.VMEM((bd, width), F32)],
        compiler_params=pltpu.CompilerParams(vmem_limit_bytes=32 << 20),
        name="rglru_step",
    )(rx, ggx, conv_t, h0, cw, row(cbias), wr, row(br), wi, row(bi), row(lam))


def _chunk_sums(x, w):
    n = x.shape[0] // D_CMP
    x3 = x.reshape(n, D_CMP, x.shape[1])
    first = jnp.sum(x3 * w[None, :D_CMP, :], axis=1)
    second = jnp.sum(x3 * w[None, D_CMP:, :], axis=1)
    return first, second


def _blocks_from_sums(first, second):
    n = first.shape[0]
    nxt = pltpu.roll(second, n - 1, 0)
    rows = lax.broadcasted_iota(jnp.int32, first.shape, 0)
    return jnp.where(rows < n - 1, first + nxt, 0.0)


COEF_PIECES = 3
POS_LOW_BITS = 7
POS_OFF = 64
MASK_BIG = 2.0 ** 100
LOG2E = 1.4426950408889634


def _pos_lanes(pos, block_of_key=None):
    lane = lax.broadcasted_iota(jnp.int32, pos.shape, 1)
    hi = ((pos >> POS_LOW_BITS) << POS_LOW_BITS).astype(F32)
    lo = (pos & ((1 << POS_LOW_BITS) - 1)).astype(F32)
    k = lane - POS_OFF
    out = jnp.where((k >= 0) & (k < COEF_PIECES), hi, jnp.where((k >= COEF_PIECES) & (k < 2 * COEF_PIECES), lo, 0.0))
    if block_of_key is not None:
        out = jnp.where((lane < POS_OFF) & (lane == block_of_key), 1.0, out)
    return out.astype(BF16)


def _coef_lanes(coef, shape):
    v = jnp.full(shape, coef, F32)
    c1 = v.astype(BF16).astype(F32)
    c2 = (v - c1).astype(BF16).astype(F32)
    c3 = (v - c1 - c2).astype(BF16).astype(F32)
    k = lax.broadcasted_iota(jnp.int32, shape, 1) - POS_OFF
    live = (k >= 0) & (k < 2 * COEF_PIECES)
    k = jnp.where(k >= COEF_PIECES, k - COEF_PIECES, k)
    return jnp.where(live, jnp.where(k == 0, c1, jnp.where(k == 1, c2, c3)), 0.0)


def _cmp_blocks_kernel(k_ref, v_ref, wk_ref, wv_ref, kc_ref, vc_ref):
    hd = k_ref.shape[1]
    nch = kc_ref.shape[0]
    kc_ref[:, :hd] = _blocks_from_sums(*_chunk_sums(k_ref[...], wk_ref[...])).astype(BF16)
    ends = (lax.broadcasted_iota(jnp.int32, (nch, V7X_LANES), 0) + 2) * D_CMP - 1
    kc_ref[:, hd:] = _pos_lanes(ends)
    vc_ref[...] = _blocks_from_sums(*_chunk_sums(v_ref[...], wv_ref[...])).astype(BF16)


def _cmp_blocks_prompt(k_c, v_c, wk, wv, nb):
    t, c = k_c.shape
    s_len = t // nb
    nch = s_len // D_CMP
    kvspec = pl.BlockSpec((s_len, HEAD_DIM), lambda b, h: (b, h))
    wspec = pl.BlockSpec((L_CMP, HEAD_DIM), lambda b, h: (0, h))

    def ospec(w):
        return pl.BlockSpec((None, None, nch, w), lambda b, h: (b, h, 0, 0))
    return pl.pallas_call(
        _cmp_blocks_kernel, grid=(nb, N_KV),
        in_specs=[kvspec, kvspec, wspec, wspec],
        out_specs=[ospec(HEAD_DIM + V7X_LANES), ospec(HEAD_DIM)],
        out_shape=[jax.ShapeDtypeStruct((nb, N_KV, nch, HEAD_DIM + V7X_LANES), BF16),
                   jax.ShapeDtypeStruct((nb, N_KV, nch, HEAD_DIM), BF16)],
        compiler_params=_params(("arbitrary", "arbitrary"), 24 << 20),
        name="cmp_blocks",
    )(k_c, v_c, wk.reshape(L_CMP, c), wv.reshape(L_CMP, c))


def _block_scores_topk_t(imp_t, q0, nsel, tq):
    nsp = -(-nsel // V7X_SUBLANES) * V7X_SUBLANES
    jt = lax.broadcasted_iota(jnp.int32, (nsp, tq), 0)
    qpos = q0 + lax.broadcasted_iota(jnp.int32, (nsp, tq), 1)
    cur = qpos >> SEL_SHIFT
    valid = (jt <= cur) & (jt < nsel)
    forced = (jt == 0) | (jt == cur) | (jt == cur - 1)
    score = jnp.where(valid, imp_t[:nsp] + jnp.where(forced, FORCE_BONUS, 0.0), NEG)
    rank = jnp.zeros((nsp, tq), F32)
    for i in range(nsel):
        ri = score[i:i + 1, :]
        beats = (ri > score) | ((ri == score) & (jt > i))
        rank = rank + jnp.where(beats, 1.0, 0.0)
    return jnp.where((rank < float(min(N_SEL, nsel))) & valid, 1.0, 0.0)


def _nsa_prompt_kernel(slopes_ref, q_ref, gt_ref, kc_ref, vc_ref, ks_ref, vs_ref, kw_ref, vw_ref,
                       o_ref, ksa, vsb, kwa, vwb, qa_ref, s_ref, p_ref, bdiag_ref, bwin_ref, bcmp_ref,
                       m_ref, acc_ref, oc_ref, os_ref, ow_ref, psum_ref, *, tq, gqa, s_len):
    h = pl.program_id(1)
    i = pl.program_id(2)
    q0 = i * tq
    hd = HEAD_DIM
    ncp = kc_ref.shape[0]
    n_cmp = ncp - 1
    nsel = s_len // L_SEL
    rows = gqa * tq
    rc = ATTN_ROW_CHUNK
    per_g = tq // rc
    r_io = lax.broadcasted_iota(jnp.int32, (tq, tq), 0)
    c_io = lax.broadcasted_iota(jnp.int32, (tq, tq), 1)

    @pl.when(i == 0)
    def _():
        kpos = lax.broadcasted_iota(jnp.int32, (s_len, V7X_LANES), 0)
        ksa[:, :hd] = ks_ref[...].astype(BF16)
        ksa[:, hd:] = _pos_lanes(kpos, kpos >> SEL_SHIFT)
        kwa[:, :hd] = kw_ref[...].astype(BF16)
        kwa[:, hd:] = _pos_lanes(kpos)
        bdiag_ref[...] = jnp.where(c_io <= r_io, 0.0, NEG)
        ones = jnp.ones((s_len, V7X_LANES), BF16)
        vsb[:, :hd] = vs_ref[...].astype(BF16)
        vsb[:, hd:] = ones
        vwb[:, :hd] = vw_ref[...].astype(BF16)
        vwb[:, hd:] = ones
        wshape = bwin_ref.shape[1:]
        rw = lax.broadcasted_iota(jnp.int32, wshape, 0)
        cw = lax.broadcasted_iota(jnp.int32, wshape, 1)
        for v in range(bwin_ref.shape[0]):
            dist = min(v * tq, WINDOW) + rw - cw
            bwin_ref[v] = jnp.where((dist >= 0) & (dist <= WINDOW), 0.0, NEG)

    coefs = [_coef_lanes(slopes_ref[h * gqa + g] * LOG2E, (V7X_SUBLANES, V7X_LANES))[0:1] for g in range(gqa)]
    for g in range(gqa):
        rs = slice(g * tq, (g + 1) * tq)
        qa_ref[rs, :hd] = q_ref[:, g * hd:(g + 1) * hd]
        qa_ref[rs, hd:] = jnp.broadcast_to(coefs[g], (tq, V7X_LANES)).astype(BF16)

    def head_rows(c, g):
        if isinstance(c, int):
            return slice(g * tq + c * rc, g * tq + (c + 1) * rc)
        return pl.ds(pl.multiple_of(g * tq + c * rc, rc), rc)

    def tile_rows(c):
        if isinstance(c, int):
            return slice(c * rc, (c + 1) * rc)
        return pl.ds(pl.multiple_of(c * rc, rc), rc)

    s_ref[:, :ncp] = _nt_dot(qa_ref[...], kc_ref[...])
    n_io = lax.broadcasted_iota(jnp.int32, (tq, ncp), 1)
    qpos_c = q0 + lax.broadcasted_iota(jnp.int32, (tq, ncp), 0)
    bcmp_ref[...] = jnp.where((qpos_c >= (n_io + 2) * D_CMP - 1) & (n_io < n_cmp), 0.0, NEG)

    def cmp_chunk(c, carry):
        br = tile_rows(c)
        bias = bcmp_ref[br, :]
        live = bias == 0.0
        tot = jnp.zeros((rc, ncp), F32)
        for g in range(gqa):
            rr = head_rows(c, g)
            s = s_ref[rr, :ncp] + bias
            p = jnp.where(live, jnp.exp2(s - jnp.max(s, axis=1, keepdims=True)), 0.0)
            p = p / jnp.maximum(jnp.sum(p, axis=1, keepdims=True), 1e-30)
            tot = tot + p
            p_ref[rr, :ncp] = p.astype(BF16)
        psum_ref[br, :] = tot
        return carry

    for c in range(per_g):
        cmp_chunk(c, 0)
    oc_ref[...] = jnp.dot(p_ref[:, :ncp], vc_ref[...], preferred_element_type=F32)
    psum = psum_ref[...]

    j_io = lax.broadcasted_iota(jnp.int32, (V7X_LANES, ncp), 0)
    n_io2 = lax.broadcasted_iota(jnp.int32, (V7X_LANES, ncp), 1)
    cover_t = jnp.where((n_io2 * D_CMP < j_io * L_SEL + L_SEL) & (n_io2 * D_CMP + L_CMP > j_io * L_SEL)
                        & (n_io2 < n_cmp) & (j_io < nsel), 1.0, 0.0)
    imp_t = _nt_dot(cover_t, psum, precision=lax.Precision.HIGHEST)
    sel_t = _block_scores_topk_t(imp_t, q0, nsel, tq)
    sel_t = jnp.concatenate([sel_t, jnp.zeros((V7X_LANES - sel_t.shape[0], tq), F32)], axis=0)
    lane_q = lax.broadcasted_iota(jnp.int32, (tq, V7X_LANES), 1)
    unpicked = (sel_t.T - 1.0) * MASK_BIG
    for g in range(gqa):
        side = jnp.where(lane_q < POS_OFF, unpicked, jnp.broadcast_to(coefs[g], (tq, V7X_LANES)))
        qa_ref[g * tq:(g + 1) * tq, hd:] = side.astype(BF16)

    def softmax_passes(width, bias_ref, first):
        lanes = [slice(j * V7X_LANES, (j + 1) * V7X_LANES) for j in range(width // V7X_LANES)]

        def scores(c, g):
            vals = [s_ref[head_rows(c, g), ls] for ls in lanes]
            if bias_ref is None:
                return vals
            return [v + bias_ref[tile_rows(c), ls] for v, ls in zip(vals, lanes)]

        for c in range(per_g):
            for g in range(gqa):
                rr = head_rows(c, g)
                m_new = jnp.max(functools.reduce(jnp.maximum, scores(c, g)), axis=1, keepdims=True)
                if first:
                    m_ref[rr, :] = jnp.broadcast_to(m_new, (rc, V7X_LANES))
                else:
                    m_prev = m_ref[rr, :]
                    m_new = jnp.maximum(m_prev, m_new)
                    alpha = jnp.exp2(m_prev - m_new)
                    m_ref[rr, :] = m_new
                    acc_ref[rr, :hd] = alpha * acc_ref[rr, :hd]
                    acc_ref[rr, hd:] = alpha * acc_ref[rr, hd:]
        for c in range(per_g):
            for g in range(gqa):
                rr = head_rows(c, g)
                m_new = m_ref[rr, :]
                for ls, v in zip(lanes, scores(c, g)):
                    p_ref[rr, ls] = jnp.exp2(v - m_new).astype(BF16)

    def sel_tile(kt, first, diagonal):
        k0 = kt * tq if isinstance(kt, int) else pl.multiple_of(kt * tq, tq)
        s_ref[:, :tq] = _nt_dot(qa_ref[...], ksa[pl.ds(k0, tq), :])
        softmax_passes(tq, bdiag_ref if diagonal else None, first)
        pv = jnp.dot(p_ref[:, :tq], vsb[pl.ds(k0, tq), :], preferred_element_type=F32)
        if first:
            acc_ref[...] = pv
        else:
            acc_ref[...] += pv

    @pl.when(i == 0)
    def _():
        sel_tile(0, True, True)

    @pl.when(i > 0)
    def _():
        sel_tile(0, True, False)

        def sel_mid(kt, carry):
            sel_tile(kt, False, False)
            return carry

        lax.fori_loop(1, i, sel_mid, 0)
        sel_tile(i, False, True)

    os_ref[...] = acc_ref[:, :hd] / jnp.maximum(acc_ref[:, hd:], 1e-30)

    wk = bwin_ref.shape[2]
    k0w = pl.multiple_of(jnp.maximum(q0 - WINDOW, 0), tq)
    s_ref[...] = _nt_dot(qa_ref[...], kwa[pl.ds(k0w, wk), :])
    softmax_passes(wk, bwin_ref.at[jnp.minimum(i, bwin_ref.shape[0] - 1)], True)
    pv = jnp.dot(p_ref[...], vwb[pl.ds(k0w, wk), :], preferred_element_type=F32)
    ow_ref[...] = pv[:, :hd] / jnp.maximum(pv[:, hd:], 1e-30)

    gt = gt_ref[...]
    for g in range(gqa):
        rs = slice(g * tq, (g + 1) * tq)
        o = (gt[:, 3 * g:3 * g + 1] * oc_ref[rs, :] + gt[:, 3 * g + 1:3 * g + 2] * os_ref[rs, :]
             + gt[:, 3 * g + 2:3 * g + 3] * ow_ref[rs, :])
        o_ref[:, g * hd:(g + 1) * hd] = o.astype(BF16)


def _nsa_prompt(q, gates, kc, vc, k_s, v_s, k_w, v_w, slopes, nb, tq):
    t, dq = q.shape
    s_len = t // nb
    gqa = dq // (N_KV * HEAD_DIM)
    nq = s_len // tq
    ncp = kc.shape[2]
    assert WINDOW % tq == 0 and s_len % tq == 0 and s_len // L_SEL <= POS_OFF and s_len >= WINDOW + tq
    assert ncp == V7X_LANES and tq % ATTN_ROW_CHUNK == 0 and s_len < (1 << (POS_LOW_BITS + 8))
    rows = gqa * tq
    hd = HEAD_DIM
    qspec = pl.BlockSpec((tq, gqa * hd), lambda b, h, i, sl: (b * nq + i, h))
    gspec = pl.BlockSpec((tq, V7X_LANES), lambda b, h, i, sl: (b * nq + i, h))

    def cspec(w):
        return pl.BlockSpec((None, None, ncp, w), lambda b, h, i, sl: (b, h, 0, 0))
    kvspec = pl.BlockSpec((s_len, hd), lambda b, h, i, sl: (b, h))
    ka_scr = pltpu.VMEM((s_len, hd + V7X_LANES), BF16)
    v_scr = pltpu.VMEM((s_len, hd + V7X_LANES), BF16)
    row_scr = pltpu.VMEM((rows, hd), F32)
    return pl.pallas_call(
        functools.partial(_nsa_prompt_kernel, tq=tq, gqa=gqa, s_len=s_len),
        grid_spec=pltpu.PrefetchScalarGridSpec(
            num_scalar_prefetch=1, grid=(nb, N_KV, nq),
            in_specs=[qspec, gspec, cspec(hd + V7X_LANES), cspec(hd), kvspec, kvspec, kvspec, kvspec],
            out_specs=qspec,
            scratch_shapes=[ka_scr, v_scr, ka_scr, v_scr,
                            pltpu.VMEM((rows, hd + V7X_LANES), BF16),
                            pltpu.VMEM((rows, WINDOW + tq), F32), pltpu.VMEM((rows, WINDOW + tq), BF16),
                            pltpu.VMEM((tq, tq), F32),
                            pltpu.VMEM((WINDOW // tq + 1, tq, WINDOW + tq), F32),
                            pltpu.VMEM((tq, ncp), F32),
                            pltpu.VMEM((rows, V7X_LANES), F32),
                            pltpu.VMEM((rows, hd + V7X_LANES), F32),
                            row_scr, row_scr, row_scr,
                            pltpu.VMEM((tq, ncp), F32)]),
        out_shape=jax.ShapeDtypeStruct((t, dq), BF16),
        compiler_params=_params(("arbitrary", "arbitrary", "arbitrary"), 48 << 20),
        name="nsa_prompt",
    )(slopes, q, gates, kc, vc, k_s, v_s, k_w, v_w)


def _page_chunks_kernel(pt_ref, *refs, pages):
    k_pages = refs[:pages]
    v_pages = refs[pages:2 * pages]
    wk_ref, wv_ref, fk_ref, sk_ref, fv_ref, sv_ref = refs[2 * pages:]
    per = PAGE_SIZE // D_CMP

    def sums(x, w):
        x4 = x.reshape((per, D_CMP) + x.shape[1:])
        return jnp.sum(x4 * w[None, :D_CMP], axis=1), jnp.sum(x4 * w[None, D_CMP:], axis=1)

    for p in range(pages):
        rs = slice(p * per, (p + 1) * per)
        fk_ref[rs], sk_ref[rs] = sums(k_pages[p][...], wk_ref[...])
        fv_ref[rs], sv_ref[rs] = sums(v_pages[p][...], wv_ref[...])


def _page_chunks(page_table, pool_k, pool_v, wk, wv, pages=16):
    bd, n_pages = page_table.shape
    n_pool, _, n_kv, hd = pool_k.shape
    per = PAGE_SIZE // D_CMP
    nch = n_pages * per
    assert n_pages % pages == 0

    def page_spec(p):
        return pl.BlockSpec((None, PAGE_SIZE, n_kv, hd), lambda b, t, pt: (pt[b, t * pages + p], 0, 0, 0))
    wspec = pl.BlockSpec((L_CMP, n_kv, hd), lambda b, t, pt: (0, 0, 0))
    ospec = pl.BlockSpec((None, pages * per, n_kv, hd), lambda b, t, pt: (b, t, 0, 0))
    osh = jax.ShapeDtypeStruct((bd, nch, n_kv, hd), F32)
    return pl.pallas_call(
        functools.partial(_page_chunks_kernel, pages=pages),
        grid_spec=pltpu.PrefetchScalarGridSpec(
            num_scalar_prefetch=1, grid=(bd, n_pages // pages),
            in_specs=[page_spec(p) for p in range(pages)] * 2 + [wspec, wspec],
            out_specs=[ospec] * 4),
        out_shape=[osh] * 4,
        compiler_params=_params(("arbitrary", "arbitrary"), 52 << 20),
        name="page_chunks",
    )(page_table, *([pool_k] * pages), *([pool_v] * pages), wk, wv)


def _nsa_sample_cmp_kernel(q_ref, sl_ref, fk_ref, sk_ref, fv_ref, sv_ref, cover_ref, oc_ref, idx_ref, ok_ref,
                           *, past, n_sel_blocks):
    n_kv, gqa, _ = q_ref.shape
    nch = fk_ref.shape[0]
    n_cmp = nch - 1
    nj = cover_ref.shape[1]
    q_pos = past
    n_io = lax.broadcasted_iota(jnp.int32, (gqa, nch), 1)
    dist = (q_pos - ((n_io + 2) * D_CMP - 1)).astype(F32)
    ok = (dist >= 0.0) & (n_io < n_cmp)
    j_io = lax.broadcasted_iota(jnp.int32, (gqa, nj), 1)
    cur = q_pos // L_SEL
    valid = (j_io <= cur) & (j_io < n_sel_blocks)
    forced = (j_io == 0) | (j_io == cur) | (j_io == cur - 1)
    lane = lax.broadcasted_iota(jnp.int32, (gqa, V7X_LANES), 1)
    j_f = j_io.astype(F32)
    for h in range(n_kv):
        kc = _blocks_from_sums(fk_ref[:, h, :], sk_ref[:, h, :]).astype(BF16)
        vc = _blocks_from_sums(fv_ref[:, h, :], sv_ref[:, h, :]).astype(BF16)
        s = jnp.where(ok, _nt_dot(q_ref[h], kc) - (sl_ref[h] * LOG2E) * dist, NEG)
        p = jnp.where(ok, jnp.exp2(s - jnp.max(s, axis=1, keepdims=True)), 0.0)
        p = p / jnp.maximum(jnp.sum(p, axis=1, keepdims=True), 1e-30)
        oc_ref[h] = jnp.dot(p.astype(BF16), vc, preferred_element_type=F32)

        psum = jnp.broadcast_to(jnp.sum(p, axis=0, keepdims=True), (gqa, nch))
        imp = jnp.dot(psum, cover_ref[...], precision=lax.Precision.HIGHEST, preferred_element_type=F32)
        score = jnp.where(valid, imp + jnp.where(forced, FORCE_BONUS, 0.0), NEG)
        score = jnp.where(j_io < n_sel_blocks, score, -jnp.inf)
        idx = jnp.zeros((gqa, V7X_LANES), F32)
        okv = jnp.zeros((gqa, V7X_LANES), jnp.int32)
        for t in range(min(N_SEL, n_sel_blocks)):
            mx = jnp.max(score, axis=1, keepdims=True)
            am = jnp.min(jnp.where(score == mx, j_f, float(nj)), axis=1, keepdims=True)
            idx = jnp.where(lane == t, am, idx)
            okv = jnp.where(lane == t, jnp.where(mx > 0.5 * NEG, 1, 0), okv)
            score = jnp.where(j_f == am, -jnp.inf, score)
        idx_ref[h] = idx.astype(jnp.int32)
        ok_ref[h] = okv


def _nsa_sample_cmp(q, slopes_col, sums, past):
    bh, gqa, hd = q.shape
    fk = sums[0]
    bd, nch, n_kv, _ = fk.shape
    n_sel_blocks = past // L_SEL + 1
    nj = -(-n_sel_blocks // V7X_LANES) * V7X_LANES
    n_io = np.arange(nch)[:, None]
    j_io = np.arange(nj)[None, :]
    cover = ((n_io * D_CMP < j_io * L_SEL + L_SEL) & (n_io * D_CMP + L_CMP > j_io * L_SEL)
             & (n_io < nch - 1) & (j_io < n_sel_blocks)).astype(np.float32)
    hspec = pl.BlockSpec((n_kv, gqa, hd), lambda b: (b, 0, 0))
    sspec = pl.BlockSpec((None, nch, n_kv, hd), lambda b: (b, 0, 0, 0))
    ispec = pl.BlockSpec((n_kv, gqa, V7X_LANES), lambda b: (b, 0, 0))
    return pl.pallas_call(
        functools.partial(_nsa_sample_cmp_kernel, past=past, n_sel_blocks=n_sel_blocks),
        grid=(bd,),
        in_specs=[hspec, pl.BlockSpec((n_kv, gqa, 1), lambda b: (0, 0, 0)), sspec, sspec, sspec, sspec,
                  pl.BlockSpec((nch, nj), lambda b: (0, 0))],
        out_specs=[hspec, ispec, ispec],
        out_shape=[jax.ShapeDtypeStruct((bh, gqa, hd), F32),
                   jax.ShapeDtypeStruct((bh, gqa, V7X_LANES), jnp.int32),
                   jax.ShapeDtypeStruct((bh, gqa, V7X_LANES), jnp.int32)],
        compiler_params=_params(("arbitrary",), 48 << 20),
        name="nsa_sample_cmp",
    )(q, slopes_col, *sums, jnp.asarray(cover))


def _nsa_sample_sel_kernel(pt_ref, idx_ref, okb_ref, *refs, past, nsb):
    k_blks = refs[:nsb]
    v_blks = refs[nsb:2 * nsb]
    (q_ref, sl_ref, gt_ref, oc_ref, ksn_ref, vsn_ref, kwc_ref, vwc_ref, kwn_ref, vwn_ref, o_ref) = refs[2 * nsb:]
    b = pl.program_id(0)
    h = pl.program_id(1)
    gqa, hd = q_ref.shape
    base = (b * N_KV + h) * nsb
    nb_past = past // L_SEL
    q_pos = past
    q = q_ref[...]
    slope = sl_ref[...] * LOG2E
    head = lambda ref: ref[:, pl.ds(h, 1), :].reshape(ref.shape[0], hd)

    row_io = lax.broadcasted_iota(jnp.int32, (L_SEL, hd), 0)
    new_k = jnp.where(row_io == 0, jnp.broadcast_to(ksn_ref[...], (L_SEL, hd)), 0.0)
    new_v = jnp.where(row_io == 0, jnp.broadcast_to(vsn_ref[...], (L_SEL, hd)), 0.0)
    lane = lax.broadcasted_iota(jnp.int32, (gqa, nsb * L_SEL), 1)
    kpos = lane & (L_SEL - 1)
    okl = jnp.zeros((gqa, nsb * L_SEL), jnp.int32)
    kg, vg = [], []
    for t in range(nsb):
        bid = idx_ref[base + t]
        is_new = bid >= nb_past
        kg.append(jnp.where(is_new, new_k, head(k_blks[t])).astype(BF16))
        vg.append(jnp.where(is_new, new_v, head(v_blks[t])).astype(BF16))
        in_t = (lane >> SEL_SHIFT) == t
        kpos = kpos + jnp.where(in_t, bid * L_SEL, 0)
        okl = okl + jnp.where(in_t, okb_ref[base + t], 0)
    kg = jnp.concatenate(kg, axis=0)
    vg = jnp.concatenate(vg, axis=0)
    dist = (q_pos - kpos).astype(F32)
    ok = (dist >= 0.0) & (okl > 0)
    s = jnp.where(ok, _nt_dot(q, kg) - slope * dist, NEG)
    p = jnp.where(ok, jnp.exp2(s - jnp.max(s, axis=1, keepdims=True)), 0.0)
    p = p / jnp.maximum(jnp.sum(p, axis=1, keepdims=True), 1e-30)
    o_s = jnp.dot(p.astype(BF16), vg, preferred_element_type=F32)

    wbuf = kwc_ref.shape[0]
    i_io = lax.broadcasted_iota(jnp.int32, (gqa, wbuf), 1)
    kpos_w = past - wbuf + i_io
    dist_w = (q_pos - kpos_w).astype(F32)
    ok_w = (dist_w >= 0.0) & (dist_w <= float(WINDOW)) & (kpos_w >= 0)
    s_w = jnp.where(ok_w, _nt_dot(q, head(kwc_ref).astype(BF16)) - slope * dist_w, NEG)
    kn = jnp.broadcast_to(kwn_ref[...], (V7X_SUBLANES, hd)).astype(BF16)
    s_n = _nt_dot(q, kn)[:, 0:1]
    m = jnp.maximum(jnp.max(s_w, axis=1, keepdims=True), s_n)
    p_w = jnp.where(ok_w, jnp.exp2(s_w - m), 0.0)
    p_n = jnp.exp2(s_n - m)
    den = jnp.maximum(jnp.sum(p_w, axis=1, keepdims=True) + p_n, 1e-30)
    p_w = p_w / den
    p_n = p_n / den
    o_w = (jnp.dot(p_w.astype(BF16), head(vwc_ref).astype(BF16), preferred_element_type=F32)
           + p_n.astype(BF16).astype(F32) * vwn_ref[...].astype(BF16).astype(F32))

    gt = gt_ref[...]
    o_ref[...] = (gt[:, 0:1] * oc_ref[...] + gt[:, 1:2] * o_s + gt[:, 2:3] * o_w).astype(BF16)


def _nsa_sample_sel(page_table, idx_flat, ok_flat, pool_k, pool_v, q, slopes_col, gates, o_c,
                    ks_new, vs_new, cache_kw, cache_vw, kw_new, vw_new, past):
    bh, gqa, hd = q.shape
    bd = bh // N_KV
    nsb = idx_flat.shape[0] // bh
    c = N_KV * hd
    halves = PAGE_SIZE // L_SEL
    nb_past = past // L_SEL
    wbuf = cache_kw.shape[1]

    def blk_spec(t):
        def imap(b, h, pt, idx, okb):
            bid = jnp.minimum(idx[(b * N_KV + h) * nsb + t], nb_past - 1)
            return (pt[b, bid // halves], bid % halves, 0, 0)
        return pl.BlockSpec((None, L_SEL, N_KV, hd), imap)
    hspec = pl.BlockSpec((None, gqa, hd), lambda b, h, *_: (b * N_KV + h, 0, 0))
    nspec = pl.BlockSpec((None, 1, hd), lambda b, h, *_: (b, 0, h))
    wspec = pl.BlockSpec((None, wbuf, N_KV, hd), lambda b, h, *_: (b, 0, 0, 0))
    return pl.pallas_call(
        functools.partial(_nsa_sample_sel_kernel, past=past, nsb=nsb),
        grid_spec=pltpu.PrefetchScalarGridSpec(
            num_scalar_prefetch=3, grid=(bd, N_KV),
            in_specs=[blk_spec(t) for t in range(nsb)] * 2
            + [hspec, pl.BlockSpec((None, gqa, 1), lambda b, h, *_: (h, 0, 0)),
               pl.BlockSpec((None, gqa, 3), lambda b, h, *_: (b * N_KV + h, 0, 0)), hspec,
               nspec, nspec, wspec, wspec, nspec, nspec],
            out_specs=hspec),
        out_shape=jax.ShapeDtypeStruct((bh, gqa, hd), BF16),
        compiler_params=_params(("arbitrary", "arbitrary"), 40 << 20),
        name="nsa_sample_sel",
    )(page_table, idx_flat, ok_flat, *([pool_k] * nsb), *([pool_v] * nsb), q, slopes_col, gates, o_c,
      ks_new.reshape(bd, 1, c), vs_new.reshape(bd, 1, c), cache_kw, cache_vw,
      kw_new.reshape(bd, 1, c), vw_new.reshape(bd, 1, c))


TM_MM = 1024
TM_ROWS = 256
TQ = 256


def kernel(x_prompt, x_sample, c_prompt, c_sample, page_table, cache_k_cmp, cache_v_cmp, cache_k_sel, cache_v_sel, cache_k_win, cache_v_win, state_h, state_conv, w_ada, b_ada, ln_g, ln_b, a_w_in, a_conv_w, a_conv_b, a_w_r, a_b_r, a_w_i, a_b_i, a_lambda, a_w_out, w_kv, w_cmp_k, w_cmp_v, b_w_qg, b_w_o, f_w_up, f_w_down):
    nb, s_len, d = x_prompt.shape
    bd, sd, _ = x_sample.shape
    assert sd == 1
    depth = w_ada.shape[0]
    n_a = a_w_in.shape[0]
    assert depth - n_a == 1 or depth == n_a, "one shared-KV NSA layer stack"
    d_ff = f_w_down.shape[1]
    lru = a_w_out.shape[1]
    n_heads = d // HEAD_DIM
    gqa = n_heads // N_KV
    ckv = N_KV * HEAD_DIM
    past = page_table.shape[1] * PAGE_SIZE
    alpha = (2.0 * depth) ** 0.25
    t = nb * s_len
    tm_mm = min(TM_MM, s_len)
    tm_rows = min(TM_ROWS, s_len)
    tn = 512

    rows = -(-(nb + bd) // V7X_SUBLANES) * V7X_SUBLANES
    c_all = jnp.concatenate([c_prompt, c_sample, jnp.zeros((rows - nb - bd, d), F32)], axis=0)
    mods = _ada(c_all, w_ada, b_ada)

    def mod_p(l, k):
        return mods[l, :nb, k * d:(k + 1) * d].reshape(nb, 1, d)

    def mod_s(l, k):
        return mods[l, nb:nb + bd, k * d:(k + 1) * d].reshape(1, bd, d)

    slopes = jnp.exp2(-8.0 * jnp.arange(1, n_heads + 1, dtype=F32) / n_heads)
    slopes_col = slopes.reshape(N_KV, gqa, 1)

    xp = x_prompt.reshape(t, d)
    xs = x_sample.reshape(bd, d)
    up = _modulate(xp, mod_p(0, 0), mod_p(0, 1), tm_rows)
    us = _modulate(xs, mod_s(0, 0), mod_s(0, 1), bd)

    w_down_bf = f_w_down.astype(BF16)

    def ffn(x_p, x_s, u_p, u_s, l, nxt):
        hmid = _matmul(u_p, u_s, f_w_up, d_ff, layer=l, tm=min(2 * tm_mm, s_len), tn=256, out_dtype=BF16,
                       epilogue=_swiglu_epilogue, col_off=0, col_off2=d_ff, name="ffn_up")
        ys = _matmul(hmid[0], hmid[1], w_down_bf, d, layer=l, tm=min(tm_mm, 512), tn=tn, out_dtype=F32,
                     name="ffn_down")
        outs = []
        for x, y, tm_r, mod in ((x_p, ys[0], tm_rows, mod_p), (x_s, ys[1], bd, mod_s)):
            if nxt is None:
                outs.append(_ln_mod(x, y, mod(l, 5), ln_g[l, 1], ln_b[l, 1], alpha, tm_r))
            else:
                outs.append(_ln_mod(x, y, mod(l, 5), ln_g[l, 1], ln_b[l, 1], alpha, tm_r,
                                    shift=mod(nxt[0], 0), scale=mod(nxt[0], 1), want_xb=nxt[1]))
        return outs

    h_p, h_s, cb_p, cb_s = [], [], [], []
    kv_p = kv_s = None
    for l in range(depth):
        if l < n_a:
            wr = a_w_r[l].astype(BF16)
            wi = a_w_i[l].astype(BF16)
            ggx_p, ggx_s = _matmul(up, us, a_w_in, lru, layer=l, tm=tm_mm, tn=tn, out_dtype=BF16,
                                   epilogue=_gelu_tanh, name="lru_in_g")
            rx_p, rx_s = _matmul(up, us, a_w_in, lru, layer=l, tm=tm_mm, tn=tn, out_dtype=F32, col_off=lru,
                                 name="lru_in_r")
            yin_p, hl_p, ct_p = _rglru_prompt(
                rx_p, ggx_p, jnp.zeros((nb, CONV_W - 1, lru), F32), jnp.zeros((nb, lru), F32),
                a_conv_w[l], a_conv_b[l], wr, a_b_r[l], wi, a_b_i[l], a_lambda[l], nb, min(TM_ROWS, s_len))
            yin_s, hl_s, ct_s = _rglru_step(rx_s, ggx_s, jnp.swapaxes(state_conv[l], 0, 1), state_h[l],
                                            a_conv_w[l], a_conv_b[l], wr, a_b_r[l], wi, a_b_i[l], a_lambda[l])
            h_p.append(hl_p.reshape(nb, lru))
            h_s.append(hl_s)
            cb_p.append(ct_p)
            cb_s.append(jnp.swapaxes(ct_s, 0, 1))
            y_p, y_s = _matmul(yin_p, yin_s, a_w_out, d, layer=l, tm=tm_mm, tn=tn, out_dtype=F32, name="lru_out")
        else:
            j = l - n_a
            w_g = b_w_qg[j][:, n_heads * HEAD_DIM:].reshape(d, N_KV, 3 * gqa)
            w_g = jnp.pad(w_g, ((0, 0), (0, 0), (0, V7X_LANES - 3 * gqa))).reshape(d, N_KV * V7X_LANES)
            w_q = b_w_qg[j, :, :n_heads * HEAD_DIM].astype(BF16)
            q_p, q_s = _matmul(up, us, w_q, d, tm=tm_mm, tn=tn, out_dtype=BF16,
                               epilogue=lambda a: a * EXP2_SCALE, name="nsa_q")
            g_p, g_s = _matmul(up, us, w_g, N_KV * V7X_LANES, tm=tm_mm, tn=N_KV * V7X_LANES, out_dtype=F32,
                               epilogue=jax.nn.sigmoid, name="nsa_gate")
            o_p = _nsa_prompt(q_p, g_p, ctx_p[0], ctx_p[1], kv_p[2], kv_p[3], kv_p[4], kv_p[5], slopes, nb,
                              min(TQ, s_len))
            q_s = q_s.reshape(bd * N_KV, gqa, HEAD_DIM)
            g_s = g_s.reshape(bd * N_KV, V7X_LANES)[:, :3 * gqa].reshape(bd * N_KV, gqa, 3)
            o_c, idx, okb = _nsa_sample_cmp(q_s, slopes_col, ctx_s, past)
            nsb = min(N_SEL, past // L_SEL + 1)
            o_s = _nsa_sample_sel(page_table, idx[:, 0, :nsb].reshape(-1), okb[:, 0, :nsb].reshape(-1),
                                  cache_k_sel, cache_v_sel, q_s, slopes_col, g_s, o_c,
                                  kv_s[2], kv_s[3], cache_k_win, cache_v_win, kv_s[4], kv_s[5], past)
            y_p, y_s = _matmul(o_p, o_s.reshape(bd, d), b_w_o, d, layer=j, tm=tm_mm, tn=tn, out_dtype=F32,
                               name="nsa_out")

        xp, up = _ln_mod(xp, y_p, mod_p(l, 2), ln_g[l, 0], ln_b[l, 0], alpha, tm_rows,
                         shift=mod_p(l, 3), scale=mod_p(l, 4))
        xs, us = _ln_mod(xs, y_s, mod_s(l, 2), ln_g[l, 0], ln_b[l, 0], alpha, bd,
                         shift=mod_s(l, 3), scale=mod_s(l, 4))
        last = l == depth - 1
        res_p, res_s = ffn(xp, xs, up, us, l, None if last else (l + 1, l == n_a - 1))
        if last:
            xp, xs = res_p[0], res_s[0]
        elif l == n_a - 1:
            xp, up, xbp = res_p
            xs, us, xbs = res_s
            kv = [_matmul(xbp, xbs, w_kv, ckv, tm=tm_mm, tn=ckv, out_dtype=F32, col_off=jj * ckv,
                          heads=(N_KV, HEAD_DIM), name="kv_proj") for jj in range(6)]
            kv_p, kv_s, kv_p4, kv_s4 = zip(*kv)
            ctx_p = _cmp_blocks_prompt(kv_p[0], kv_p[1], w_cmp_k, w_cmp_v, nb)
            ctx_s = _page_chunks(page_table, cache_k_cmp, cache_v_cmp, w_cmp_k, w_cmp_v)
        else:
            xp, up = res_p
            xs, us = res_s

    wwin = min(WINDOW, s_len)
    shp_p = (nb, s_len, N_KV, HEAD_DIM)
    shp_s = (bd, sd, N_KV, HEAD_DIM)
    kvp = [a.reshape(shp_p) for a in kv_p4]
    kvs = [a.reshape(shp_s) for a in kv_s4]
    wbuf = cache_k_win.shape[1]
    keep = wbuf - sd
    k_win_s = jnp.concatenate([cache_k_win[:, wbuf - keep:], kvs[4]], axis=1)
    v_win_s = jnp.concatenate([cache_v_win[:, wbuf - keep:], kvs[5]], axis=1)
    return (xp.reshape(nb, s_len, d), xs.reshape(bd, sd, d),
            kvp[0], kvs[0], kvp[1], kvs[1], kvp[2], kvs[2], kvp[3], kvs[3],
            kvp[4][:, -wwin:], k_win_s, kvp[5][:, -wwin:], v_win_s,
            jnp.stack(h_p), jnp.stack(h_s), jnp.stack(cb_p), jnp.stack(cb_s))
```

```python
import functools

import numpy as np
import jax
import jax.numpy as jnp
from jax import lax
from jax.experimental import pallas as pl
from jax.experimental.pallas import tpu as pltpu

F32 = jnp.float32
BF16 = jnp.bfloat16

HEAD_DIM = 128
N_KV = 4
L_CMP = 32
D_CMP = 16
L_SEL = 64
N_SEL = 16
WINDOW = 512
PAGE_SIZE = 128
CONV_W = 4
LRU_BLOCKS = 16
LRU_C = 8.0
NEG = -1e30
FORCE_BONUS = 1e3
ATTN_SCALE = HEAD_DIM ** -0.5
LN_EPS = 1e-5
EXP2_SCALE = ATTN_SCALE * 1.4426950408889634
ATTN_ROW_CHUNK = 32
SEL_SHIFT = L_SEL.bit_length() - 1
assert 1 << SEL_SHIFT == L_SEL

V7X_VMEM_BYTES = 64 * 1024 * 1024
V7X_LANES = 128
V7X_SUBLANES = 8
VMEM_CAP = V7X_VMEM_BYTES - 3 * 1024 * 1024


def _params(sem, vmem_bytes):
    return pltpu.CompilerParams(dimension_semantics=sem,
                                vmem_limit_bytes=int(min(max(vmem_bytes, 16 << 20), VMEM_CAP)))


def _nt_dot(a, b, **kw):
    return lax.dot_general(a, b, (((1,), (1,)), ((), ())), preferred_element_type=F32, **kw)


def _ada_block(c_ref, w_ref, b_ref, o_ref):
    c = c_ref[...]
    s = (c * jax.nn.sigmoid(c)).astype(BF16)
    o_ref[...] = jnp.dot(s, w_ref[...].astype(BF16), preferred_element_type=F32) + b_ref[...]


def _ada(c_all, w_ada, b_ada, layer, tn=1024):
    depth, d, n6 = w_ada.shape
    r = c_all.shape[0]
    return pl.pallas_call(
        _ada_block,
        grid=(n6 // tn,),
        in_specs=[pl.BlockSpec((r, d), lambda n: (0, 0)),
                  pl.BlockSpec((None, d, tn), lambda n: (layer, 0, n)),
                  pl.BlockSpec((None, 1, tn), lambda n: (layer, 0, n))],
        out_specs=pl.BlockSpec((r, tn), lambda n: (0, n)),
        out_shape=jax.ShapeDtypeStruct((r, n6), F32),
        compiler_params=_params(("arbitrary",), 2 * d * tn * 4 + d * tn * 2 + (8 << 20)),
        name="ada",
    )(c_all, w_ada, b_ada.reshape(depth, 1, n6))


def _modulate_kernel(x_ref, sh_ref, sc_ref, u_ref):
    u_ref[...] = (x_ref[...] * (1.0 + sc_ref[...]) + sh_ref[...]).astype(u_ref.dtype)


def _row_specs(t, d, tm, nb):
    tiles_per_batch = (t // nb) // tm
    xspec = pl.BlockSpec((tm, d), lambda i: (i, 0))

    def vspec(r):
        return pl.BlockSpec((None, r, d), lambda i: (i // tiles_per_batch, 0, 0))
    return xspec, vspec


def _modulate(x, shift, scale, tm):
    t, d = x.shape
    nb, r, _ = shift.shape
    xspec, vspec = _row_specs(t, d, tm, nb)
    return pl.pallas_call(
        _modulate_kernel,
        grid=(t // tm,),
        in_specs=[xspec, vspec(r), vspec(r)],
        out_specs=xspec,
        out_shape=jax.ShapeDtypeStruct((t, d), BF16),
        compiler_params=_params(("arbitrary",), 6 * tm * d * 4),
        name="modulate",
    )(x, shift, scale)


def _ln_mod_kernel(*refs, alpha, want_u, want_xb):
    x_ref, y_ref, gate_ref, g_ref, b_ref = refs[:5]
    pos = 5
    if want_u:
        sh_ref, sc_ref = refs[pos:pos + 2]
        pos += 2
    xo_ref = refs[pos]
    pos += 1
    v = alpha * x_ref[...] + (1.0 + gate_ref[...]) * y_ref[...]
    mu = jnp.mean(v, axis=-1, keepdims=True)
    dlt = v - mu
    var = jnp.mean(dlt * dlt, axis=-1, keepdims=True)
    xn = dlt * lax.rsqrt(var + LN_EPS) * g_ref[...] + b_ref[...]
    xo_ref[...] = xn
    if want_u:
        refs[pos][...] = (xn * (1.0 + sc_ref[...]) + sh_ref[...]).astype(BF16)
        pos += 1
    if want_xb:
        refs[pos][...] = xn.astype(BF16)


def _ln_mod(x, y, gate, ln_g, ln_b, alpha, tm, shift=None, scale=None, want_xb=False):
    t, d = x.shape
    nb, r, _ = gate.shape
    xspec, vspec = _row_specs(t, d, tm, nb)
    pspec = pl.BlockSpec((1, d), lambda i: (0, 0))
    want_u = shift is not None
    ins = [x, y, gate, ln_g.reshape(1, d), ln_b.reshape(1, d)]
    in_specs = [xspec, xspec, vspec(r), pspec, pspec]
    out_shape = [jax.ShapeDtypeStruct((t, d), F32)]
    out_specs = [xspec]
    if want_u:
        ins += [shift, scale]
        in_specs += [vspec(r), vspec(r)]
        out_shape.append(jax.ShapeDtypeStruct((t, d), BF16))
        out_specs.append(xspec)
    if want_xb:
        out_shape.append(jax.ShapeDtypeStruct((t, d), BF16))
        out_specs.append(xspec)
    return pl.pallas_call(
        functools.partial(_ln_mod_kernel, alpha=alpha, want_u=want_u, want_xb=want_xb),
        grid=(t // tm,),
        in_specs=in_specs, out_specs=out_specs, out_shape=out_shape,
        compiler_params=_params(("arbitrary",), 12 * tm * d * 4),
        name="ln_mod",
    )(*ins)


def _gelu_tanh(x):
    return jax.nn.gelu(x, approximate=True)


def _mm_kernel(*refs, n_w, cast_w, epilogue, heads, ada_blocks):
    x_ref, xs_ref = refs[:2]
    w_refs = refs[2:2 + n_w]
    n_in = 2 + n_w + (3 if ada_blocks else 0)
    n_out = (4 if heads else 2) + (1 if ada_blocks else 0)
    outs = refs[n_in:n_in + n_out]
    scr = refs[n_in + n_out:]
    first_row_tile = pl.program_id(1) == 0
    if ada_blocks:
        step = pl.program_id(0) * pl.num_programs(1) + pl.program_id(1)
        pl.when(step < ada_blocks)(lambda: _ada_block(*refs[2 + n_w:n_in], outs[-1]))
    if cast_w:
        @pl.when(first_row_tile)
        def _():
            for w_ref, s_ref in zip(w_refs, scr):
                s_ref[...] = w_ref[...].astype(BF16)
        w_refs = scr

    def product(rows_ref, out_ref, heads_ref):
        rows = rows_ref[...]
        accs = [jnp.dot(rows, w_ref[...], preferred_element_type=F32) for w_ref in w_refs]
        val = epilogue(*accs).astype(out_ref.dtype)
        out_ref[...] = val
        if heads_ref is not None:
            hd = heads_ref.shape[-1]
            for h in range(heads_ref.shape[-2]):
                heads_ref[:, h, :] = val[:, h * hd:(h + 1) * hd]

    product(x_ref, outs[0], outs[2] if heads else None)
    pl.when(first_row_tile)(lambda: product(xs_ref, outs[1], outs[3] if heads else None))


def _matmul(x, xs, w, n_out, *, tm, tn, out_dtype, layer=None, epilogue=None, col_off=0, col_off2=None,
            heads=None, ada=None, name="mm"):
    m, k = x.shape
    ms = xs.shape[0]
    assert m % tm == 0 and n_out % tn == 0 and col_off % tn == 0
    assert (w.ndim == 3) == (layer is not None) and w.shape[-2] == k and xs.shape[1] == k
    offs = [col_off // tn] + ([] if col_off2 is None else [col_off2 // tn])
    n_w = len(offs)
    if epilogue is None:
        epilogue = lambda a: a
    cast_w = w.dtype != BF16
    scratch = [pltpu.VMEM((k, tn), BF16) for _ in range(n_w)] if cast_w else []
    w_bytes = k * tn * w.dtype.itemsize
    vmem = (2 * tm * k * 2 + n_w * (2 * w_bytes + len(scratch) * k * tn * 2)
            + 2 * tm * tn * 4 * (n_w + 1) + (4 << 20))
    if layer is None:
        w_specs = [pl.BlockSpec((k, tn), functools.partial(lambda n, i, o: (0, n + o), o=o)) for o in offs]
    else:
        w_specs = [pl.BlockSpec((None, k, tn), functools.partial(lambda n, i, o: (layer, 0, n + o), o=o))
                   for o in offs]
    out_specs = [pl.BlockSpec((tm, tn), lambda n, i: (i, n)), pl.BlockSpec((ms, tn), lambda n, i: (0, n))]
    out_shape = [jax.ShapeDtypeStruct((m, n_out), out_dtype), jax.ShapeDtypeStruct((ms, n_out), out_dtype)]
    if heads is not None:
        assert n_out == tn == heads[0] * heads[1]
        out_specs += [pl.BlockSpec((tm,) + heads, lambda n, i: (i, 0, 0)),
                      pl.BlockSpec((ms,) + heads, lambda n, i: (0, 0, 0))]
        out_shape += [jax.ShapeDtypeStruct((m,) + heads, out_dtype), jax.ShapeDtypeStruct((ms,) + heads, out_dtype)]
        vmem += 4 * tm * tn * 4
    m_tiles = m // tm
    extra_in, extra_specs, ada_blocks = [], [], 0
    if ada is not None:
        c_all, w_ada, b_ada, ada_layer = ada
        depth, d, n6 = w_ada.shape
        r = c_all.shape[0]
        steps = (n_out // tn) * m_tiles
        atn = next(c for c in range(V7X_LANES, n6 + 1, V7X_LANES) if n6 % c == 0 and n6 // c <= steps)
        ada_blocks = n6 // atn

        def blk(n, i):
            return jnp.minimum(n * m_tiles + i, ada_blocks - 1)
        extra_in = [c_all, w_ada, b_ada.reshape(depth, 1, n6)]
        extra_specs = [pl.BlockSpec((r, d), lambda n, i: (0, 0)),
                       pl.BlockSpec((None, d, atn), lambda n, i: (ada_layer, 0, blk(n, i))),
                       pl.BlockSpec((None, 1, atn), lambda n, i: (ada_layer, 0, blk(n, i)))]
        out_specs.append(pl.BlockSpec((r, atn), lambda n, i: (0, blk(n, i))))
        out_shape.append(jax.ShapeDtypeStruct((r, n6), F32))
        vmem += 2 * d * atn * 4 + d * atn * 2
    return pl.pallas_call(
        functools.partial(_mm_kernel, n_w=n_w, cast_w=cast_w, epilogue=epilogue, heads=heads is not None,
                          ada_blocks=ada_blocks),
        grid=(n_out // tn, m_tiles),
        in_specs=[pl.BlockSpec((tm, k), lambda n, i: (i, 0)), pl.BlockSpec((ms, k), lambda n, i: (0, 0))]
        + w_specs + extra_specs,
        out_specs=out_specs, out_shape=out_shape,
        scratch_shapes=scratch,
        compiler_params=_params(("arbitrary", "arbitrary"), vmem),
        name=name,
    )(x, xs, *([w] * n_w), *extra_in)


def _swiglu_epilogue(g, v):
    return g * jax.nn.sigmoid(g) * v


def _softplus_neg(lam):
    return jnp.maximum(-lam, 0.0) + jnp.log1p(jnp.exp(-jnp.abs(lam)))


def _lru_gates(xc, wr_ref, br_ref, wi_ref, bi_ref, sp, a_ref, b_ref, nblk, bw):
    def put(ref, k, val):
        ref[:, k * bw:(k + 1) * bw] = val

    nsp = (-LRU_C) * sp
    xcb = xc.astype(BF16)
    for k in range(nblk):
        cs = slice(k * bw, (k + 1) * bw)
        xb = xcb[:, cs]
        r = jax.nn.sigmoid(jnp.dot(xb, wr_ref[k], preferred_element_type=F32) + br_ref[:, cs])
        ig = jax.nn.sigmoid(jnp.dot(xb, wi_ref[k], preferred_element_type=F32) + bi_ref[:, cs])
        log_a = r * nsp[:, cs]
        put(a_ref, k, jnp.exp(log_a))
        th = jnp.tanh(log_a)
        put(b_ref, k, jnp.sqrt(-2.0 * th / (1.0 - th)) * (ig * xc[:, cs]))


SCAN_TILES = 8


def _rglru_kernel(rx_ref, ggx_ref, cb0_ref, h0_ref, cw_ref, cbias_ref, wr_ref, br_ref, wi_ref, bi_ref,
                  lam_ref, y_ref, hlast_ref, ctail_ref, xpad_ref, a_ref, b_ref, hcar_ref, *, tm, nblk, bw):
    s = pl.program_id(1)
    width = nblk * bw
    pad = V7X_SUBLANES
    tail = CONV_W - 1

    @pl.when(s == 0)
    def _():
        xpad_ref[pad - tail:pad, :] = cb0_ref[...]
        hcar_ref[...] = h0_ref[...]

    rx = rx_ref[...]
    xpad_ref[pad:pad + tm, :] = rx
    cw = cw_ref[...]
    xc = cbias_ref[...] + rx * cw[tail:tail + 1]
    for j in range(tail):
        xc = xc + xpad_ref[pad - tail + j:pad - tail + j + tm, :] * cw[j:j + 1]
    xpad_ref[pad - tail:pad, :] = rx[tm - tail:tm, :]

    _lru_gates(xc, wr_ref, br_ref, wi_ref, bi_ref, _softplus_neg(lam_ref[...]), a_ref, b_ref, nblk, bw)

    scan_lanes = min(SCAN_TILES * V7X_LANES, width)
    row = lax.broadcasted_iota(jnp.int32, (V7X_SUBLANES, scan_lanes), 0)
    for c in range(width // scan_lanes):
        cs = slice(c * scan_lanes, (c + 1) * scan_lanes)

        def group(gi, h, cs=cs):
            r0 = pl.multiple_of(gi * V7X_SUBLANES, V7X_SUBLANES)
            a = a_ref[pl.ds(r0, V7X_SUBLANES), cs]
            b = b_ref[pl.ds(r0, V7X_SUBLANES), cs]
            for sft in (1, 2, 4):
                keep = row >= sft
                a_s = jnp.where(keep, pltpu.roll(a, sft, 0), 1.0)
                b_s = jnp.where(keep, pltpu.roll(b, sft, 0), 0.0)
                b = a * b_s + b
                a = a * a_s
            hs = a * h + b
            b_ref[pl.ds(r0, V7X_SUBLANES), cs] = hs
            return jnp.broadcast_to(hs[V7X_SUBLANES - 1:V7X_SUBLANES, :], hs.shape)

        h_in = jnp.broadcast_to(hcar_ref[:, cs], (V7X_SUBLANES, scan_lanes))
        h_out = lax.fori_loop(0, tm // V7X_SUBLANES, group, h_in)
        hcar_ref[:, cs] = h_out[0:1, :]

    y_ref[...] = (ggx_ref[...].astype(F32) * b_ref[...]).astype(BF16)

    @pl.when(s == pl.num_programs(1) - 1)
    def _():
        hlast_ref[...] = hcar_ref[...]
        ctail_ref[...] = rx[tm - tail:tm, :]


def _rglru_prompt(rx, ggx, conv_buf, h0, cw, cbias, wr, br, wi, bi, lam, nb, tm):
    t, width = rx.shape
    s_len = t // nb
    nblk, bw, _ = wr.shape
    tail = CONV_W - 1
    ns = s_len // tm
    row = lambda a: a.reshape(1, width)
    xspec = pl.BlockSpec((tm, width), lambda b, s: (b * ns + s, 0))
    pspec = pl.BlockSpec((1, width), lambda b, s: (0, 0))
    wspec = pl.BlockSpec((nblk, bw, bw), lambda b, s: (0, 0, 0))
    vmem = (2 * tm * width * (4 + 2 + 2) + 3 * (tm + 8) * width * 4 + 4 * nblk * bw * bw * 2 + (6 << 20))
    return pl.pallas_call(
        functools.partial(_rglru_kernel, tm=tm, nblk=nblk, bw=bw),
        grid=(nb, ns),
        in_specs=[xspec, xspec,
                  pl.BlockSpec((None, tail, width), lambda b, s: (b, 0, 0)),
                  pl.BlockSpec((None, 1, width), lambda b, s: (b, 0, 0)),
                  pl.BlockSpec((CONV_W, width), lambda b, s: (0, 0)), pspec,
                  wspec, pspec, wspec, pspec, pspec],
        out_specs=[xspec,
                   pl.BlockSpec((None, 1, width), lambda b, s: (b, 0, 0)),
                   pl.BlockSpec((None, tail, width), lambda b, s: (b, 0, 0))],
        out_shape=[jax.ShapeDtypeStruct((t, width), BF16),
                   jax.ShapeDtypeStruct((nb, 1, width), F32),
                   jax.ShapeDtypeStruct((nb, tail, width), F32)],
        scratch_shapes=[pltpu.VMEM((tm + V7X_SUBLANES, width), F32), pltpu.VMEM((tm, width), F32),
                        pltpu.VMEM((tm, width), F32), pltpu.VMEM((1, width), F32)],
        compiler_params=_params(("arbitrary", "arbitrary"), vmem),
        name="rglru",
    )(rx, ggx, conv_buf, h0.reshape(nb, 1, width), cw, row(cbias), wr, row(br), wi, row(bi), row(lam))


def _rglru_step_kernel(rx_ref, ggx_ref, cb_ref, h0_ref, cw_ref, cbias_ref, wr_ref, br_ref, wi_ref, bi_ref,
                       lam_ref, y_ref, h_ref, cnew_ref, a_ref, b_ref, *, nblk, bw):
    tail = CONV_W - 1
    rx = rx_ref[...]
    cw = cw_ref[...]
    xc = cbias_ref[...] + rx * cw[tail:tail + 1]
    for j in range(tail):
        xc = xc + cb_ref[j] * cw[j:j + 1]
    _lru_gates(xc, wr_ref, br_ref, wi_ref, bi_ref, _softplus_neg(lam_ref[...]), a_ref, b_ref, nblk, bw)
    h = a_ref[...] * h0_ref[...] + b_ref[...]
    h_ref[...] = h
    y_ref[...] = (ggx_ref[...].astype(F32) * h).astype(BF16)
    for j in range(tail - 1):
        cnew_ref[j] = cb_ref[j + 1]
    cnew_ref[tail - 1] = rx


def _rglru_step(rx, ggx, conv_t, h0, cw, cbias, wr, br, wi, bi, lam):
    bd, width = rx.shape
    nblk, bw, _ = wr.shape
    row = lambda a: a.reshape(1, width)
    return pl.pallas_call(
        functools.partial(_rglru_step_kernel, nblk=nblk, bw=bw),
        out_shape=[jax.ShapeDtypeStruct((bd, width), BF16), jax.ShapeDtypeStruct((bd, width), F32),
                   jax.ShapeDtypeStruct(conv_t.shape, F32)],
        scratch_shapes=[pltpu.VMEM((bd, width), F32), pltpu.VMEM((bd, width), F32)],
        compiler_params=pltpu.CompilerParams(vmem_limit_bytes=32 << 20),
        name="rglru_step",
    )(rx, ggx, conv_t, h0, cw, row(cbias), wr, row(br), wi, row(bi), row(lam))


def _chunk_sums(x, w):
    n = x.shape[0] // D_CMP
    x3 = x.reshape(n, D_CMP, x.shape[1])
    first = jnp.sum(x3 * w[None, :D_CMP, :], axis=1)
    second = jnp.sum(x3 * w[None, D_CMP:, :], axis=1)
    return first, second


def _blocks_from_sums(first, second):
    n = first.shape[0]
    nxt = pltpu.roll(second, n - 1, 0)
    rows = lax.broadcasted_iota(jnp.int32, first.shape, 0)
    return jnp.where(rows < n - 1, first + nxt, 0.0)


COEF_PIECES = 3
POS_LOW_BITS = 7
POS_OFF = 64
MASK_BIG = 2.0 ** 100
LOG2E = 1.4426950408889634


def _pos_lanes(pos, block_of_key=None):
    lane = lax.broadcasted_iota(jnp.int32, pos.shape, 1)
    hi = ((pos >> POS_LOW_BITS) << POS_LOW_BITS).astype(F32)
    lo = (pos & ((1 << POS_LOW_BITS) - 1)).astype(F32)
    k = lane - POS_OFF
    out = jnp.where((k >= 0) & (k < COEF_PIECES), hi, jnp.where((k >= COEF_PIECES) & (k < 2 * COEF_PIECES), lo, 0.0))
    if block_of_key is not None:
        out = jnp.where((lane < POS_OFF) & (lane == block_of_key), 1.0, out)
    return out.astype(BF16)


def _coef_lanes(coef, shape):
    v = jnp.full(shape, coef, F32)
    c1 = v.astype(BF16).astype(F32)
    c2 = (v - c1).astype(BF16).astype(F32)
    c3 = (v - c1 - c2).astype(BF16).astype(F32)
    k = lax.broadcasted_iota(jnp.int32, shape, 1) - POS_OFF
    live = (k >= 0) & (k < 2 * COEF_PIECES)
    k = jnp.where(k >= COEF_PIECES, k - COEF_PIECES, k)
    return jnp.where(live, jnp.where(k == 0, c1, jnp.where(k == 1, c2, c3)), 0.0)


def _cmp_blocks_kernel(k_ref, v_ref, wk_ref, wv_ref, kc_ref, vc_ref):
    hd = k_ref.shape[1]
    nch = kc_ref.shape[0]
    kc_ref[:, :hd] = _blocks_from_sums(*_chunk_sums(k_ref[...], wk_ref[...])).astype(BF16)
    ends = (lax.broadcasted_iota(jnp.int32, (nch, V7X_LANES), 0) + 2) * D_CMP - 1
    kc_ref[:, hd:] = _pos_lanes(ends)
    vc_ref[...] = _blocks_from_sums(*_chunk_sums(v_ref[...], wv_ref[...])).astype(BF16)


def _cmp_blocks_prompt(k_c, v_c, wk, wv, nb):
    t, c = k_c.shape
    s_len = t // nb
    nch = s_len // D_CMP
    kvspec = pl.BlockSpec((s_len, HEAD_DIM), lambda b, h: (b, h))
    wspec = pl.BlockSpec((L_CMP, HEAD_DIM), lambda b, h: (0, h))

    def ospec(w):
        return pl.BlockSpec((None, None, nch, w), lambda b, h: (b, h, 0, 0))
    return pl.pallas_call(
        _cmp_blocks_kernel, grid=(nb, N_KV),
        in_specs=[kvspec, kvspec, wspec, wspec],
        out_specs=[ospec(HEAD_DIM + V7X_LANES), ospec(HEAD_DIM)],
        out_shape=[jax.ShapeDtypeStruct((nb, N_KV, nch, HEAD_DIM + V7X_LANES), BF16),
                   jax.ShapeDtypeStruct((nb, N_KV, nch, HEAD_DIM), BF16)],
        compiler_params=_params(("arbitrary", "arbitrary"), 24 << 20),
        name="cmp_blocks",
    )(k_c, v_c, wk.reshape(L_CMP, c), wv.reshape(L_CMP, c))


def _block_scores_topk_t(imp_t, q0, nsel, tq):
    nsp = -(-nsel // V7X_SUBLANES) * V7X_SUBLANES
    jt = lax.broadcasted_iota(jnp.int32, (nsp, tq), 0)
    qpos = q0 + lax.broadcasted_iota(jnp.int32, (nsp, tq), 1)
    cur = qpos >> SEL_SHIFT
    valid = (jt <= cur) & (jt < nsel)
    forced = (jt == 0) | (jt == cur) | (jt == cur - 1)
    score = jnp.where(valid, imp_t[:nsp] + jnp.where(forced, FORCE_BONUS, 0.0), NEG)
    rank = jnp.zeros((nsp, tq), F32)
    for i in range(nsel):
        ri = score[i:i + 1, :]
        beats = (ri > score) | ((ri == score) & (jt > i))
        rank = rank + jnp.where(beats, 1.0, 0.0)
    return jnp.where((rank < float(min(N_SEL, nsel))) & valid, 1.0, 0.0)


def _nsa_prompt_kernel(slopes_ref, q_ref, gt_ref, kc_ref, vc_ref, ks_ref, vs_ref, kw_ref, vw_ref,
                       o_ref, ksa, vsb, kwa, vwb, qa_ref, s_ref, p_ref, bdiag_ref, bwin_ref, bcmp_ref,
                       m_ref, acc_ref, oc_ref, os_ref, ow_ref, psum_ref, *, tq, gqa, s_len):
    h = pl.program_id(1)
    i = pl.program_id(2)
    q0 = i * tq
    hd = HEAD_DIM
    ncp = kc_ref.shape[0]
    n_cmp = ncp - 1
    nsel = s_len // L_SEL
    rows = gqa * tq
    rc = ATTN_ROW_CHUNK
    per_g = tq // rc
    r_io = lax.broadcasted_iota(jnp.int32, (tq, tq), 0)
    c_io = lax.broadcasted_iota(jnp.int32, (tq, tq), 1)

    @pl.when(i == 0)
    def _():
        kpos = lax.broadcasted_iota(jnp.int32, (s_len, V7X_LANES), 0)
        ksa[:, :hd] = ks_ref[...].astype(BF16)
        ksa[:, hd:] = _pos_lanes(kpos, kpos >> SEL_SHIFT)
        kwa[:, :hd] = kw_ref[...].astype(BF16)
        kwa[:, hd:] = _pos_lanes(kpos)
        bdiag_ref[...] = jnp.where(c_io <= r_io, 0.0, NEG)
        ones = jnp.ones((s_len, V7X_LANES), BF16)
        vsb[:, :hd] = vs_ref[...].astype(BF16)
        vsb[:, hd:] = ones
        vwb[:, :hd] = vw_ref[...].astype(BF16)
        vwb[:, hd:] = ones
        wshape = bwin_ref.shape[1:]
        rw = lax.broadcasted_iota(jnp.int32, wshape, 0)
        cw = lax.broadcasted_iota(jnp.int32, wshape, 1)
        for v in range(bwin_ref.shape[0]):
            dist = min(v * tq, WINDOW) + rw - cw
            bwin_ref[v] = jnp.where((dist >= 0) & (dist <= WINDOW), 0.0, NEG)

    coefs = [_coef_lanes(slopes_ref[h * gqa + g] * LOG2E, (V7X_SUBLANES, V7X_LANES))[0:1] for g in range(gqa)]
    for g in range(gqa):
        rs = slice(g * tq, (g + 1) * tq)
        qa_ref[rs, :hd] = q_ref[:, g * hd:(g + 1) * hd]
        qa_ref[rs, hd:] = jnp.broadcast_to(coefs[g], (tq, V7X_LANES)).astype(BF16)

    def head_rows(c, g):
        if isinstance(c, int):
            return slice(g * tq + c * rc, g * tq + (c + 1) * rc)
        return pl.ds(pl.multiple_of(g * tq + c * rc, rc), rc)

    def tile_rows(c):
        if isinstance(c, int):
            return slice(c * rc, (c + 1) * rc)
        return pl.ds(pl.multiple_of(c * rc, rc), rc)

    s_ref[:, :ncp] = _nt_dot(qa_ref[...], kc_ref[...])
    n_io = lax.broadcasted_iota(jnp.int32, (tq, ncp), 1)
    qpos_c = q0 + lax.broadcasted_iota(jnp.int32, (tq, ncp), 0)
    bcmp_ref[...] = jnp.where((qpos_c >= (n_io + 2) * D_CMP - 1) & (n_io < n_cmp), 0.0, NEG)

    def cmp_chunk(c, carry):
        br = tile_rows(c)
        bias = bcmp_ref[br, :]
        live = bias == 0.0
        tot = jnp.zeros((rc, ncp), F32)
        for g in range(gqa):
            rr = head_rows(c, g)
            s = s_ref[rr, :ncp] + bias
            p = jnp.where(live, jnp.exp2(s - jnp.max(s, axis=1, keepdims=True)), 0.0)
            p = p / jnp.maximum(jnp.sum(p, axis=1, keepdims=True), 1e-30)
            tot = tot + p
            p_ref[rr, :ncp] = p.astype(BF16)
        psum_ref[br, :] = tot
        return carry

    for c in range(per_g):
        cmp_chunk(c, 0)
    oc_ref[...] = jnp.dot(p_ref[:, :ncp], vc_ref[...], preferred_element_type=F32)
    psum = psum_ref[...]

    j_io = lax.broadcasted_iota(jnp.int32, (V7X_LANES, ncp), 0)
    n_io2 = lax.broadcasted_iota(jnp.int32, (V7X_LANES, ncp), 1)
    cover_t = jnp.where((n_io2 * D_CMP < j_io * L_SEL + L_SEL) & (n_io2 * D_CMP + L_CMP > j_io * L_SEL)
                        & (n_io2 < n_cmp) & (j_io < nsel), 1.0, 0.0)
    imp_t = _nt_dot(cover_t, psum, precision=lax.Precision.HIGHEST)
    sel_t = _block_scores_topk_t(imp_t, q0, nsel, tq)
    sel_t = jnp.concatenate([sel_t, jnp.zeros((V7X_LANES - sel_t.shape[0], tq), F32)], axis=0)
    lane_q = lax.broadcasted_iota(jnp.int32, (tq, V7X_LANES), 1)
    unpicked = (sel_t.T - 1.0) * MASK_BIG
    for g in range(gqa):
        side = jnp.where(lane_q < POS_OFF, unpicked, jnp.broadcast_to(coefs[g], (tq, V7X_LANES)))
        qa_ref[g * tq:(g + 1) * tq, hd:] = side.astype(BF16)

    def softmax_passes(src_ref, width, bias_ref, first):
        lanes = [slice(j * V7X_LANES, (j + 1) * V7X_LANES) for j in range(width // V7X_LANES)]

        def scores(c, g):
            vals = [src_ref[head_rows(c, g), ls] for ls in lanes]
            if bias_ref is None:
                return vals
            return [v + bias_ref[tile_rows(c), ls] for v, ls in zip(vals, lanes)]

        for c in range(per_g):
            for g in range(gqa):
                rr = head_rows(c, g)
                m_new = jnp.max(functools.reduce(jnp.maximum, scores(c, g)), axis=1, keepdims=True)
                if first:
                    m_ref[rr, :] = jnp.broadcast_to(m_new, (rc, V7X_LANES))
                else:
                    m_prev = m_ref[rr, :]
                    m_new = jnp.maximum(m_prev, m_new)
                    alpha = jnp.exp2(m_prev - m_new)
                    m_ref[rr, :] = m_new
                    acc_ref[rr, :hd] = alpha * acc_ref[rr, :hd]
                    acc_ref[rr, hd:] = alpha * acc_ref[rr, hd:]
        for c in range(per_g):
            for g in range(gqa):
                rr = head_rows(c, g)
                m_new = m_ref[rr, :]
                for ls, v in zip(lanes, scores(c, g)):
                    p_ref[rr, ls] = jnp.exp2(v - m_new).astype(BF16)

    def sel_tile(kt, first, diagonal):
        k0 = kt * tq if isinstance(kt, int) else pl.multiple_of(kt * tq, tq)
        s_ref[:, :tq] = _nt_dot(qa_ref[...], ksa[pl.ds(k0, tq), :])
        softmax_passes(s_ref, tq, bdiag_ref if diagonal else None, first)
        pv = jnp.dot(p_ref[:, :tq], vsb[pl.ds(k0, tq), :], preferred_element_type=F32)
        if first:
            acc_ref[...] = pv
        else:
            acc_ref[...] += pv

    @pl.when(i == 0)
    def _():
        sel_tile(0, True, True)

    @pl.when(i > 0)
    def _():
        sel_tile(0, True, False)

        def sel_mid(kt, carry):
            sel_tile(kt, False, False)
            return carry

        lax.fori_loop(1, i, sel_mid, 0)
        sel_tile(i, False, True)

    os_ref[...] = acc_ref[:, :hd] / jnp.maximum(acc_ref[:, hd:], 1e-30)

    wk = bwin_ref.shape[2]
    k0w = pl.multiple_of(jnp.maximum(q0 - WINDOW, 0), tq)
    s_ref[...] = _nt_dot(qa_ref[...], kwa[pl.ds(k0w, wk), :])
    softmax_passes(s_ref, wk, bwin_ref.at[jnp.minimum(i, bwin_ref.shape[0] - 1)], True)
    pv = jnp.dot(p_ref[...], vwb[pl.ds(k0w, wk), :], preferred_element_type=F32)
    ow_ref[...] = pv[:, :hd] / jnp.maximum(pv[:, hd:], 1e-30)

    gt = gt_ref[...]
    for g in range(gqa):
        rs = slice(g * tq, (g + 1) * tq)
        o = (gt[:, 3 * g:3 * g + 1] * oc_ref[rs, :] + gt[:, 3 * g + 1:3 * g + 2] * os_ref[rs, :]
             + gt[:, 3 * g + 2:3 * g + 3] * ow_ref[rs, :])
        o_ref[:, g * hd:(g + 1) * hd] = o.astype(BF16)


def _nsa_prompt(q, gates, kc, vc, k_s, v_s, k_w, v_w, slopes, nb, tq):
    t, dq = q.shape
    s_len = t // nb
    gqa = dq // (N_KV * HEAD_DIM)
    nq = s_len // tq
    ncp = kc.shape[2]
    assert WINDOW % tq == 0 and s_len % tq == 0 and s_len // L_SEL <= POS_OFF and s_len >= WINDOW + tq
    assert ncp == V7X_LANES and tq % ATTN_ROW_CHUNK == 0 and s_len < (1 << (POS_LOW_BITS + 8))
    rows = gqa * tq
    hd = HEAD_DIM
    qspec = pl.BlockSpec((tq, gqa * hd), lambda b, h, i, sl: (b * nq + i, h))
    gspec = pl.BlockSpec((tq, V7X_LANES), lambda b, h, i, sl: (b * nq + i, h))

    def cspec(w):
        return pl.BlockSpec((None, None, ncp, w), lambda b, h, i, sl: (b, h, 0, 0))
    kvspec = pl.BlockSpec((s_len, hd), lambda b, h, i, sl: (b, h))
    ka_scr = pltpu.VMEM((s_len, hd + V7X_LANES), BF16)
    v_scr = pltpu.VMEM((s_len, hd + V7X_LANES), BF16)
    row_scr = pltpu.VMEM((rows, hd), F32)
    return pl.pallas_call(
        functools.partial(_nsa_prompt_kernel, tq=tq, gqa=gqa, s_len=s_len),
        grid_spec=pltpu.PrefetchScalarGridSpec(
            num_scalar_prefetch=1, grid=(nb, N_KV, nq),
            in_specs=[qspec, gspec, cspec(hd + V7X_LANES), cspec(hd), kvspec, kvspec, kvspec, kvspec],
            out_specs=qspec,
            scratch_shapes=[ka_scr, v_scr, ka_scr, v_scr,
                            pltpu.VMEM((rows, hd + V7X_LANES), BF16),
                            pltpu.VMEM((rows, WINDOW + tq), F32), pltpu.VMEM((rows, WINDOW + tq), BF16),
                            pltpu.VMEM((tq, tq), F32),
                            pltpu.VMEM((WINDOW // tq + 1, tq, WINDOW + tq), F32),
                            pltpu.VMEM((tq, ncp), F32),
                            pltpu.VMEM((rows, V7X_LANES), F32),
                            pltpu.VMEM((rows, hd + V7X_LANES), F32),
                            row_scr, row_scr, row_scr,
                            pltpu.VMEM((tq, ncp), F32)]),
        out_shape=jax.ShapeDtypeStruct((t, dq), BF16),
        compiler_params=_params(("arbitrary", "arbitrary", "arbitrary"), 48 << 20),
        name="nsa_prompt",
    )(slopes, q, gates, kc, vc, k_s, v_s, k_w, v_w)


def _page_chunks_kernel(pt_ref, *refs, pages):
    k_pages = refs[:pages]
    v_pages = refs[pages:2 * pages]
    wk_ref, wv_ref, fk_ref, sk_ref, fv_ref, sv_ref = refs[2 * pages:]
    per = PAGE_SIZE // D_CMP

    def sums(x, w):
        x4 = x.reshape((per, D_CMP) + x.shape[1:])
        return jnp.sum(x4 * w[None, :D_CMP], axis=1), jnp.sum(x4 * w[None, D_CMP:], axis=1)

    for p in range(pages):
        rs = slice(p * per, (p + 1) * per)
        fk_ref[rs], sk_ref[rs] = sums(k_pages[p][...], wk_ref[...])
        fv_ref[rs], sv_ref[rs] = sums(v_pages[p][...], wv_ref[...])


def _page_chunks(page_table, pool_k, pool_v, wk, wv, pages=16):
    bd, n_pages = page_table.shape
    n_pool, _, n_kv, hd = pool_k.shape
    per = PAGE_SIZE // D_CMP
    nch = n_pages * per
    assert n_pages % pages == 0

    def page_spec(p):
        return pl.BlockSpec((None, PAGE_SIZE, n_kv, hd), lambda b, t, pt: (pt[b, t * pages + p], 0, 0, 0))
    wspec = pl.BlockSpec((L_CMP, n_kv, hd), lambda b, t, pt: (0, 0, 0))
    ospec = pl.BlockSpec((None, pages * per, n_kv, hd), lambda b, t, pt: (b, t, 0, 0))
    osh = jax.ShapeDtypeStruct((bd, nch, n_kv, hd), F32)
    return pl.pallas_call(
        functools.partial(_page_chunks_kernel, pages=pages),
        grid_spec=pltpu.PrefetchScalarGridSpec(
            num_scalar_prefetch=1, grid=(bd, n_pages // pages),
            in_specs=[page_spec(p) for p in range(pages)] * 2 + [wspec, wspec],
            out_specs=[ospec] * 4),
        out_shape=[osh] * 4,
        compiler_params=_params(("arbitrary", "arbitrary"), 52 << 20),
        name="page_chunks",
    )(page_table, *([pool_k] * pages), *([pool_v] * pages), wk, wv)


def _nsa_sample_cmp_kernel(q_ref, sl_ref, fk_ref, sk_ref, fv_ref, sv_ref, cover_ref, oc_ref, idx_ref, ok_ref,
                           *, past, n_sel_blocks):
    n_kv, gqa, _ = q_ref.shape
    nch = fk_ref.shape[0]
    n_cmp = nch - 1
    nj = cover_ref.shape[1]
    q_pos = past
    n_io = lax.broadcasted_iota(jnp.int32, (gqa, nch), 1)
    dist = (q_pos - ((n_io + 2) * D_CMP - 1)).astype(F32)
    ok = (dist >= 0.0) & (n_io < n_cmp)
    j_io = lax.broadcasted_iota(jnp.int32, (gqa, nj), 1)
    cur = q_pos // L_SEL
    valid = (j_io <= cur) & (j_io < n_sel_blocks)
    forced = (j_io == 0) | (j_io == cur) | (j_io == cur - 1)
    lane = lax.broadcasted_iota(jnp.int32, (gqa, V7X_LANES), 1)
    j_f = j_io.astype(F32)
    for h in range(n_kv):
        kc = _blocks_from_sums(fk_ref[:, h, :], sk_ref[:, h, :]).astype(BF16)
        vc = _blocks_from_sums(fv_ref[:, h, :], sv_ref[:, h, :]).astype(BF16)
        s = jnp.where(ok, _nt_dot(q_ref[h], kc) - (sl_ref[h] * LOG2E) * dist, NEG)
        p = jnp.where(ok, jnp.exp2(s - jnp.max(s, axis=1, keepdims=True)), 0.0)
        p = p / jnp.maximum(jnp.sum(p, axis=1, keepdims=True), 1e-30)
        oc_ref[h] = jnp.dot(p.astype(BF16), vc, preferred_element_type=F32)

        psum = jnp.broadcast_to(jnp.sum(p, axis=0, keepdims=True), (gqa, nch))
        imp = jnp.dot(psum, cover_ref[...], precision=lax.Precision.HIGHEST, preferred_element_type=F32)
        score = jnp.where(valid, imp + jnp.where(forced, FORCE_BONUS, 0.0), NEG)
        score = jnp.where(j_io < n_sel_blocks, score, -jnp.inf)
        idx = jnp.zeros((gqa, V7X_LANES), F32)
        okv = jnp.zeros((gqa, V7X_LANES), jnp.int32)
        for t in range(min(N_SEL, n_sel_blocks)):
            mx = jnp.max(score, axis=1, keepdims=True)
            am = jnp.min(jnp.where(score == mx, j_f, float(nj)), axis=1, keepdims=True)
            idx = jnp.where(lane == t, am, idx)
            okv = jnp.where(lane == t, jnp.where(mx > 0.5 * NEG, 1, 0), okv)
            score = jnp.where(j_f == am, -jnp.inf, score)
        idx_ref[h] = idx.astype(jnp.int32)
        ok_ref[h] = okv


def _nsa_sample_cmp(q, slopes_col, sums, past):
    bh, gqa, hd = q.shape
    fk = sums[0]
    bd, nch, n_kv, _ = fk.shape
    n_sel_blocks = past // L_SEL + 1
    nj = -(-n_sel_blocks // V7X_LANES) * V7X_LANES
    n_io = np.arange(nch)[:, None]
    j_io = np.arange(nj)[None, :]
    cover = ((n_io * D_CMP < j_io * L_SEL + L_SEL) & (n_io * D_CMP + L_CMP > j_io * L_SEL)
             & (n_io < nch - 1) & (j_io < n_sel_blocks)).astype(np.float32)
    hspec = pl.BlockSpec((n_kv, gqa, hd), lambda b: (b, 0, 0))
    sspec = pl.BlockSpec((None, nch, n_kv, hd), lambda b: (b, 0, 0, 0))
    ispec = pl.BlockSpec((n_kv, gqa, V7X_LANES), lambda b: (b, 0, 0))
    return pl.pallas_call(
        functools.partial(_nsa_sample_cmp_kernel, past=past, n_sel_blocks=n_sel_blocks),
        grid=(bd,),
        in_specs=[hspec, pl.BlockSpec((n_kv, gqa, 1), lambda b: (0, 0, 0)), sspec, sspec, sspec, sspec,
                  pl.BlockSpec((nch, nj), lambda b: (0, 0))],
        out_specs=[hspec, ispec, ispec],
        out_shape=[jax.ShapeDtypeStruct((bh, gqa, hd), F32),
                   jax.ShapeDtypeStruct((bh, gqa, V7X_LANES), jnp.int32),
                   jax.ShapeDtypeStruct((bh, gqa, V7X_LANES), jnp.int32)],
        compiler_params=_params(("arbitrary",), 48 << 20),
        name="nsa_sample_cmp",
    )(q, slopes_col, *sums, jnp.asarray(cover))


def _nsa_sample_sel_kernel(pt_ref, idx_ref, okb_ref, *refs, past, nsb):
    k_blks = refs[:nsb]
    v_blks = refs[nsb:2 * nsb]
    (q_ref, sl_ref, gt_ref, oc_ref, ksn_ref, vsn_ref, kwc_ref, vwc_ref, kwn_ref, vwn_ref, o_ref) = refs[2 * nsb:]
    b = pl.program_id(0)
    h = pl.program_id(1)
    gqa, hd = q_ref.shape
    base = (b * N_KV + h) * nsb
    nb_past = past // L_SEL
    q_pos = past
    q = q_ref[...]
    slope = sl_ref[...] * LOG2E
    head = lambda ref: ref[:, pl.ds(h, 1), :].reshape(ref.shape[0], hd)

    row_io = lax.broadcasted_iota(jnp.int32, (L_SEL, hd), 0)
    new_k = jnp.where(row_io == 0, jnp.broadcast_to(ksn_ref[...], (L_SEL, hd)), 0.0)
    new_v = jnp.where(row_io == 0, jnp.broadcast_to(vsn_ref[...], (L_SEL, hd)), 0.0)
    lane = lax.broadcasted_iota(jnp.int32, (gqa, nsb * L_SEL), 1)
    kpos = lane & (L_SEL - 1)
    okl = jnp.zeros((gqa, nsb * L_SEL), jnp.int32)
    kg, vg = [], []
    for t in range(nsb):
        bid = idx_ref[base + t]
        is_new = bid >= nb_past
        kg.append(jnp.where(is_new, new_k, head(k_blks[t])).astype(BF16))
        vg.append(jnp.where(is_new, new_v, head(v_blks[t])).astype(BF16))
        in_t = (lane >> SEL_SHIFT) == t
        kpos = kpos + jnp.where(in_t, bid * L_SEL, 0)
        okl = okl + jnp.where(in_t, okb_ref[base + t], 0)
    kg = jnp.concatenate(kg, axis=0)
    vg = jnp.concatenate(vg, axis=0)
    dist = (q_pos - kpos).astype(F32)
    ok = (dist >= 0.0) & (okl > 0)
    s = jnp.where(ok, _nt_dot(q, kg) - slope * dist, NEG)
    p = jnp.where(ok, jnp.exp2(s - jnp.max(s, axis=1, keepdims=True)), 0.0)
    p = p / jnp.maximum(jnp.sum(p, axis=1, keepdims=True), 1e-30)
    o_s = jnp.dot(p.astype(BF16), vg, preferred_element_type=F32)

    wbuf = kwc_ref.shape[0]
    i_io = lax.broadcasted_iota(jnp.int32, (gqa, wbuf), 1)
    kpos_w = past - wbuf + i_io
    dist_w = (q_pos - kpos_w).astype(F32)
    ok_w = (dist_w >= 0.0) & (dist_w <= float(WINDOW)) & (kpos_w >= 0)
    s_w = jnp.where(ok_w, _nt_dot(q, head(kwc_ref).astype(BF16)) - slope * dist_w, NEG)
    kn = jnp.broadcast_to(kwn_ref[...], (V7X_SUBLANES, hd)).astype(BF16)
    s_n = _nt_dot(q, kn)[:, 0:1]
    m = jnp.maximum(jnp.max(s_w, axis=1, keepdims=True), s_n)
    p_w = jnp.where(ok_w, jnp.exp2(s_w - m), 0.0)
    p_n = jnp.exp2(s_n - m)
    den = jnp.maximum(jnp.sum(p_w, axis=1, keepdims=True) + p_n, 1e-30)
    p_w = p_w / den
    p_n = p_n / den
    o_w = (jnp.dot(p_w.astype(BF16), head(vwc_ref).astype(BF16), preferred_element_type=F32)
           + p_n.astype(BF16).astype(F32) * vwn_ref[...].astype(BF16).astype(F32))

    gt = gt_ref[...]
    o_ref[...] = (gt[:, 0:1] * oc_ref[...] + gt[:, 1:2] * o_s + gt[:, 2:3] * o_w).astype(BF16)


def _nsa_sample_sel(page_table, idx_flat, ok_flat, pool_k, pool_v, q, slopes_col, gates, o_c,
                    ks_new, vs_new, cache_kw, cache_vw, kw_new, vw_new, past):
    bh, gqa, hd = q.shape
    bd = bh // N_KV
    nsb = idx_flat.shape[0] // bh
    c = N_KV * hd
    halves = PAGE_SIZE // L_SEL
    nb_past = past // L_SEL
    wbuf = cache_kw.shape[1]

    def blk_spec(t):
        def imap(b, h, pt, idx, okb):
            bid = jnp.minimum(idx[(b * N_KV + h) * nsb + t], nb_past - 1)
            return (pt[b, bid // halves], bid % halves, 0, 0)
        return pl.BlockSpec((None, L_SEL, N_KV, hd), imap)
    hspec = pl.BlockSpec((None, gqa, hd), lambda b, h, *_: (b * N_KV + h, 0, 0))
    nspec = pl.BlockSpec((None, 1, hd), lambda b, h, *_: (b, 0, h))
    wspec = pl.BlockSpec((None, wbuf, N_KV, hd), lambda b, h, *_: (b, 0, 0, 0))
    return pl.pallas_call(
        functools.partial(_nsa_sample_sel_kernel, past=past, nsb=nsb),
        grid_spec=pltpu.PrefetchScalarGridSpec(
            num_scalar_prefetch=3, grid=(bd, N_KV),
            in_specs=[blk_spec(t) for t in range(nsb)] * 2
            + [hspec, pl.BlockSpec((None, gqa, 1), lambda b, h, *_: (h, 0, 0)),
               pl.BlockSpec((None, gqa, 3), lambda b, h, *_: (b * N_KV + h, 0, 0)), hspec,
               nspec, nspec, wspec, wspec, nspec, nspec],
            out_specs=hspec),
        out_shape=jax.ShapeDtypeStruct((bh, gqa, hd), BF16),
        compiler_params=_params(("arbitrary", "arbitrary"), 40 << 20),
        name="nsa_sample_sel",
    )(page_table, idx_flat, ok_flat, *([pool_k] * nsb), *([pool_v] * nsb), q, slopes_col, gates, o_c,
      ks_new.reshape(bd, 1, c), vs_new.reshape(bd, 1, c), cache_kw, cache_vw,
      kw_new.reshape(bd, 1, c), vw_new.reshape(bd, 1, c))


TM_MM = 1024
TM_ROWS = 256
TQ = 256


def kernel(x_prompt, x_sample, c_prompt, c_sample, page_table, cache_k_cmp, cache_v_cmp, cache_k_sel, cache_v_sel, cache_k_win, cache_v_win, state_h, state_conv, w_ada, b_ada, ln_g, ln_b, a_w_in, a_conv_w, a_conv_b, a_w_r, a_b_r, a_w_i, a_b_i, a_lambda, a_w_out, w_kv, w_cmp_k, w_cmp_v, b_w_qg, b_w_o, f_w_up, f_w_down):
    nb, s_len, d = x_prompt.shape
    bd, sd, _ = x_sample.shape
    assert sd == 1
    depth = w_ada.shape[0]
    n_a = a_w_in.shape[0]
    assert depth - n_a == 1 or depth == n_a, "one shared-KV NSA layer stack"
    d_ff = f_w_down.shape[1]
    lru = a_w_out.shape[1]
    n_heads = d // HEAD_DIM
    gqa = n_heads // N_KV
    ckv = N_KV * HEAD_DIM
    past = page_table.shape[1] * PAGE_SIZE
    alpha = (2.0 * depth) ** 0.25
    t = nb * s_len
    tm_mm = min(TM_MM, s_len)
    tm_rows = min(TM_ROWS, s_len)
    tn = 512

    rows = -(-(nb + bd) // V7X_SUBLANES) * V7X_SUBLANES
    c_all = jnp.concatenate([c_prompt, c_sample, jnp.zeros((rows - nb - bd, d), F32)], axis=0)
    mods = {0: _ada(c_all, w_ada, b_ada, 0)}

    def mod_p(l, k):
        return mods[l][:nb, k * d:(k + 1) * d].reshape(nb, 1, d)

    def mod_s(l, k):
        return mods[l][nb:nb + bd, k * d:(k + 1) * d].reshape(1, bd, d)

    slopes = jnp.exp2(-8.0 * jnp.arange(1, n_heads + 1, dtype=F32) / n_heads)
    slopes_col = slopes.reshape(N_KV, gqa, 1)

    xp = x_prompt.reshape(t, d)
    xs = x_sample.reshape(bd, d)
    up = _modulate(xp, mod_p(0, 0), mod_p(0, 1), tm_rows)
    us = _modulate(xs, mod_s(0, 0), mod_s(0, 1), bd)

    w_down_bf = f_w_down.astype(BF16)

    def ffn(x_p, x_s, u_p, u_s, l, nxt):
        hmid = _matmul(u_p, u_s, f_w_up, d_ff, layer=l, tm=min(2 * tm_mm, s_len), tn=256, out_dtype=BF16,
                       epilogue=_swiglu_epilogue, col_off=0, col_off2=d_ff, name="ffn_up")
        ys = _matmul(hmid[0], hmid[1], w_down_bf, d, layer=l, tm=min(tm_mm, 512), tn=tn, out_dtype=F32,
                     ada=None if nxt is None else (c_all, w_ada, b_ada, nxt[0]), name="ffn_down")
        if nxt is not None:
            mods[nxt[0]] = ys[2]
        outs = []
        for x, y, tm_r, mod in ((x_p, ys[0], tm_rows, mod_p), (x_s, ys[1], bd, mod_s)):
            if nxt is None:
                outs.append(_ln_mod(x, y, mod(l, 5), ln_g[l, 1], ln_b[l, 1], alpha, tm_r))
            else:
                outs.append(_ln_mod(x, y, mod(l, 5), ln_g[l, 1], ln_b[l, 1], alpha, tm_r,
                                    shift=mod(nxt[0], 0), scale=mod(nxt[0], 1), want_xb=nxt[1]))
        return outs

    h_p, h_s, cb_p, cb_s = [], [], [], []
    kv_p = kv_s = None
    for l in range(depth):
        if l < n_a:
            wr = a_w_r[l].astype(BF16)
            wi = a_w_i[l].astype(BF16)
            ggx_p, ggx_s = _matmul(up, us, a_w_in, lru, layer=l, tm=tm_mm, tn=tn, out_dtype=BF16,
                                   epilogue=_gelu_tanh, name="lru_in_g")
            rx_p, rx_s = _matmul(up, us, a_w_in, lru, layer=l, tm=tm_mm, tn=tn, out_dtype=F32, col_off=lru,
                                 name="lru_in_r")
            yin_p, hl_p, ct_p = _rglru_prompt(
                rx_p, ggx_p, jnp.zeros((nb, CONV_W - 1, lru), F32), jnp.zeros((nb, lru), F32),
                a_conv_w[l], a_conv_b[l], wr, a_b_r[l], wi, a_b_i[l], a_lambda[l], nb, min(TM_ROWS, s_len))
            yin_s, hl_s, ct_s = _rglru_step(rx_s, ggx_s, jnp.swapaxes(state_conv[l], 0, 1), state_h[l],
                                            a_conv_w[l], a_conv_b[l], wr, a_b_r[l], wi, a_b_i[l], a_lambda[l])
            h_p.append(hl_p.reshape(nb, lru))
            h_s.append(hl_s)
            cb_p.append(ct_p)
            cb_s.append(jnp.swapaxes(ct_s, 0, 1))
            y_p, y_s = _matmul(yin_p, yin_s, a_w_out, d, layer=l, tm=tm_mm, tn=tn, out_dtype=F32, name="lru_out")
        else:
            j = l - n_a
            w_g = b_w_qg[j][:, n_heads * HEAD_DIM:].reshape(d, N_KV, 3 * gqa)
            w_g = jnp.pad(w_g, ((0, 0), (0, 0), (0, V7X_LANES - 3 * gqa))).reshape(d, N_KV * V7X_LANES)
            w_q = b_w_qg[j, :, :n_heads * HEAD_DIM].astype(BF16)
            q_p, q_s = _matmul(up, us, w_q, d, tm=tm_mm, tn=tn, out_dtype=BF16,
                               epilogue=lambda a: a * EXP2_SCALE, name="nsa_q")
            g_p, g_s = _matmul(up, us, w_g, N_KV * V7X_LANES, tm=tm_mm, tn=N_KV * V7X_LANES, out_dtype=F32,
                               epilogue=jax.nn.sigmoid, name="nsa_gate")
            o_p = _nsa_prompt(q_p, g_p, ctx_p[0], ctx_p[1], kv_p[2], kv_p[3], kv_p[4], kv_p[5], slopes, nb,
                              min(TQ, s_len))
            q_s = q_s.reshape(bd * N_KV, gqa, HEAD_DIM)
            g_s = g_s.reshape(bd * N_KV, V7X_LANES)[:, :3 * gqa].reshape(bd * N_KV, gqa, 3)
            o_c, idx, okb = _nsa_sample_cmp(q_s, slopes_col, ctx_s, past)
            nsb = min(N_SEL, past // L_SEL + 1)
            o_s = _nsa_sample_sel(page_table, idx[:, 0, :nsb].reshape(-1), okb[:, 0, :nsb].reshape(-1),
                                  cache_k_sel, cache_v_sel, q_s, slopes_col, g_s, o_c,
                                  kv_s[2], kv_s[3], cache_k_win, cache_v_win, kv_s[4], kv_s[5], past)
            y_p, y_s = _matmul(o_p, o_s.reshape(bd, d), b_w_o, d, layer=j, tm=tm_mm, tn=tn, out_dtype=F32,
                               name="nsa_out")

        xp, up = _ln_mod(xp, y_p, mod_p(l, 2), ln_g[l, 0], ln_b[l, 0], alpha, tm_rows,
                         shift=mod_p(l, 3), scale=mod_p(l, 4))
        xs, us = _ln_mod(xs, y_s, mod_s(l, 2), ln_g[l, 0], ln_b[l, 0], alpha, bd,
                         shift=mod_s(l, 3), scale=mod_s(l, 4))
        last = l == depth - 1
        res_p, res_s = ffn(xp, xs, up, us, l, None if last else (l + 1, l == n_a - 1))
        if last:
            xp, xs = res_p[0], res_s[0]
        elif l == n_a - 1:
            xp, up, xbp = res_p
            xs, us, xbs = res_s
            kv = [_matmul(xbp, xbs, w_kv, ckv, tm=tm_mm, tn=ckv, out_dtype=F32, col_off=jj * ckv,
                          heads=(N_KV, HEAD_DIM), name="kv_proj") for jj in range(6)]
            kv_p, kv_s, kv_p4, kv_s4 = zip(*kv)
            ctx_p = _cmp_blocks_prompt(kv_p[0], kv_p[1], w_cmp_k, w_cmp_v, nb)
            ctx_s = _page_chunks(page_table, cache_k_cmp, cache_v_cmp, w_cmp_k, w_cmp_v)
        else:
            xp, up = res_p
            xs, us = res_s

    wwin = min(WINDOW, s_len)
    shp_p = (nb, s_len, N_KV, HEAD_DIM)
    shp_s = (bd, sd, N_KV, HEAD_DIM)
    kvp = [a.reshape(shp_p) for a in kv_p4]
    kvs = [a.reshape(shp_s) for a in kv_s4]
    wbuf = cache_k_win.shape[1]
    keep = wbuf - sd
    k_win_s = jnp.concatenate([cache_k_win[:, wbuf - keep:], kvs[4]], axis=1)
    v_win_s = jnp.concatenate([cache_v_win[:, wbuf - keep:], kvs[5]], axis=1)
    return (xp.reshape(nb, s_len, d), xs.reshape(bd, sd, d),
            kvp[0], kvs[0], kvp[1], kvs[1], kvp[2], kvs[2], kvp[3], kvs[3],
            kvp[4][:, -wwin:], k_win_s, kvp[5][:, -wwin:], v_win_s,
            jnp.stack(h_p), jnp.stack(h_s), jnp.stack(cb_p), jnp.stack(cb_s))
```

```python
import functools

import numpy as np
import jax
import jax.numpy as jnp
from jax import lax
from jax.experimental import pallas as pl
from jax.experimental.pallas import tpu as pltpu

F32 = jnp.float32
BF16 = jnp.bfloat16

HEAD_DIM = 128
N_KV = 4
L_CMP = 32
D_CMP = 16
L_SEL = 64
N_SEL = 16
WINDOW = 512
PAGE_SIZE = 128
CONV_W = 4
LRU_BLOCKS = 16
LRU_C = 8.0
NEG = -1e30
FORCE_BONUS = 1e3
ATTN_SCALE = HEAD_DIM ** -0.5
LN_EPS = 1e-5
EXP2_SCALE = ATTN_SCALE * 1.4426950408889634
ATTN_ROW_CHUNK = 32
SEL_SHIFT = L_SEL.bit_length() - 1
assert 1 << SEL_SHIFT == L_SEL

V7X_VMEM_BYTES = 64 * 1024 * 1024
V7X_LANES = 128
V7X_SUBLANES = 8
VMEM_CAP = V7X_VMEM_BYTES - 3 * 1024 * 1024


def _params(sem, vmem_bytes):
    return pltpu.CompilerParams(dimension_semantics=sem,
                                vmem_limit_bytes=int(min(max(vmem_bytes, 16 << 20), VMEM_CAP)))


def _nt_dot(a, b, **kw):
    return lax.dot_general(a, b, (((1,), (1,)), ((), ())), preferred_element_type=F32, **kw)


def _ada_block(c_ref, w_ref, b_ref, o_ref):
    c = c_ref[...]
    s = (c * jax.nn.sigmoid(c)).astype(BF16)
    o_ref[...] = jnp.dot(s, w_ref[...].astype(BF16), preferred_element_type=F32) + b_ref[...]


def _ada(c_all, w_ada, b_ada, layer, tn=1024):
    depth, d, n6 = w_ada.shape
    r = c_all.shape[0]
    return pl.pallas_call(
        _ada_block,
        grid=(n6 // tn,),
        in_specs=[pl.BlockSpec((r, d), lambda n: (0, 0)),
                  pl.BlockSpec((None, d, tn), lambda n: (layer, 0, n)),
                  pl.BlockSpec((None, 1, tn), lambda n: (layer, 0, n))],
        out_specs=pl.BlockSpec((r, tn), lambda n: (0, n)),
        out_shape=jax.ShapeDtypeStruct((r, n6), F32),
        compiler_params=_params(("arbitrary",), 2 * d * tn * 4 + d * tn * 2 + (8 << 20)),
        name="ada",
    )(c_all, w_ada, b_ada.reshape(depth, 1, n6))


def _modulate_kernel(x_ref, sh_ref, sc_ref, u_ref):
    u_ref[...] = (x_ref[...] * (1.0 + sc_ref[...]) + sh_ref[...]).astype(u_ref.dtype)


def _row_specs(t, d, tm, nb):
    tiles_per_batch = (t // nb) // tm
    xspec = pl.BlockSpec((tm, d), lambda i: (i, 0))

    def vspec(r):
        return pl.BlockSpec((None, r, d), lambda i: (i // tiles_per_batch, 0, 0))
    return xspec, vspec


def _modulate(x, shift, scale, tm):
    t, d = x.shape
    nb, r, _ = shift.shape
    xspec, vspec = _row_specs(t, d, tm, nb)
    return pl.pallas_call(
        _modulate_kernel,
        grid=(t // tm,),
        in_specs=[xspec, vspec(r), vspec(r)],
        out_specs=xspec,
        out_shape=jax.ShapeDtypeStruct((t, d), BF16),
        compiler_params=_params(("arbitrary",), 6 * tm * d * 4),
        name="modulate",
    )(x, shift, scale)


def _ln_mod_kernel(*refs, alpha, want_u, want_xb):
    x_ref, y_ref, gate_ref, g_ref, b_ref = refs[:5]
    pos = 5
    if want_u:
        sh_ref, sc_ref = refs[pos:pos + 2]
        pos += 2
    xo_ref = refs[pos]
    pos += 1
    v = alpha * x_ref[...] + (1.0 + gate_ref[...]) * y_ref[...]
    mu = jnp.mean(v, axis=-1, keepdims=True)
    dlt = v - mu
    var = jnp.mean(dlt * dlt, axis=-1, keepdims=True)
    xn = dlt * lax.rsqrt(var + LN_EPS) * g_ref[...] + b_ref[...]
    xo_ref[...] = xn
    if want_u:
        refs[pos][...] = (xn * (1.0 + sc_ref[...]) + sh_ref[...]).astype(BF16)
        pos += 1
    if want_xb:
        refs[pos][...] = xn.astype(BF16)


def _ln_mod(x, y, gate, ln_g, ln_b, alpha, tm, shift=None, scale=None, want_xb=False):
    t, d = x.shape
    nb, r, _ = gate.shape
    xspec, vspec = _row_specs(t, d, tm, nb)
    pspec = pl.BlockSpec((1, d), lambda i: (0, 0))
    want_u = shift is not None
    ins = [x, y, gate, ln_g.reshape(1, d), ln_b.reshape(1, d)]
    in_specs = [xspec, xspec, vspec(r), pspec, pspec]
    out_shape = [jax.ShapeDtypeStruct((t, d), F32)]
    out_specs = [xspec]
    if want_u:
        ins += [shift, scale]
        in_specs += [vspec(r), vspec(r)]
        out_shape.append(jax.ShapeDtypeStruct((t, d), BF16))
        out_specs.append(xspec)
    if want_xb:
        out_shape.append(jax.ShapeDtypeStruct((t, d), BF16))
        out_specs.append(xspec)
    return pl.pallas_call(
        functools.partial(_ln_mod_kernel, alpha=alpha, want_u=want_u, want_xb=want_xb),
        grid=(t // tm,),
        in_specs=in_specs, out_specs=out_specs, out_shape=out_shape,
        compiler_params=_params(("arbitrary",), 12 * tm * d * 4),
        name="ln_mod",
    )(*ins)


def _gelu_tanh(x):
    return jax.nn.gelu(x, approximate=True)


def _mm_kernel(*refs, n_w, cast_w, epilogue, heads, ada_blocks):
    x_ref, xs_ref = refs[:2]
    w_refs = refs[2:2 + n_w]
    n_in = 2 + n_w + (3 if ada_blocks else 0)
    n_out = (4 if heads else 2) + (1 if ada_blocks else 0)
    outs = refs[n_in:n_in + n_out]
    scr = refs[n_in + n_out:]
    first_row_tile = pl.program_id(1) == 0
    if ada_blocks:
        step = pl.program_id(0) * pl.num_programs(1) + pl.program_id(1)
        pl.when(step < ada_blocks)(lambda: _ada_block(*refs[2 + n_w:n_in], outs[-1]))
    if cast_w:
        @pl.when(first_row_tile)
        def _():
            for w_ref, s_ref in zip(w_refs, scr):
                s_ref[...] = w_ref[...].astype(BF16)
        w_refs = scr

    def product(rows_ref, out_ref, heads_ref):
        rows = rows_ref[...]
        accs = [jnp.dot(rows, w_ref[...], preferred_element_type=F32) for w_ref in w_refs]
        val = epilogue(*accs).astype(out_ref.dtype)
        out_ref[...] = val
        if heads_ref is not None:
            hd = heads_ref.shape[-1]
            for h in range(heads_ref.shape[-2]):
                heads_ref[:, h, :] = val[:, h * hd:(h + 1) * hd]

    product(x_ref, outs[0], outs[2] if heads else None)
    pl.when(first_row_tile)(lambda: product(xs_ref, outs[1], outs[3] if heads else None))


def _matmul(x, xs, w, n_out, *, tm, tn, out_dtype, layer=None, epilogue=None, col_off=0, col_off2=None,
            heads=None, ada=None, name="mm"):
    m, k = x.shape
    ms = xs.shape[0]
    assert m % tm == 0 and n_out % tn == 0 and col_off % tn == 0
    assert (w.ndim == 3) == (layer is not None) and w.shape[-2] == k and xs.shape[1] == k
    offs = [col_off // tn] + ([] if col_off2 is None else [col_off2 // tn])
    n_w = len(offs)
    if epilogue is None:
        epilogue = lambda a: a
    cast_w = w.dtype != BF16
    scratch = [pltpu.VMEM((k, tn), BF16) for _ in range(n_w)] if cast_w else []
    w_bytes = k * tn * w.dtype.itemsize
    vmem = (2 * tm * k * 2 + n_w * (2 * w_bytes + len(scratch) * k * tn * 2)
            + 2 * tm * tn * 4 * (n_w + 1) + (4 << 20))
    if layer is None:
        w_specs = [pl.BlockSpec((k, tn), functools.partial(lambda n, i, o: (0, n + o), o=o)) for o in offs]
    else:
        w_specs = [pl.BlockSpec((None, k, tn), functools.partial(lambda n, i, o: (layer, 0, n + o), o=o))
                   for o in offs]
    out_specs = [pl.BlockSpec((tm, tn), lambda n, i: (i, n)), pl.BlockSpec((ms, tn), lambda n, i: (0, n))]
    out_shape = [jax.ShapeDtypeStruct((m, n_out), out_dtype), jax.ShapeDtypeStruct((ms, n_out), out_dtype)]
    if heads is not None:
        assert n_out == tn == heads[0] * heads[1]
        out_specs += [pl.BlockSpec((tm,) + heads, lambda n, i: (i, 0, 0)),
                      pl.BlockSpec((ms,) + heads, lambda n, i: (0, 0, 0))]
        out_shape += [jax.ShapeDtypeStruct((m,) + heads, out_dtype), jax.ShapeDtypeStruct((ms,) + heads, out_dtype)]
        vmem += 4 * tm * tn * 4
    m_tiles = m // tm
    extra_in, extra_specs, ada_blocks = [], [], 0
    if ada is not None:
        c_all, w_ada, b_ada, ada_layer = ada
        depth, d, n6 = w_ada.shape
        r = c_all.shape[0]
        steps = (n_out // tn) * m_tiles
        atn = next(c for c in range(V7X_LANES, n6 + 1, V7X_LANES) if n6 % c == 0 and n6 // c <= steps)
        ada_blocks = n6 // atn

        def blk(n, i):
            return jnp.minimum(n * m_tiles + i, ada_blocks - 1)
        extra_in = [c_all, w_ada, b_ada.reshape(depth, 1, n6)]
        extra_specs = [pl.BlockSpec((r, d), lambda n, i: (0, 0)),
                       pl.BlockSpec((None, d, atn), lambda n, i: (ada_layer, 0, blk(n, i))),
                       pl.BlockSpec((None, 1, atn), lambda n, i: (ada_layer, 0, blk(n, i)))]
        out_specs.append(pl.BlockSpec((r, atn), lambda n, i: (0, blk(n, i))))
        out_shape.append(jax.ShapeDtypeStruct((r, n6), F32))
        vmem += 2 * d * atn * 4 + d * atn * 2
    return pl.pallas_call(
        functools.partial(_mm_kernel, n_w=n_w, cast_w=cast_w, epilogue=epilogue, heads=heads is not None,
                          ada_blocks=ada_blocks),
        grid=(n_out // tn, m_tiles),
        in_specs=[pl.BlockSpec((tm, k), lambda n, i: (i, 0)), pl.BlockSpec((ms, k), lambda n, i: (0, 0))]
        + w_specs + extra_specs,
        out_specs=out_specs, out_shape=out_shape,
        scratch_shapes=scratch,
        compiler_params=_params(("arbitrary", "arbitrary"), vmem),
        name=name,
    )(x, xs, *([w] * n_w), *extra_in)


def _swiglu_epilogue(g, v):
    return g * jax.nn.sigmoid(g) * v


def _cast_rider(w_stack, layer, grid_shape):
    _, k, n = w_stack.shape
    n_blocks = n // V7X_LANES
    steps = int(np.prod(grid_shape))
    assert n % V7X_LANES == 0 and n_blocks <= steps

    def flat_step(idx):
        s = idx[0]
        for i, extent in zip(idx[1:], grid_shape[1:]):
            s = s * extent + i
        return s

    def blk(*idx):
        return jnp.minimum(flat_step(idx[:len(grid_shape)]), n_blocks - 1)

    def body(in_ref, out_ref):
        step = flat_step([pl.program_id(a) for a in range(len(grid_shape))])

        @pl.when(step < n_blocks)
        def _():
            out_ref[...] = in_ref[...].astype(BF16)

    in_spec = pl.BlockSpec((None, k, V7X_LANES), lambda *idx: (layer, 0, blk(*idx)))
    out_spec = pl.BlockSpec((k, V7X_LANES), lambda *idx: (0, blk(*idx)))
    return in_spec, out_spec, jax.ShapeDtypeStruct((k, n), BF16), body, 2 * k * V7X_LANES * 6


def _softplus_neg(lam):
    return jnp.maximum(-lam, 0.0) + jnp.log1p(jnp.exp(-jnp.abs(lam)))


def _lru_gates(xc, wr_ref, br_ref, wi_ref, bi_ref, sp, a_ref, b_ref, nblk, bw):
    def put(ref, k, val):
        ref[:, k * bw:(k + 1) * bw] = val

    nsp = (-LRU_C) * sp
    xcb = xc.astype(BF16)
    for k in range(nblk):
        cs = slice(k * bw, (k + 1) * bw)
        xb = xcb[:, cs]
        r = jax.nn.sigmoid(jnp.dot(xb, wr_ref[k], preferred_element_type=F32) + br_ref[:, cs])
        ig = jax.nn.sigmoid(jnp.dot(xb, wi_ref[k], preferred_element_type=F32) + bi_ref[:, cs])
        log_a = r * nsp[:, cs]
        put(a_ref, k, jnp.exp(log_a))
        th = jnp.tanh(log_a)
        put(b_ref, k, jnp.sqrt(-2.0 * th / (1.0 - th)) * (ig * xc[:, cs]))


SCAN_TILES = 8


def _rglru_kernel(rx_ref, ggx_ref, cb0_ref, h0_ref, cw_ref, cbias_ref, wr_ref, br_ref, wi_ref, bi_ref,
                  lam_ref, wcast_ref, y_ref, hlast_ref, ctail_ref, wcast_out_ref, xpad_ref, a_ref, b_ref,
                  hcar_ref, *, tm, nblk, bw, cast_body):
    s = pl.program_id(1)
    width = nblk * bw
    pad = V7X_SUBLANES
    tail = CONV_W - 1
    cast_body(wcast_ref, wcast_out_ref)

    @pl.when(s == 0)
    def _():
        xpad_ref[pad - tail:pad, :] = cb0_ref[...]
        hcar_ref[...] = h0_ref[...]

    rx = rx_ref[...]
    xpad_ref[pad:pad + tm, :] = rx
    cw = cw_ref[...]
    xc = cbias_ref[...] + rx * cw[tail:tail + 1]
    for j in range(tail):
        xc = xc + xpad_ref[pad - tail + j:pad - tail + j + tm, :] * cw[j:j + 1]
    xpad_ref[pad - tail:pad, :] = rx[tm - tail:tm, :]

    _lru_gates(xc, wr_ref, br_ref, wi_ref, bi_ref, _softplus_neg(lam_ref[...]), a_ref, b_ref, nblk, bw)

    scan_lanes = min(SCAN_TILES * V7X_LANES, width)
    row = lax.broadcasted_iota(jnp.int32, (V7X_SUBLANES, scan_lanes), 0)
    for c in range(width // scan_lanes):
        cs = slice(c * scan_lanes, (c + 1) * scan_lanes)

        def group(gi, h, cs=cs):
            r0 = pl.multiple_of(gi * V7X_SUBLANES, V7X_SUBLANES)
            a = a_ref[pl.ds(r0, V7X_SUBLANES), cs]
            b = b_ref[pl.ds(r0, V7X_SUBLANES), cs]
            for sft in (1, 2, 4):
                keep = row >= sft
                a_s = jnp.where(keep, pltpu.roll(a, sft, 0), 1.0)
                b_s = jnp.where(keep, pltpu.roll(b, sft, 0), 0.0)
                b = a * b_s + b
                a = a * a_s
            hs = a * h + b
            b_ref[pl.ds(r0, V7X_SUBLANES), cs] = hs
            return jnp.broadcast_to(hs[V7X_SUBLANES - 1:V7X_SUBLANES, :], hs.shape)

        h_in = jnp.broadcast_to(hcar_ref[:, cs], (V7X_SUBLANES, scan_lanes))
        h_out = lax.fori_loop(0, tm // V7X_SUBLANES, group, h_in)
        hcar_ref[:, cs] = h_out[0:1, :]

    y_ref[...] = (ggx_ref[...].astype(F32) * b_ref[...]).astype(BF16)

    @pl.when(s == pl.num_programs(1) - 1)
    def _():
        hlast_ref[...] = hcar_ref[...]
        ctail_ref[...] = rx[tm - tail:tm, :]


def _rglru_prompt(rx, ggx, conv_buf, h0, cw, cbias, wr, br, wi, bi, lam, nb, tm, w_cast, cast_layer):
    t, width = rx.shape
    s_len = t // nb
    nblk, bw, _ = wr.shape
    tail = CONV_W - 1
    ns = s_len // tm
    row = lambda a: a.reshape(1, width)
    xspec = pl.BlockSpec((tm, width), lambda b, s: (b * ns + s, 0))
    pspec = pl.BlockSpec((1, width), lambda b, s: (0, 0))
    wspec = pl.BlockSpec((nblk, bw, bw), lambda b, s: (0, 0, 0))
    cast_in, cast_out, cast_shape, cast_body, cast_vmem = _cast_rider(w_cast, cast_layer, (nb, ns))
    vmem = (2 * tm * width * (4 + 2 + 2) + 3 * (tm + 8) * width * 4 + 4 * nblk * bw * bw * 2 + (6 << 20)
            + cast_vmem)
    return pl.pallas_call(
        functools.partial(_rglru_kernel, tm=tm, nblk=nblk, bw=bw, cast_body=cast_body),
        grid=(nb, ns),
        in_specs=[xspec, xspec,
                  pl.BlockSpec((None, tail, width), lambda b, s: (b, 0, 0)),
                  pl.BlockSpec((None, 1, width), lambda b, s: (b, 0, 0)),
                  pl.BlockSpec((CONV_W, width), lambda b, s: (0, 0)), pspec,
                  wspec, pspec, wspec, pspec, pspec, cast_in],
        out_specs=[xspec,
                   pl.BlockSpec((None, 1, width), lambda b, s: (b, 0, 0)),
                   pl.BlockSpec((None, tail, width), lambda b, s: (b, 0, 0)), cast_out],
        out_shape=[jax.ShapeDtypeStruct((t, width), BF16),
                   jax.ShapeDtypeStruct((nb, 1, width), F32),
                   jax.ShapeDtypeStruct((nb, tail, width), F32), cast_shape],
        scratch_shapes=[pltpu.VMEM((tm + V7X_SUBLANES, width), F32), pltpu.VMEM((tm, width), F32),
                        pltpu.VMEM((tm, width), F32), pltpu.VMEM((1, width), F32)],
        compiler_params=_params(("arbitrary", "arbitrary"), vmem),
        name="rglru",
    )(rx, ggx, conv_buf, h0.reshape(nb, 1, width), cw, row(cbias), wr, row(br), wi, row(bi), row(lam), w_cast)


def _rglru_step_kernel(rx_ref, ggx_ref, cb_ref, h0_ref, cw_ref, cbias_ref, wr_ref, br_ref, wi_ref, bi_ref,
                       lam_ref, y_ref, h_ref, cnew_ref, a_ref, b_ref, *, nblk, bw):
    tail = CONV_W - 1
    rx = rx_ref[...]
    cw = cw_ref[...]
    xc = cbias_ref[...] + rx * cw[tail:tail + 1]
    for j in range(tail):
        xc = xc + cb_ref[j] * cw[j:j + 1]
    _lru_gates(xc, wr_ref, br_ref, wi_ref, bi_ref, _softplus_neg(lam_ref[...]), a_ref, b_ref, nblk, bw)
    h = a_ref[...] * h0_ref[...] + b_ref[...]
    h_ref[...] = h
    y_ref[...] = (ggx_ref[...].astype(F32) * h).astype(BF16)
    for j in range(tail - 1):
        cnew_ref[j] = cb_ref[j + 1]
    cnew_ref[tail - 1] = rx


def _rglru_step(rx, ggx, conv_t, h0, cw, cbias, wr, br, wi, bi, lam):
    bd, width = rx.shape
    nblk, bw, _ = wr.shape
    row = lambda a: a.reshape(1, width)
    return pl.pallas_call(
        functools.partial(_rglru_step_kernel, nblk=nblk, bw=bw),
        out_shape=[jax.ShapeDtypeStruct((bd, width), BF16), jax.ShapeDtypeStruct((bd, width), F32),
                   jax.ShapeDtypeStruct(conv_t.shape, F32)],
        scratch_shapes=[pltpu.VMEM((bd, width), F32), pltpu.VMEM((bd, width), F32)],
        compiler_params=pltpu.CompilerParams(vmem_limit_bytes=32 << 20),
        name="rglru_step",
    )(rx, ggx, conv_t, h0, cw, row(cbias), wr, row(br), wi, row(bi), row(lam))


def _chunk_sums(x, w):
    n = x.shape[0] // D_CMP
    x3 = x.reshape(n, D_CMP, x.shape[1])
    first = jnp.sum(x3 * w[None, :D_CMP, :], axis=1)
    second = jnp.sum(x3 * w[None, D_CMP:, :], axis=1)
    return first, second


def _blocks_from_sums(first, second):
    n = first.shape[0]
    nxt = pltpu.roll(second, n - 1, 0)
    rows = lax.broadcasted_iota(jnp.int32, first.shape, 0)
    return jnp.where(rows < n - 1, first + nxt, 0.0)


COEF_PIECES = 3
POS_LOW_BITS = 7
POS_OFF = 64
MASK_BIG = 2.0 ** 100
LOG2E = 1.4426950408889634


def _pos_lanes(pos, block_of_key=None):
    lane = lax.broadcasted_iota(jnp.int32, pos.shape, 1)
    hi = ((pos >> POS_LOW_BITS) << POS_LOW_BITS).astype(F32)
    lo = (pos & ((1 << POS_LOW_BITS) - 1)).astype(F32)
    k = lane - POS_OFF
    out = jnp.where((k >= 0) & (k < COEF_PIECES), hi, jnp.where((k >= COEF_PIECES) & (k < 2 * COEF_PIECES), lo, 0.0))
    if block_of_key is not None:
        out = jnp.where((lane < POS_OFF) & (lane == block_of_key), 1.0, out)
    return out.astype(BF16)


def _coef_lanes(coef, shape):
    v = jnp.full(shape, coef, F32)
    c1 = v.astype(BF16).astype(F32)
    c2 = (v - c1).astype(BF16).astype(F32)
    c3 = (v - c1 - c2).astype(BF16).astype(F32)
    k = lax.broadcasted_iota(jnp.int32, shape, 1) - POS_OFF
    live = (k >= 0) & (k < 2 * COEF_PIECES)
    k = jnp.where(k >= COEF_PIECES, k - COEF_PIECES, k)
    return jnp.where(live, jnp.where(k == 0, c1, jnp.where(k == 1, c2, c3)), 0.0)


def _cmp_blocks_kernel(k_ref, v_ref, wk_ref, wv_ref, kc_ref, vc_ref):
    hd = k_ref.shape[1]
    nch = kc_ref.shape[0]
    kc_ref[:, :hd] = _blocks_from_sums(*_chunk_sums(k_ref[...], wk_ref[...])).astype(BF16)
    ends = (lax.broadcasted_iota(jnp.int32, (nch, V7X_LANES), 0) + 2) * D_CMP - 1
    kc_ref[:, hd:] = _pos_lanes(ends)
    vc_ref[...] = _blocks_from_sums(*_chunk_sums(v_ref[...], wv_ref[...])).astype(BF16)


def _cmp_blocks_prompt(k_c, v_c, wk, wv, nb):
    t, c = k_c.shape
    s_len = t // nb
    nch = s_len // D_CMP
    kvspec = pl.BlockSpec((s_len, HEAD_DIM), lambda b, h: (b, h))
    wspec = pl.BlockSpec((L_CMP, HEAD_DIM), lambda b, h: (0, h))

    def ospec(w):
        return pl.BlockSpec((None, None, nch, w), lambda b, h: (b, h, 0, 0))
    return pl.pallas_call(
        _cmp_blocks_kernel, grid=(nb, N_KV),
        in_specs=[kvspec, kvspec, wspec, wspec],
        out_specs=[ospec(HEAD_DIM + V7X_LANES), ospec(HEAD_DIM)],
        out_shape=[jax.ShapeDtypeStruct((nb, N_KV, nch, HEAD_DIM + V7X_LANES), BF16),
                   jax.ShapeDtypeStruct((nb, N_KV, nch, HEAD_DIM), BF16)],
        compiler_params=_params(("arbitrary", "arbitrary"), 24 << 20),
        name="cmp_blocks",
    )(k_c, v_c, wk.reshape(L_CMP, c), wv.reshape(L_CMP, c))


def _block_scores_topk_t(imp_t, q0, nsel, tq):
    nsp = -(-nsel // V7X_SUBLANES) * V7X_SUBLANES
    jt = lax.broadcasted_iota(jnp.int32, (nsp, tq), 0)
    qpos = q0 + lax.broadcasted_iota(jnp.int32, (nsp, tq), 1)
    cur = qpos >> SEL_SHIFT
    valid = (jt <= cur) & (jt < nsel)
    forced = (jt == 0) | (jt == cur) | (jt == cur - 1)
    score = jnp.where(valid, imp_t[:nsp] + jnp.where(forced, FORCE_BONUS, 0.0), NEG)
    rank = jnp.zeros((nsp, tq), F32)
    for i in range(nsel):
        ri = score[i:i + 1, :]
        beats = (ri > score) | ((ri == score) & (jt > i))
        rank = rank + jnp.where(beats, 1.0, 0.0)
    return jnp.where((rank < float(min(N_SEL, nsel))) & valid, 1.0, 0.0)


def _nsa_prompt_kernel(slopes_ref, q_ref, gt_ref, kc_ref, vc_ref, ks_ref, vs_ref, kw_ref, vw_ref, wcast_ref,
                       o_ref, wcast_out_ref, ksa, vsb, kwa, vwb, qa_ref, s_ref, p_ref, bdiag_ref, bwin_ref,
                       bcmp_ref, m_ref, acc_ref, oc_ref, os_ref, ow_ref, psum_ref, *, tq, gqa, s_len, cast_body):
    h = pl.program_id(1)
    i = pl.program_id(2)
    cast_body(wcast_ref, wcast_out_ref)
    q0 = i * tq
    hd = HEAD_DIM
    ncp = kc_ref.shape[0]
    n_cmp = ncp - 1
    nsel = s_len // L_SEL
    rows = gqa * tq
    rc = ATTN_ROW_CHUNK
    per_g = tq // rc
    r_io = lax.broadcasted_iota(jnp.int32, (tq, tq), 0)
    c_io = lax.broadcasted_iota(jnp.int32, (tq, tq), 1)

    @pl.when(i == 0)
    def _():
        kpos = lax.broadcasted_iota(jnp.int32, (s_len, V7X_LANES), 0)
        ksa[:, :hd] = ks_ref[...].astype(BF16)
        ksa[:, hd:] = _pos_lanes(kpos, kpos >> SEL_SHIFT)
        kwa[:, :hd] = kw_ref[...].astype(BF16)
        kwa[:, hd:] = _pos_lanes(kpos)
        bdiag_ref[...] = jnp.where(c_io <= r_io, 0.0, NEG)
        ones = jnp.ones((s_len, V7X_LANES), BF16)
        vsb[:, :hd] = vs_ref[...].astype(BF16)
        vsb[:, hd:] = ones
        vwb[:, :hd] = vw_ref[...].astype(BF16)
        vwb[:, hd:] = ones
        wshape = bwin_ref.shape[1:]
        rw = lax.broadcasted_iota(jnp.int32, wshape, 0)
        cw = lax.broadcasted_iota(jnp.int32, wshape, 1)
        for v in range(bwin_ref.shape[0]):
            dist = min(v * tq, WINDOW) + rw - cw
            bwin_ref[v] = jnp.where((dist >= 0) & (dist <= WINDOW), 0.0, NEG)

    coefs = [_coef_lanes(slopes_ref[h * gqa + g] * LOG2E, (V7X_SUBLANES, V7X_LANES))[0:1] for g in range(gqa)]
    for g in range(gqa):
        rs = slice(g * tq, (g + 1) * tq)
        qa_ref[rs, :hd] = q_ref[:, g * hd:(g + 1) * hd]
        qa_ref[rs, hd:] = jnp.broadcast_to(coefs[g], (tq, V7X_LANES)).astype(BF16)

    def head_rows(c, g):
        if isinstance(c, int):
            return slice(g * tq + c * rc, g * tq + (c + 1) * rc)
        return pl.ds(pl.multiple_of(g * tq + c * rc, rc), rc)

    def tile_rows(c):
        if isinstance(c, int):
            return slice(c * rc, (c + 1) * rc)
        return pl.ds(pl.multiple_of(c * rc, rc), rc)

    s_ref[:, :ncp] = _nt_dot(qa_ref[...], kc_ref[...])
    n_io = lax.broadcasted_iota(jnp.int32, (tq, ncp), 1)
    qpos_c = q0 + lax.broadcasted_iota(jnp.int32, (tq, ncp), 0)
    bcmp_ref[...] = jnp.where((qpos_c >= (n_io + 2) * D_CMP - 1) & (n_io < n_cmp), 0.0, NEG)

    def cmp_chunk(c, carry):
        br = tile_rows(c)
        bias = bcmp_ref[br, :]
        live = bias == 0.0
        tot = jnp.zeros((rc, ncp), F32)
        for g in range(gqa):
            rr = head_rows(c, g)
            s = s_ref[rr, :ncp] + bias
            p = jnp.where(live, jnp.exp2(s - jnp.max(s, axis=1, keepdims=True)), 0.0)
            p = p / jnp.maximum(jnp.sum(p, axis=1, keepdims=True), 1e-30)
            tot = tot + p
            p_ref[rr, :ncp] = p.astype(BF16)
        psum_ref[br, :] = tot
        return carry

    for c in range(per_g):
        cmp_chunk(c, 0)
    oc_ref[...] = jnp.dot(p_ref[:, :ncp], vc_ref[...], preferred_element_type=F32)
    psum = psum_ref[...]

    j_io = lax.broadcasted_iota(jnp.int32, (V7X_LANES, ncp), 0)
    n_io2 = lax.broadcasted_iota(jnp.int32, (V7X_LANES, ncp), 1)
    cover_t = jnp.where((n_io2 * D_CMP < j_io * L_SEL + L_SEL) & (n_io2 * D_CMP + L_CMP > j_io * L_SEL)
                        & (n_io2 < n_cmp) & (j_io < nsel), 1.0, 0.0)
    imp_t = _nt_dot(cover_t, psum, precision=lax.Precision.HIGHEST)
    sel_t = _block_scores_topk_t(imp_t, q0, nsel, tq)
    sel_t = jnp.concatenate([sel_t, jnp.zeros((V7X_LANES - sel_t.shape[0], tq), F32)], axis=0)
    lane_q = lax.broadcasted_iota(jnp.int32, (tq, V7X_LANES), 1)
    unpicked = (sel_t.T - 1.0) * MASK_BIG
    for g in range(gqa):
        side = jnp.where(lane_q < POS_OFF, unpicked, jnp.broadcast_to(coefs[g], (tq, V7X_LANES)))
        qa_ref[g * tq:(g + 1) * tq, hd:] = side.astype(BF16)

    def softmax_passes(src_ref, width, bias_ref, first):
        lanes = [slice(j * V7X_LANES, (j + 1) * V7X_LANES) for j in range(width // V7X_LANES)]

        def scores(c, g):
            vals = [src_ref[head_rows(c, g), ls] for ls in lanes]
            if bias_ref is None:
                return vals
            return [v + bias_ref[tile_rows(c), ls] for v, ls in zip(vals, lanes)]

        for c in range(per_g):
            for g in range(gqa):
                rr = head_rows(c, g)
                m_new = jnp.max(functools.reduce(jnp.maximum, scores(c, g)), axis=1, keepdims=True)
                if first:
                    m_ref[rr, :] = jnp.broadcast_to(m_new, (rc, V7X_LANES))
                else:
                    m_prev = m_ref[rr, :]
                    m_new = jnp.maximum(m_prev, m_new)
                    alpha = jnp.exp2(m_prev - m_new)
                    m_ref[rr, :] = m_new
                    acc_ref[rr, :hd] = alpha * acc_ref[rr, :hd]
                    acc_ref[rr, hd:] = alpha * acc_ref[rr, hd:]
        for c in range(per_g):
            for g in range(gqa):
                rr = head_rows(c, g)
                m_new = m_ref[rr, :]
                for ls, v in zip(lanes, scores(c, g)):
                    p_ref[rr, ls] = jnp.exp2(v - m_new).astype(BF16)

    def sel_tile(kt, first, diagonal):
        k0 = kt * tq if isinstance(kt, int) else pl.multiple_of(kt * tq, tq)
        s_ref[:, :tq] = _nt_dot(qa_ref[...], ksa[pl.ds(k0, tq), :])
        softmax_passes(s_ref, tq, bdiag_ref if diagonal else None, first)
        pv = jnp.dot(p_ref[:, :tq], vsb[pl.ds(k0, tq), :], preferred_element_type=F32)
        if first:
            acc_ref[...] = pv
        else:
            acc_ref[...] += pv

    @pl.when(i == 0)
    def _():
        sel_tile(0, True, True)

    @pl.when(i > 0)
    def _():
        sel_tile(0, True, False)

        def sel_mid(kt, carry):
            sel_tile(kt, False, False)
            return carry

        lax.fori_loop(1, i, sel_mid, 0)
        sel_tile(i, False, True)

    os_ref[...] = acc_ref[:, :hd] / jnp.maximum(acc_ref[:, hd:], 1e-30)

    wk = bwin_ref.shape[2]
    k0w = pl.multiple_of(jnp.maximum(q0 - WINDOW, 0), tq)
    s_ref[...] = _nt_dot(qa_ref[...], kwa[pl.ds(k0w, wk), :])
    softmax_passes(s_ref, wk, bwin_ref.at[jnp.minimum(i, bwin_ref.shape[0] - 1)], True)
    pv = jnp.dot(p_ref[...], vwb[pl.ds(k0w, wk), :], preferred_element_type=F32)
    ow_ref[...] = pv[:, :hd] / jnp.maximum(pv[:, hd:], 1e-30)

    gt = gt_ref[...]
    for g in range(gqa):
        rs = slice(g * tq, (g + 1) * tq)
        o = (gt[:, 3 * g:3 * g + 1] * oc_ref[rs, :] + gt[:, 3 * g + 1:3 * g + 2] * os_ref[rs, :]
             + gt[:, 3 * g + 2:3 * g + 3] * ow_ref[rs, :])
        o_ref[:, g * hd:(g + 1) * hd] = o.astype(BF16)


def _nsa_prompt(q, gates, kc, vc, k_s, v_s, k_w, v_w, slopes, nb, tq, w_cast, cast_layer):
    t, dq = q.shape
    s_len = t // nb
    gqa = dq // (N_KV * HEAD_DIM)
    nq = s_len // tq
    ncp = kc.shape[2]
    assert WINDOW % tq == 0 and s_len % tq == 0 and s_len // L_SEL <= POS_OFF and s_len >= WINDOW + tq
    assert ncp == V7X_LANES and tq % ATTN_ROW_CHUNK == 0 and s_len < (1 << (POS_LOW_BITS + 8))
    rows = gqa * tq
    hd = HEAD_DIM
    qspec = pl.BlockSpec((tq, gqa * hd), lambda b, h, i, sl: (b * nq + i, h))
    gspec = pl.BlockSpec((tq, V7X_LANES), lambda b, h, i, sl: (b * nq + i, h))

    def cspec(w):
        return pl.BlockSpec((None, None, ncp, w), lambda b, h, i, sl: (b, h, 0, 0))
    kvspec = pl.BlockSpec((s_len, hd), lambda b, h, i, sl: (b, h))
    ka_scr = pltpu.VMEM((s_len, hd + V7X_LANES), BF16)
    v_scr = pltpu.VMEM((s_len, hd + V7X_LANES), BF16)
    row_scr = pltpu.VMEM((rows, hd), F32)
    cast_in, cast_out, cast_shape, cast_body, cast_vmem = _cast_rider(w_cast, cast_layer, (nb, N_KV, nq))
    return pl.pallas_call(
        functools.partial(_nsa_prompt_kernel, tq=tq, gqa=gqa, s_len=s_len, cast_body=cast_body),
        grid_spec=pltpu.PrefetchScalarGridSpec(
            num_scalar_prefetch=1, grid=(nb, N_KV, nq),
            in_specs=[qspec, gspec, cspec(hd + V7X_LANES), cspec(hd), kvspec, kvspec, kvspec, kvspec, cast_in],
            out_specs=[qspec, cast_out],
            scratch_shapes=[ka_scr, v_scr, ka_scr, v_scr,
                            pltpu.VMEM((rows, hd + V7X_LANES), BF16),
                            pltpu.VMEM((rows, WINDOW + tq), F32), pltpu.VMEM((rows, WINDOW + tq), BF16),
                            pltpu.VMEM((tq, tq), F32),
                            pltpu.VMEM((WINDOW // tq + 1, tq, WINDOW + tq), F32),
                            pltpu.VMEM((tq, ncp), F32),
                            pltpu.VMEM((rows, V7X_LANES), F32),
                            pltpu.VMEM((rows, hd + V7X_LANES), F32),
                            row_scr, row_scr, row_scr,
                            pltpu.VMEM((tq, ncp), F32)]),
        out_shape=[jax.ShapeDtypeStruct((t, dq), BF16), cast_shape],
        compiler_params=_params(("arbitrary", "arbitrary", "arbitrary"), (40 << 20) + cast_vmem),
        name="nsa_prompt",
    )(slopes, q, gates, kc, vc, k_s, v_s, k_w, v_w, w_cast)


def _page_chunks_kernel(pt_ref, *refs, pages):
    k_pages = refs[:pages]
    v_pages = refs[pages:2 * pages]
    wk_ref, wv_ref, fk_ref, sk_ref, fv_ref, sv_ref = refs[2 * pages:]
    per = PAGE_SIZE // D_CMP

    def sums(x, w):
        x4 = x.reshape((per, D_CMP) + x.shape[1:])
        return jnp.sum(x4 * w[None, :D_CMP], axis=1), jnp.sum(x4 * w[None, D_CMP:], axis=1)

    for p in range(pages):
        rs = slice(p * per, (p + 1) * per)
        fk_ref[rs], sk_ref[rs] = sums(k_pages[p][...], wk_ref[...])
        fv_ref[rs], sv_ref[rs] = sums(v_pages[p][...], wv_ref[...])


def _page_chunks(page_table, pool_k, pool_v, wk, wv, pages=16):
    bd, n_pages = page_table.shape
    n_pool, _, n_kv, hd = pool_k.shape
    per = PAGE_SIZE // D_CMP
    nch = n_pages * per
    assert n_pages % pages == 0

    def page_spec(p):
        return pl.BlockSpec((None, PAGE_SIZE, n_kv, hd), lambda b, t, pt: (pt[b, t * pages + p], 0, 0, 0))
    wspec = pl.BlockSpec((L_CMP, n_kv, hd), lambda b, t, pt: (0, 0, 0))
    ospec = pl.BlockSpec((None, pages * per, n_kv, hd), lambda b, t, pt: (b, t, 0, 0))
    osh = jax.ShapeDtypeStruct((bd, nch, n_kv, hd), F32)
    return pl.pallas_call(
        functools.partial(_page_chunks_kernel, pages=pages),
        grid_spec=pltpu.PrefetchScalarGridSpec(
            num_scalar_prefetch=1, grid=(bd, n_pages // pages),
            in_specs=[page_spec(p) for p in range(pages)] * 2 + [wspec, wspec],
            out_specs=[ospec] * 4),
        out_shape=[osh] * 4,
        compiler_params=_params(("arbitrary", "arbitrary"), 52 << 20),
        name="page_chunks",
    )(page_table, *([pool_k] * pages), *([pool_v] * pages), wk, wv)


def _nsa_sample_cmp_kernel(q_ref, sl_ref, fk_ref, sk_ref, fv_ref, sv_ref, cover_ref, oc_ref, idx_ref, ok_ref,
                           *, past, n_sel_blocks):
    n_kv, gqa, _ = q_ref.shape
    nch = fk_ref.shape[0]
    n_cmp = nch - 1
    nj = cover_ref.shape[1]
    q_pos = past
    n_io = lax.broadcasted_iota(jnp.int32, (gqa, nch), 1)
    dist = (q_pos - ((n_io + 2) * D_CMP - 1)).astype(F32)
    ok = (dist >= 0.0) & (n_io < n_cmp)
    j_io = lax.broadcasted_iota(jnp.int32, (gqa, nj), 1)
    cur = q_pos // L_SEL
    valid = (j_io <= cur) & (j_io < n_sel_blocks)
    forced = (j_io == 0) | (j_io == cur) | (j_io == cur - 1)
    lane = lax.broadcasted_iota(jnp.int32, (gqa, V7X_LANES), 1)
    j_f = j_io.astype(F32)
    for h in range(n_kv):
        kc = _blocks_from_sums(fk_ref[:, h, :], sk_ref[:, h, :]).astype(BF16)
        vc = _blocks_from_sums(fv_ref[:, h, :], sv_ref[:, h, :]).astype(BF16)
        s = jnp.where(ok, _nt_dot(q_ref[h], kc) - (sl_ref[h] * LOG2E) * dist, NEG)
        p = jnp.where(ok, jnp.exp2(s - jnp.max(s, axis=1, keepdims=True)), 0.0)
        p = p / jnp.maximum(jnp.sum(p, axis=1, keepdims=True), 1e-30)
        oc_ref[h] = jnp.dot(p.astype(BF16), vc, preferred_element_type=F32)

        psum = jnp.broadcast_to(jnp.sum(p, axis=0, keepdims=True), (gqa, nch))
        imp = jnp.dot(psum, cover_ref[...], precision=lax.Precision.HIGHEST, preferred_element_type=F32)
        score = jnp.where(valid, imp + jnp.where(forced, FORCE_BONUS, 0.0), NEG)
        score = jnp.where(j_io < n_sel_blocks, score, -jnp.inf)
        idx = jnp.zeros((gqa, V7X_LANES), F32)
        okv = jnp.zeros((gqa, V7X_LANES), jnp.int32)
        for t in range(min(N_SEL, n_sel_blocks)):
            mx = jnp.max(score, axis=1, keepdims=True)
            am = jnp.min(jnp.where(score == mx, j_f, float(nj)), axis=1, keepdims=True)
            idx = jnp.where(lane == t, am, idx)
            okv = jnp.where(lane == t, jnp.where(mx > 0.5 * NEG, 1, 0), okv)
            score = jnp.where(j_f == am, -jnp.inf, score)
        idx_ref[h] = idx.astype(jnp.int32)
        ok_ref[h] = okv


def _nsa_sample_cmp(q, slopes_col, sums, past):
    bh, gqa, hd = q.shape
    fk = sums[0]
    bd, nch, n_kv, _ = fk.shape
    n_sel_blocks = past // L_SEL + 1
    nj = -(-n_sel_blocks // V7X_LANES) * V7X_LANES
    n_io = np.arange(nch)[:, None]
    j_io = np.arange(nj)[None, :]
    cover = ((n_io * D_CMP < j_io * L_SEL + L_SEL) & (n_io * D_CMP + L_CMP > j_io * L_SEL)
             & (n_io < nch - 1) & (j_io < n_sel_blocks)).astype(np.float32)
    hspec = pl.BlockSpec((n_kv, gqa, hd), lambda b: (b, 0, 0))
    sspec = pl.BlockSpec((None, nch, n_kv, hd), lambda b: (b, 0, 0, 0))
    ispec = pl.BlockSpec((n_kv, gqa, V7X_LANES), lambda b: (b, 0, 0))
    return pl.pallas_call(
        functools.partial(_nsa_sample_cmp_kernel, past=past, n_sel_blocks=n_sel_blocks),
        grid=(bd,),
        in_specs=[hspec, pl.BlockSpec((n_kv, gqa, 1), lambda b: (0, 0, 0)), sspec, sspec, sspec, sspec,
                  pl.BlockSpec((nch, nj), lambda b: (0, 0))],
        out_specs=[hspec, ispec, ispec],
        out_shape=[jax.ShapeDtypeStruct((bh, gqa, hd), F32),
                   jax.ShapeDtypeStruct((bh, gqa, V7X_LANES), jnp.int32),
                   jax.ShapeDtypeStruct((bh, gqa, V7X_LANES), jnp.int32)],
        compiler_params=_params(("arbitrary",), 48 << 20),
        name="nsa_sample_cmp",
    )(q, slopes_col, *sums, jnp.asarray(cover))


def _nsa_sample_sel_kernel(pt_ref, idx_ref, okb_ref, *refs, past, nsb):
    k_blks = refs[:nsb]
    v_blks = refs[nsb:2 * nsb]
    (q_ref, sl_ref, gt_ref, oc_ref, ksn_ref, vsn_ref, kwc_ref, vwc_ref, kwn_ref, vwn_ref, o_ref) = refs[2 * nsb:]
    b = pl.program_id(0)
    h = pl.program_id(1)
    gqa, hd = q_ref.shape
    base = (b * N_KV + h) * nsb
    nb_past = past // L_SEL
    q_pos = past
    q = q_ref[...]
    slope = sl_ref[...] * LOG2E
    head = lambda ref: ref[:, pl.ds(h, 1), :].reshape(ref.shape[0], hd)

    row_io = lax.broadcasted_iota(jnp.int32, (L_SEL, hd), 0)
    new_k = jnp.where(row_io == 0, jnp.broadcast_to(ksn_ref[...], (L_SEL, hd)), 0.0)
    new_v = jnp.where(row_io == 0, jnp.broadcast_to(vsn_ref[...], (L_SEL, hd)), 0.0)
    lane = lax.broadcasted_iota(jnp.int32, (gqa, nsb * L_SEL), 1)
    kpos = lane & (L_SEL - 1)
    okl = jnp.zeros((gqa, nsb * L_SEL), jnp.int32)
    kg, vg = [], []
    for t in range(nsb):
        bid = idx_ref[base + t]
        is_new = bid >= nb_past
        kg.append(jnp.where(is_new, new_k, head(k_blks[t])).astype(BF16))
        vg.append(jnp.where(is_new, new_v, head(v_blks[t])).astype(BF16))
        in_t = (lane >> SEL_SHIFT) == t
        kpos = kpos + jnp.where(in_t, bid * L_SEL, 0)
        okl = okl + jnp.where(in_t, okb_ref[base + t], 0)
    kg = jnp.concatenate(kg, axis=0)
    vg = jnp.concatenate(vg, axis=0)
    dist = (q_pos - kpos).astype(F32)
    ok = (dist >= 0.0) & (okl > 0)
    s = jnp.where(ok, _nt_dot(q, kg) - slope * dist, NEG)
    p = jnp.where(ok, jnp.exp2(s - jnp.max(s, axis=1, keepdims=True)), 0.0)
    p = p / jnp.maximum(jnp.sum(p, axis=1, keepdims=True), 1e-30)
    o_s = jnp.dot(p.astype(BF16), vg, preferred_element_type=F32)

    wbuf = kwc_ref.shape[0]
    i_io = lax.broadcasted_iota(jnp.int32, (gqa, wbuf), 1)
    kpos_w = past - wbuf + i_io
    dist_w = (q_pos - kpos_w).astype(F32)
    ok_w = (dist_w >= 0.0) & (dist_w <= float(WINDOW)) & (kpos_w >= 0)
    s_w = jnp.where(ok_w, _nt_dot(q, head(kwc_ref).astype(BF16)) - slope * dist_w, NEG)
    kn = jnp.broadcast_to(kwn_ref[...], (V7X_SUBLANES, hd)).astype(BF16)
    s_n = _nt_dot(q, kn)[:, 0:1]
    m = jnp.maximum(jnp.max(s_w, axis=1, keepdims=True), s_n)
    p_w = jnp.where(ok_w, jnp.exp2(s_w - m), 0.0)
    p_n = jnp.exp2(s_n - m)
    den = jnp.maximum(jnp.sum(p_w, axis=1, keepdims=True) + p_n, 1e-30)
    p_w = p_w / den
    p_n = p_n / den
    o_w = (jnp.dot(p_w.astype(BF16), head(vwc_ref).astype(BF16), preferred_element_type=F32)
           + p_n.astype(BF16).astype(F32) * vwn_ref[...].astype(BF16).astype(F32))

    gt = gt_ref[...]
    o_ref[...] = (gt[:, 0:1] * oc_ref[...] + gt[:, 1:2] * o_s + gt[:, 2:3] * o_w).astype(BF16)


def _nsa_sample_sel(page_table, idx_flat, ok_flat, pool_k, pool_v, q, slopes_col, gates, o_c,
                    ks_new, vs_new, cache_kw, cache_vw, kw_new, vw_new, past):
    bh, gqa, hd = q.shape
    bd = bh // N_KV
    nsb = idx_flat.shape[0] // bh
    c = N_KV * hd
    halves = PAGE_SIZE // L_SEL
    nb_past = past // L_SEL
    wbuf = cache_kw.shape[1]

    def blk_spec(t):
        def imap(b, h, pt, idx, okb):
            bid = jnp.minimum(idx[(b * N_KV + h) * nsb + t], nb_past - 1)
            return (pt[b, bid // halves], bid % halves, 0, 0)
        return pl.BlockSpec((None, L_SEL, N_KV, hd), imap)
    hspec = pl.BlockSpec((None, gqa, hd), lambda b, h, *_: (b * N_KV + h, 0, 0))
    nspec = pl.BlockSpec((None, 1, hd), lambda b, h, *_: (b, 0, h))
    wspec = pl.BlockSpec((None, wbuf, N_KV, hd), lambda b, h, *_: (b, 0, 0, 0))
    return pl.pallas_call(
        functools.partial(_nsa_sample_sel_kernel, past=past, nsb=nsb),
        grid_spec=pltpu.PrefetchScalarGridSpec(
            num_scalar_prefetch=3, grid=(bd, N_KV),
            in_specs=[blk_spec(t) for t in range(nsb)] * 2
            + [hspec, pl.BlockSpec((None, gqa, 1), lambda b, h, *_: (h, 0, 0)),
               pl.BlockSpec((None, gqa, 3), lambda b, h, *_: (b * N_KV + h, 0, 0)), hspec,
               nspec, nspec, wspec, wspec, nspec, nspec],
            out_specs=hspec),
        out_shape=jax.ShapeDtypeStruct((bh, gqa, hd), BF16),
        compiler_params=_params(("arbitrary", "arbitrary"), 40 << 20),
        name="nsa_sample_sel",
    )(page_table, idx_flat, ok_flat, *([pool_k] * nsb), *([pool_v] * nsb), q, slopes_col, gates, o_c,
      ks_new.reshape(bd, 1, c), vs_new.reshape(bd, 1, c), cache_kw, cache_vw,
      kw_new.reshape(bd, 1, c), vw_new.reshape(bd, 1, c))


TM_MM = 1024
TM_ROWS = 256
TQ = 256


def kernel(x_prompt, x_sample, c_prompt, c_sample, page_table, cache_k_cmp, cache_v_cmp, cache_k_sel, cache_v_sel, cache_k_win, cache_v_win, state_h, state_conv, w_ada, b_ada, ln_g, ln_b, a_w_in, a_conv_w, a_conv_b, a_w_r, a_b_r, a_w_i, a_b_i, a_lambda, a_w_out, w_kv, w_cmp_k, w_cmp_v, b_w_qg, b_w_o, f_w_up, f_w_down):
    nb, s_len, d = x_prompt.shape
    bd, sd, _ = x_sample.shape
    assert sd == 1
    depth = w_ada.shape[0]
    n_a = a_w_in.shape[0]
    assert depth - n_a == 1 or depth == n_a, "one shared-KV NSA layer stack"
    d_ff = f_w_down.shape[1]
    lru = a_w_out.shape[1]
    n_heads = d // HEAD_DIM
    gqa = n_heads // N_KV
    ckv = N_KV * HEAD_DIM
    past = page_table.shape[1] * PAGE_SIZE
    alpha = (2.0 * depth) ** 0.25
    t = nb * s_len
    tm_mm = min(TM_MM, s_len)
    tm_rows = min(TM_ROWS, s_len)
    tn = 512

    rows = -(-(nb + bd) // V7X_SUBLANES) * V7X_SUBLANES
    c_all = jnp.concatenate([c_prompt, c_sample, jnp.zeros((rows - nb - bd, d), F32)], axis=0)
    mods = {0: _ada(c_all, w_ada, b_ada, 0)}

    def mod_p(l, k):
        return mods[l][:nb, k * d:(k + 1) * d].reshape(nb, 1, d)

    def mod_s(l, k):
        return mods[l][nb:nb + bd, k * d:(k + 1) * d].reshape(1, bd, d)

    slopes = jnp.exp2(-8.0 * jnp.arange(1, n_heads + 1, dtype=F32) / n_heads)
    slopes_col = slopes.reshape(N_KV, gqa, 1)

    xp = x_prompt.reshape(t, d)
    xs = x_sample.reshape(bd, d)
    up = _modulate(xp, mod_p(0, 0), mod_p(0, 1), tm_rows)
    us = _modulate(xs, mod_s(0, 0), mod_s(0, 1), bd)

    def ffn(x_p, x_s, u_p, u_s, l, nxt, w_down):
        hmid = _matmul(u_p, u_s, f_w_up, d_ff, layer=l, tm=min(2 * tm_mm, s_len), tn=256, out_dtype=BF16,
                       epilogue=_swiglu_epilogue, col_off=0, col_off2=d_ff, name="ffn_up")
        ys = _matmul(hmid[0], hmid[1], w_down, d, tm=min(tm_mm, 512), tn=tn, out_dtype=F32,
                     ada=None if nxt is None else (c_all, w_ada, b_ada, nxt[0]), name="ffn_down")
        if nxt is not None:
            mods[nxt[0]] = ys[2]
        outs = []
        for x, y, tm_r, mod in ((x_p, ys[0], tm_rows, mod_p), (x_s, ys[1], bd, mod_s)):
            if nxt is None:
                outs.append(_ln_mod(x, y, mod(l, 5), ln_g[l, 1], ln_b[l, 1], alpha, tm_r))
            else:
                outs.append(_ln_mod(x, y, mod(l, 5), ln_g[l, 1], ln_b[l, 1], alpha, tm_r,
                                    shift=mod(nxt[0], 0), scale=mod(nxt[0], 1), want_xb=nxt[1]))
        return outs

    h_p, h_s, cb_p, cb_s = [], [], [], []
    kv_p = kv_s = None
    for l in range(depth):
        if l < n_a:
            wr = a_w_r[l].astype(BF16)
            wi = a_w_i[l].astype(BF16)
            ggx_p, ggx_s = _matmul(up, us, a_w_in, lru, layer=l, tm=tm_mm, tn=tn, out_dtype=BF16,
                                   epilogue=_gelu_tanh, name="lru_in_g")
            rx_p, rx_s = _matmul(up, us, a_w_in, lru, layer=l, tm=tm_mm, tn=tn, out_dtype=F32, col_off=lru,
                                 name="lru_in_r")
            yin_p, hl_p, ct_p, w_down = _rglru_prompt(
                rx_p, ggx_p, jnp.zeros((nb, CONV_W - 1, lru), F32), jnp.zeros((nb, lru), F32),
                a_conv_w[l], a_conv_b[l], wr, a_b_r[l], wi, a_b_i[l], a_lambda[l], nb, min(TM_ROWS, s_len),
                f_w_down, l)
            yin_s, hl_s, ct_s = _rglru_step(rx_s, ggx_s, jnp.swapaxes(state_conv[l], 0, 1), state_h[l],
                                            a_conv_w[l], a_conv_b[l], wr, a_b_r[l], wi, a_b_i[l], a_lambda[l])
            h_p.append(hl_p.reshape(nb, lru))
            h_s.append(hl_s)
            cb_p.append(ct_p)
            cb_s.append(jnp.swapaxes(ct_s, 0, 1))
            y_p, y_s = _matmul(yin_p, yin_s, a_w_out, d, layer=l, tm=tm_mm, tn=tn, out_dtype=F32, name="lru_out")
        else:
            j = l - n_a
            w_g = b_w_qg[j][:, n_heads * HEAD_DIM:].reshape(d, N_KV, 3 * gqa)
            w_g = jnp.pad(w_g, ((0, 0), (0, 0), (0, V7X_LANES - 3 * gqa))).reshape(d, N_KV * V7X_LANES)
            w_q = b_w_qg[j, :, :n_heads * HEAD_DIM].astype(BF16)
            q_p, q_s = _matmul(up, us, w_q, d, tm=tm_mm, tn=tn, out_dtype=BF16,
                               epilogue=lambda a: a * EXP2_SCALE, name="nsa_q")
            g_p, g_s = _matmul(up, us, w_g, N_KV * V7X_LANES, tm=tm_mm, tn=N_KV * V7X_LANES, out_dtype=F32,
                               epilogue=jax.nn.sigmoid, name="nsa_gate")
            o_p, w_down = _nsa_prompt(q_p, g_p, ctx_p[0], ctx_p[1], kv_p[2], kv_p[3], kv_p[4], kv_p[5], slopes, nb,
                                      min(TQ, s_len), f_w_down, l)
            q_s = q_s.reshape(bd * N_KV, gqa, HEAD_DIM)
            g_s = g_s.reshape(bd * N_KV, V7X_LANES)[:, :3 * gqa].reshape(bd * N_KV, gqa, 3)
            o_c, idx, okb = _nsa_sample_cmp(q_s, slopes_col, ctx_s, past)
            nsb = min(N_SEL, past // L_SEL + 1)
            o_s = _nsa_sample_sel(page_table, idx[:, 0, :nsb].reshape(-1), okb[:, 0, :nsb].reshape(-1),
                                  cache_k_sel, cache_v_sel, q_s, slopes_col, g_s, o_c,
                                  kv_s[2], kv_s[3], cache_k_win, cache_v_win, kv_s[4], kv_s[5], past)
            y_p, y_s = _matmul(o_p, o_s.reshape(bd, d), b_w_o, d, layer=j, tm=tm_mm, tn=tn, out_dtype=F32,
                               name="nsa_out")

        xp, up = _ln_mod(xp, y_p, mod_p(l, 2), ln_g[l, 0], ln_b[l, 0], alpha, tm_rows,
                         shift=mod_p(l, 3), scale=mod_p(l, 4))
        xs, us = _ln_mod(xs, y_s, mod_s(l, 2), ln_g[l, 0], ln_b[l, 0], alpha, bd,
                         shift=mod_s(l, 3), scale=mod_s(l, 4))
        last = l == depth - 1
        res_p, res_s = ffn(xp, xs, up, us, l, None if last else (l + 1, l == n_a - 1), w_down)
        if last:
            xp, xs = res_p[0], res_s[0]
        elif l == n_a - 1:
            xp, up, xbp = res_p
            xs, us, xbs = res_s
            kv = [_matmul(xbp, xbs, w_kv, ckv, tm=tm_mm, tn=ckv, out_dtype=F32, col_off=jj * ckv,
                          heads=(N_KV, HEAD_DIM), name="kv_proj") for jj in range(6)]
            kv_p, kv_s, kv_p4, kv_s4 = zip(*kv)
            ctx_p = _cmp_blocks_prompt(kv_p[0], kv_p[1], w_cmp_k, w_cmp_v, nb)
            ctx_s = _page_chunks(page_table, cache_k_cmp, cache_v_cmp, w_cmp_k, w_cmp_v)
        else:
            xp, up = res_p
            xs, us = res_s

    wwin = min(WINDOW, s_len)
    shp_p = (nb, s_len, N_KV, HEAD_DIM)
    shp_s = (bd, sd, N_KV, HEAD_DIM)
    kvp = [a.reshape(shp_p) for a in kv_p4]
    kvs = [a.reshape(shp_s) for a in kv_s4]
    wbuf = cache_k_win.shape[1]
    keep = wbuf - sd
    k_win_s = jnp.concatenate([cache_k_win[:, wbuf - keep:], kvs[4]], axis=1)
    v_win_s = jnp.concatenate([cache_v_win[:, wbuf - keep:], kvs[5]], axis=1)
    return (xp.reshape(nb, s_len, d), xs.reshape(bd, sd, d),
            kvp[0], kvs[0], kvp[1], kvs[1], kvp[2], kvs[2], kvp[3], kvs[3],
            kvp[4][:, -wwin:], k_win_s, kvp[5][:, -wwin:], v_win_s,
            jnp.stack(h_p), jnp.stack(h_s), jnp.stack(cb_p), jnp.stack(cb_s))
```

```python
import functools

import numpy as np
import jax
import jax.numpy as jnp
from jax import lax
from jax.experimental import pallas as pl
from jax.experimental.pallas import tpu as pltpu

F32 = jnp.float32
BF16 = jnp.bfloat16

HEAD_DIM = 128
N_KV = 4
L_CMP = 32
D_CMP = 16
L_SEL = 64
N_SEL = 16
WINDOW = 512
PAGE_SIZE = 128
CONV_W = 4
LRU_C = 8.0
NEG = -1e30
FORCE_BONUS = 1e3
ATTN_SCALE = HEAD_DIM ** -0.5
LN_EPS = 1e-5
EXP2_SCALE = ATTN_SCALE * 1.4426950408889634
ATTN_ROW_CHUNK = 32
SEL_SHIFT = L_SEL.bit_length() - 1
assert 1 << SEL_SHIFT == L_SEL

V7X_VMEM_BYTES = 64 * 1024 * 1024
V7X_LANES = 128
V7X_SUBLANES = 8
VMEM_CAP = V7X_VMEM_BYTES - 3 * 1024 * 1024


def _params(sem, vmem_bytes):
    return pltpu.CompilerParams(dimension_semantics=sem,
                                vmem_limit_bytes=int(min(max(vmem_bytes, 16 << 20), VMEM_CAP)))


def _nt_dot(a, b, **kw):
    return lax.dot_general(a, b, (((1,), (1,)), ((), ())), preferred_element_type=F32, **kw)


def _ada_block(c_ref, w_ref, b_ref, o_ref):
    c = c_ref[...]
    s = (c * jax.nn.sigmoid(c)).astype(BF16)
    o_ref[...] = jnp.dot(s, w_ref[...].astype(BF16), preferred_element_type=F32) + b_ref[...]


def _ada(c_all, w_ada, b_ada, layer, tn=1024):
    depth, d, n6 = w_ada.shape
    r = c_all.shape[0]
    return pl.pallas_call(
        _ada_block,
        grid=(n6 // tn,),
        in_specs=[pl.BlockSpec((r, d), lambda n: (0, 0)),
                  pl.BlockSpec((None, d, tn), lambda n: (layer, 0, n)),
                  pl.BlockSpec((None, 1, tn), lambda n: (layer, 0, n))],
        out_specs=pl.BlockSpec((r, tn), lambda n: (0, n)),
        out_shape=jax.ShapeDtypeStruct((r, n6), F32),
        compiler_params=_params(("arbitrary",), 2 * d * tn * 4 + d * tn * 2 + (8 << 20)),
        name="ada",
    )(c_all, w_ada, b_ada.reshape(depth, 1, n6))


def _modulate_kernel(x_ref, sh_ref, sc_ref, u_ref):
    u_ref[...] = (x_ref[...] * (1.0 + sc_ref[...]) + sh_ref[...]).astype(u_ref.dtype)


def _row_specs(t, d, tm, nb):
    tiles_per_batch = (t // nb) // tm
    xspec = pl.BlockSpec((tm, d), lambda i: (i, 0))

    def vspec(r):
        return pl.BlockSpec((None, r, d), lambda i: (i // tiles_per_batch, 0, 0))
    return xspec, vspec


def _modulate(x, shift, scale, tm):
    t, d = x.shape
    nb, r, _ = shift.shape
    xspec, vspec = _row_specs(t, d, tm, nb)
    return pl.pallas_call(
        _modulate_kernel,
        grid=(t // tm,),
        in_specs=[xspec, vspec(r), vspec(r)],
        out_specs=xspec,
        out_shape=jax.ShapeDtypeStruct((t, d), BF16),
        compiler_params=_params(("arbitrary",), 6 * tm * d * 4),
        name="modulate",
    )(x, shift, scale)


def _ln_mod_kernel(*refs, alpha, want_u, want_xb):
    x_ref, y_ref, gate_ref, g_ref, b_ref = refs[:5]
    pos = 5
    if want_u:
        sh_ref, sc_ref = refs[pos:pos + 2]
        pos += 2
    xo_ref = refs[pos]
    pos += 1
    v = alpha * x_ref[...] + (1.0 + gate_ref[...]) * y_ref[...]
    mu = jnp.mean(v, axis=-1, keepdims=True)
    dlt = v - mu
    var = jnp.mean(dlt * dlt, axis=-1, keepdims=True)
    xn = dlt * lax.rsqrt(var + LN_EPS) * g_ref[...] + b_ref[...]
    xo_ref[...] = xn
    if want_u:
        refs[pos][...] = (xn * (1.0 + sc_ref[...]) + sh_ref[...]).astype(BF16)
        pos += 1
    if want_xb:
        refs[pos][...] = xn.astype(BF16)


def _ln_mod(x, y, gate, ln_g, ln_b, alpha, tm, shift=None, scale=None, want_xb=False):
    t, d = x.shape
    nb, r, _ = gate.shape
    xspec, vspec = _row_specs(t, d, tm, nb)
    pspec = pl.BlockSpec((1, d), lambda i: (0, 0))
    want_u = shift is not None
    ins = [x, y, gate, ln_g.reshape(1, d), ln_b.reshape(1, d)]
    in_specs = [xspec, xspec, vspec(r), pspec, pspec]
    out_shape = [jax.ShapeDtypeStruct((t, d), F32)]
    out_specs = [xspec]
    if want_u:
        ins += [shift, scale]
        in_specs += [vspec(r), vspec(r)]
        out_shape.append(jax.ShapeDtypeStruct((t, d), BF16))
        out_specs.append(xspec)
    if want_xb:
        out_shape.append(jax.ShapeDtypeStruct((t, d), BF16))
        out_specs.append(xspec)
    return pl.pallas_call(
        functools.partial(_ln_mod_kernel, alpha=alpha, want_u=want_u, want_xb=want_xb),
        grid=(t // tm,),
        in_specs=in_specs, out_specs=out_specs, out_shape=out_shape,
        compiler_params=_params(("arbitrary",), 12 * tm * d * 4),
        name="ln_mod",
    )(*ins)


def _gelu_tanh(x):
    return jax.nn.gelu(x, approximate=True)


def _mm_kernel(*refs, n_w, cast_w, epilogue, heads, ada_blocks):
    x_ref, xs_ref = refs[:2]
    w_refs = refs[2:2 + n_w]
    n_in = 2 + n_w + (3 if ada_blocks else 0)
    n_out = (4 if heads else 2) + (1 if ada_blocks else 0)
    outs = refs[n_in:n_in + n_out]
    scr = refs[n_in + n_out:]
    first_row_tile = pl.program_id(1) == 0
    if ada_blocks:
        step = pl.program_id(0) * pl.num_programs(1) + pl.program_id(1)
        pl.when(step < ada_blocks)(lambda: _ada_block(*refs[2 + n_w:n_in], outs[-1]))
    if cast_w:
        @pl.when(first_row_tile)
        def _():
            for w_ref, s_ref in zip(w_refs, scr):
                s_ref[...] = w_ref[...].astype(BF16)
        w_refs = scr

    def product(rows_ref, out_ref, heads_ref):
        rows = rows_ref[...]
        accs = [jnp.dot(rows, w_ref[...], preferred_element_type=F32) for w_ref in w_refs]
        val = epilogue(*accs).astype(out_ref.dtype)
        out_ref[...] = val
        if heads_ref is not None:
            hd = heads_ref.shape[-1]
            for h in range(heads_ref.shape[-2]):
                heads_ref[:, h, :] = val[:, h * hd:(h + 1) * hd]

    product(x_ref, outs[0], outs[2] if heads else None)
    pl.when(first_row_tile)(lambda: product(xs_ref, outs[1], outs[3] if heads else None))


def _matmul(x, xs, w, n_out, *, tm, tn, out_dtype, layer=None, epilogue=None, col_off=0, col_off2=None,
            heads=None, ada=None, name="mm"):
    m, k = x.shape
    ms = xs.shape[0]
    assert m % tm == 0 and n_out % tn == 0 and col_off % tn == 0
    assert (w.ndim == 3) == (layer is not None) and w.shape[-2] == k and xs.shape[1] == k
    offs = [col_off // tn] + ([] if col_off2 is None else [col_off2 // tn])
    n_w = len(offs)
    if epilogue is None:
        epilogue = lambda a: a
    cast_w = w.dtype != BF16
    scratch = [pltpu.VMEM((k, tn), BF16) for _ in range(n_w)] if cast_w else []
    w_bytes = k * tn * w.dtype.itemsize
    vmem = (2 * tm * k * 2 + n_w * (2 * w_bytes + len(scratch) * k * tn * 2)
            + 2 * tm * tn * 4 * (n_w + 1) + (4 << 20))
    if layer is None:
        w_specs = [pl.BlockSpec((k, tn), functools.partial(lambda n, i, o: (0, n + o), o=o)) for o in offs]
    else:
        w_specs = [pl.BlockSpec((None, k, tn), functools.partial(lambda n, i, o: (layer, 0, n + o), o=o))
                   for o in offs]
    out_specs = [pl.BlockSpec((tm, tn), lambda n, i: (i, n)), pl.BlockSpec((ms, tn), lambda n, i: (0, n))]
    out_shape = [jax.ShapeDtypeStruct((m, n_out), out_dtype), jax.ShapeDtypeStruct((ms, n_out), out_dtype)]
    if heads is not None:
        assert n_out == tn == heads[0] * heads[1]
        out_specs += [pl.BlockSpec((tm,) + heads, lambda n, i: (i, 0, 0)),
                      pl.BlockSpec((ms,) + heads, lambda n, i: (0, 0, 0))]
        out_shape += [jax.ShapeDtypeStruct((m,) + heads, out_dtype), jax.ShapeDtypeStruct((ms,) + heads, out_dtype)]
        vmem += 4 * tm * tn * 4
    m_tiles = m // tm
    extra_in, extra_specs, ada_blocks = [], [], 0
    if ada is not None:
        c_all, w_ada, b_ada, ada_layer = ada
        depth, d, n6 = w_ada.shape
        r = c_all.shape[0]
        steps = (n_out // tn) * m_tiles
        atn = next(c for c in range(V7X_LANES, n6 + 1, V7X_LANES) if n6 % c == 0 and n6 // c <= steps)
        ada_blocks = n6 // atn

        def blk(n, i):
            return jnp.minimum(n * m_tiles + i, ada_blocks - 1)
        extra_in = [c_all, w_ada, b_ada.reshape(depth, 1, n6)]
        extra_specs = [pl.BlockSpec((r, d), lambda n, i: (0, 0)),
                       pl.BlockSpec((None, d, atn), lambda n, i: (ada_layer, 0, blk(n, i))),
                       pl.BlockSpec((None, 1, atn), lambda n, i: (ada_layer, 0, blk(n, i)))]
        out_specs.append(pl.BlockSpec((r, atn), lambda n, i: (0, blk(n, i))))
        out_shape.append(jax.ShapeDtypeStruct((r, n6), F32))
        vmem += 2 * d * atn * 4 + d * atn * 2
    return pl.pallas_call(
        functools.partial(_mm_kernel, n_w=n_w, cast_w=cast_w, epilogue=epilogue, heads=heads is not None,
                          ada_blocks=ada_blocks),
        grid=(n_out // tn, m_tiles),
        in_specs=[pl.BlockSpec((tm, k), lambda n, i: (i, 0)), pl.BlockSpec((ms, k), lambda n, i: (0, 0))]
        + w_specs + extra_specs,
        out_specs=out_specs, out_shape=out_shape,
        scratch_shapes=scratch,
        compiler_params=_params(("arbitrary", "arbitrary"), vmem),
        name=name,
    )(x, xs, *([w] * n_w), *extra_in)


def _swiglu_epilogue(g, v):
    return g * jax.nn.sigmoid(g) * v


def _cast_rider(w_stack, layer, grid_shape):
    _, k, n = w_stack.shape
    n_blocks = n // V7X_LANES
    steps = int(np.prod(grid_shape))
    assert n % V7X_LANES == 0 and n_blocks <= steps

    def flat_step(idx):
        s = idx[0]
        for i, extent in zip(idx[1:], grid_shape[1:]):
            s = s * extent + i
        return s

    def blk(*idx):
        return jnp.minimum(flat_step(idx[:len(grid_shape)]), n_blocks - 1)

    def body(in_ref, out_ref):
        step = flat_step([pl.program_id(a) for a in range(len(grid_shape))])

        @pl.when(step < n_blocks)
        def _():
            out_ref[...] = in_ref[...].astype(BF16)

    in_spec = pl.BlockSpec((None, k, V7X_LANES), lambda *idx: (layer, 0, blk(*idx)))
    out_spec = pl.BlockSpec((k, V7X_LANES), lambda *idx: (0, blk(*idx)))
    return in_spec, out_spec, jax.ShapeDtypeStruct((k, n), BF16), body, 2 * k * V7X_LANES * 6


def _softplus_neg(lam):
    return jnp.maximum(-lam, 0.0) + jnp.log1p(jnp.exp(-jnp.abs(lam)))


def _lru_gates(xc, wr_ref, br_ref, wi_ref, bi_ref, sp, a_ref, b_ref, nblk, bw):
    def put(ref, k, val):
        ref[:, k * bw:(k + 1) * bw] = val

    nsp = (-LRU_C) * sp
    xcb = xc.astype(BF16)
    for k in range(nblk):
        cs = slice(k * bw, (k + 1) * bw)
        xb = xcb[:, cs]
        r = jax.nn.sigmoid(jnp.dot(xb, wr_ref[k], preferred_element_type=F32) + br_ref[:, cs])
        ig = jax.nn.sigmoid(jnp.dot(xb, wi_ref[k], preferred_element_type=F32) + bi_ref[:, cs])
        log_a = r * nsp[:, cs]
        put(a_ref, k, jnp.exp(log_a))
        th = jnp.tanh(log_a)
        put(b_ref, k, jnp.sqrt(-2.0 * th / (1.0 - th)) * (ig * xc[:, cs]))


SCAN_TILES = 8


def _rglru_kernel(rx_ref, ggx_ref, cb0_ref, h0_ref, cw_ref, cbias_ref, wr_ref, br_ref, wi_ref, bi_ref,
                  lam_ref, wcast_ref, y_ref, hlast_ref, ctail_ref, wcast_out_ref, xpad_ref, a_ref, b_ref,
                  hcar_ref, *, tm, nblk, bw, cast_body):
    s = pl.program_id(1)
    width = nblk * bw
    pad = V7X_SUBLANES
    tail = CONV_W - 1
    cast_body(wcast_ref, wcast_out_ref)

    @pl.when(s == 0)
    def _():
        xpad_ref[pad - tail:pad, :] = cb0_ref[...]
        hcar_ref[...] = h0_ref[...]

    rx = rx_ref[...]
    xpad_ref[pad:pad + tm, :] = rx
    cw = cw_ref[...]
    xc = cbias_ref[...] + rx * cw[tail:tail + 1]
    for j in range(tail):
        xc = xc + xpad_ref[pad - tail + j:pad - tail + j + tm, :] * cw[j:j + 1]
    xpad_ref[pad - tail:pad, :] = rx[tm - tail:tm, :]

    _lru_gates(xc, wr_ref, br_ref, wi_ref, bi_ref, _softplus_neg(lam_ref[...]), a_ref, b_ref, nblk, bw)

    scan_lanes = min(SCAN_TILES * V7X_LANES, width)
    row = lax.broadcasted_iota(jnp.int32, (V7X_SUBLANES, scan_lanes), 0)
    for c in range(width // scan_lanes):
        cs = slice(c * scan_lanes, (c + 1) * scan_lanes)

        def group(gi, h, cs=cs):
            r0 = pl.multiple_of(gi * V7X_SUBLANES, V7X_SUBLANES)
            a = a_ref[pl.ds(r0, V7X_SUBLANES), cs]
            b = b_ref[pl.ds(r0, V7X_SUBLANES), cs]
            for sft in (1, 2, 4):
                keep = row >= sft
                a_s = jnp.where(keep, pltpu.roll(a, sft, 0), 1.0)
                b_s = jnp.where(keep, pltpu.roll(b, sft, 0), 0.0)
                b = a * b_s + b
                a = a * a_s
            hs = a * h + b
            b_ref[pl.ds(r0, V7X_SUBLANES), cs] = hs
            return jnp.broadcast_to(hs[V7X_SUBLANES - 1:V7X_SUBLANES, :], hs.shape)

        h_in = jnp.broadcast_to(hcar_ref[:, cs], (V7X_SUBLANES, scan_lanes))
        h_out = lax.fori_loop(0, tm // V7X_SUBLANES, group, h_in)
        hcar_ref[:, cs] = h_out[0:1, :]

    y_ref[...] = (ggx_ref[...].astype(F32) * b_ref[...]).astype(BF16)

    @pl.when(s == pl.num_programs(1) - 1)
    def _():
        hlast_ref[...] = hcar_ref[...]
        ctail_ref[...] = rx[tm - tail:tm, :]


def _rglru_prompt(rx, ggx, conv_buf, h0, cw, cbias, wr, br, wi, bi, lam, nb, tm, w_cast, cast_layer):
    t, width = rx.shape
    s_len = t // nb
    nblk, bw, _ = wr.shape
    tail = CONV_W - 1
    ns = s_len // tm
    row = lambda a: a.reshape(1, width)
    xspec = pl.BlockSpec((tm, width), lambda b, s: (b * ns + s, 0))
    pspec = pl.BlockSpec((1, width), lambda b, s: (0, 0))
    wspec = pl.BlockSpec((nblk, bw, bw), lambda b, s: (0, 0, 0))
    cast_in, cast_out, cast_shape, cast_body, cast_vmem = _cast_rider(w_cast, cast_layer, (nb, ns))
    vmem = (2 * tm * width * (4 + 2 + 2) + 3 * (tm + 8) * width * 4 + 4 * nblk * bw * bw * 2 + (6 << 20)
            + cast_vmem)
    return pl.pallas_call(
        functools.partial(_rglru_kernel, tm=tm, nblk=nblk, bw=bw, cast_body=cast_body),
        grid=(nb, ns),
        in_specs=[xspec, xspec,
                  pl.BlockSpec((None, tail, width), lambda b, s: (b, 0, 0)),
                  pl.BlockSpec((None, 1, width), lambda b, s: (b, 0, 0)),
                  pl.BlockSpec((CONV_W, width), lambda b, s: (0, 0)), pspec,
                  wspec, pspec, wspec, pspec, pspec, cast_in],
        out_specs=[xspec,
                   pl.BlockSpec((None, 1, width), lambda b, s: (b, 0, 0)),
                   pl.BlockSpec((None, tail, width), lambda b, s: (b, 0, 0)), cast_out],
        out_shape=[jax.ShapeDtypeStruct((t, width), BF16),
                   jax.ShapeDtypeStruct((nb, 1, width), F32),
                   jax.ShapeDtypeStruct((nb, tail, width), F32), cast_shape],
        scratch_shapes=[pltpu.VMEM((tm + V7X_SUBLANES, width), F32), pltpu.VMEM((tm, width), F32),
                        pltpu.VMEM((tm, width), F32), pltpu.VMEM((1, width), F32)],
        compiler_params=_params(("arbitrary", "arbitrary"), vmem),
        name="rglru",
    )(rx, ggx, conv_buf, h0.reshape(nb, 1, width), cw, row(cbias), wr, row(br), wi, row(bi), row(lam), w_cast)


def _rglru_step_kernel(rx_ref, ggx_ref, cb_ref, h0_ref, cw_ref, cbias_ref, wr_ref, br_ref, wi_ref, bi_ref,
                       lam_ref, y_ref, h_ref, cnew_ref, a_ref, b_ref, *, nblk, bw):
    tail = CONV_W - 1
    rx = rx_ref[...]
    cw = cw_ref[...]
    xc = cbias_ref[...] + rx * cw[tail:tail + 1]
    for j in range(tail):
        xc = xc + cb_ref[j] * cw[j:j + 1]
    _lru_gates(xc, wr_ref, br_ref, wi_ref, bi_ref, _softplus_neg(lam_ref[...]), a_ref, b_ref, nblk, bw)
    h = a_ref[...] * h0_ref[...] + b_ref[...]
    h_ref[...] = h
    y_ref[...] = (ggx_ref[...].astype(F32) * h).astype(BF16)
    for j in range(tail - 1):
        cnew_ref[j] = cb_ref[j + 1]
    cnew_ref[tail - 1] = rx


def _rglru_step(rx, ggx, conv_t, h0, cw, cbias, wr, br, wi, bi, lam):
    bd, width = rx.shape
    nblk, bw, _ = wr.shape
    row = lambda a: a.reshape(1, width)
    return pl.pallas_call(
        functools.partial(_rglru_step_kernel, nblk=nblk, bw=bw),
        out_shape=[jax.ShapeDtypeStruct((bd, width), BF16), jax.ShapeDtypeStruct((bd, width), F32),
                   jax.ShapeDtypeStruct(conv_t.shape, F32)],
        scratch_shapes=[pltpu.VMEM((bd, width), F32), pltpu.VMEM((bd, width), F32)],
        compiler_params=pltpu.CompilerParams(vmem_limit_bytes=32 << 20),
        name="rglru_step",
    )(rx, ggx, conv_t, h0, cw, row(cbias), wr, row(br), wi, row(bi), row(lam))


def _chunk_sums(x, w):
    n = x.shape[0] // D_CMP
    x3 = x.reshape(n, D_CMP, x.shape[1])
    first = jnp.sum(x3 * w[None, :D_CMP, :], axis=1)
    second = jnp.sum(x3 * w[None, D_CMP:, :], axis=1)
    return first, second


def _blocks_from_sums(first, second):
    n = first.shape[0]
    nxt = pltpu.roll(second, n - 1, 0)
    rows = lax.broadcasted_iota(jnp.int32, first.shape, 0)
    return jnp.where(rows < n - 1, first + nxt, 0.0)


COEF_PIECES = 3
POS_LOW_BITS = 7
POS_OFF = 64
MASK_BIG = 2.0 ** 100
LOG2E = 1.4426950408889634


def _pos_lanes(pos, block_of_key=None):
    lane = lax.broadcasted_iota(jnp.int32, pos.shape, 1)
    hi = ((pos >> POS_LOW_BITS) << POS_LOW_BITS).astype(F32)
    lo = (pos & ((1 << POS_LOW_BITS) - 1)).astype(F32)
    k = lane - POS_OFF
    out = jnp.where((k >= 0) & (k < COEF_PIECES), hi, jnp.where((k >= COEF_PIECES) & (k < 2 * COEF_PIECES), lo, 0.0))
    if block_of_key is not None:
        out = jnp.where((lane < POS_OFF) & (lane == block_of_key), 1.0, out)
    return out.astype(BF16)


def _coef_lanes(coef, shape):
    v = jnp.full(shape, coef, F32)
    c1 = v.astype(BF16).astype(F32)
    c2 = (v - c1).astype(BF16).astype(F32)
    c3 = (v - c1 - c2).astype(BF16).astype(F32)
    k = lax.broadcasted_iota(jnp.int32, shape, 1) - POS_OFF
    live = (k >= 0) & (k < 2 * COEF_PIECES)
    k = jnp.where(k >= COEF_PIECES, k - COEF_PIECES, k)
    return jnp.where(live, jnp.where(k == 0, c1, jnp.where(k == 1, c2, c3)), 0.0)


def _cmp_blocks_kernel(k_ref, v_ref, wk_ref, wv_ref, kc_ref, vc_ref):
    hd = k_ref.shape[1]
    nch = kc_ref.shape[0]
    kc_ref[:, :hd] = _blocks_from_sums(*_chunk_sums(k_ref[...], wk_ref[...])).astype(BF16)
    ends = (lax.broadcasted_iota(jnp.int32, (nch, V7X_LANES), 0) + 2) * D_CMP - 1
    kc_ref[:, hd:] = _pos_lanes(ends)
    vc_ref[...] = _blocks_from_sums(*_chunk_sums(v_ref[...], wv_ref[...])).astype(BF16)


def _cmp_blocks_prompt(k_c, v_c, wk, wv, nb):
    t, c = k_c.shape
    s_len = t // nb
    nch = s_len // D_CMP
    kvspec = pl.BlockSpec((s_len, HEAD_DIM), lambda b, h: (b, h))
    wspec = pl.BlockSpec((L_CMP, HEAD_DIM), lambda b, h: (0, h))

    def ospec(w):
        return pl.BlockSpec((None, None, nch, w), lambda b, h: (b, h, 0, 0))
    return pl.pallas_call(
        _cmp_blocks_kernel, grid=(nb, N_KV),
        in_specs=[kvspec, kvspec, wspec, wspec],
        out_specs=[ospec(HEAD_DIM + V7X_LANES), ospec(HEAD_DIM)],
        out_shape=[jax.ShapeDtypeStruct((nb, N_KV, nch, HEAD_DIM + V7X_LANES), BF16),
                   jax.ShapeDtypeStruct((nb, N_KV, nch, HEAD_DIM), BF16)],
        compiler_params=_params(("arbitrary", "arbitrary"), 24 << 20),
        name="cmp_blocks",
    )(k_c, v_c, wk.reshape(L_CMP, c), wv.reshape(L_CMP, c))


def _block_scores_topk_t(imp_t, q0, nsel, tq):
    nsp = -(-nsel // V7X_SUBLANES) * V7X_SUBLANES
    jt = lax.broadcasted_iota(jnp.int32, (nsp, tq), 0)
    qpos = q0 + lax.broadcasted_iota(jnp.int32, (nsp, tq), 1)
    cur = qpos >> SEL_SHIFT
    valid = (jt <= cur) & (jt < nsel)
    forced = (jt == 0) | (jt == cur) | (jt == cur - 1)
    score = jnp.where(valid, imp_t[:nsp] + jnp.where(forced, FORCE_BONUS, 0.0), NEG)
    rank = jnp.zeros((nsp, tq), F32)
    for i in range(nsel):
        ri = score[i:i + 1, :]
        beats = (ri > score) | ((ri == score) & (jt > i))
        rank = rank + jnp.where(beats, 1.0, 0.0)
    return jnp.where((rank < float(min(N_SEL, nsel))) & valid, 1.0, 0.0)


def _nsa_prompt_kernel(slopes_ref, q_ref, gt_ref, kc_ref, vc_ref, ks_ref, vs_ref, kw_ref, vw_ref, wcast_ref,
                       o_ref, wcast_out_ref, ksa, vsb, kwa, vwb, qa_ref, s_ref, p_ref, bdiag_ref, bwin_ref,
                       bcmp_ref, m_ref, acc_ref, oc_ref, os_ref, ow_ref, psum_ref, *, tq, gqa, s_len, cast_body):
    h = pl.program_id(1)
    i = pl.program_id(2)
    cast_body(wcast_ref, wcast_out_ref)
    q0 = i * tq
    hd = HEAD_DIM
    ncp = kc_ref.shape[0]
    n_cmp = ncp - 1
    nsel = s_len // L_SEL
    rc = ATTN_ROW_CHUNK
    per_g = tq // rc
    r_io = lax.broadcasted_iota(jnp.int32, (tq, tq), 0)
    c_io = lax.broadcasted_iota(jnp.int32, (tq, tq), 1)

    @pl.when(i == 0)
    def _():
        kpos = lax.broadcasted_iota(jnp.int32, (s_len, V7X_LANES), 0)
        ksa[:, :hd] = ks_ref[...].astype(BF16)
        ksa[:, hd:] = _pos_lanes(kpos, kpos >> SEL_SHIFT)
        kwa[:, :hd] = kw_ref[...].astype(BF16)
        kwa[:, hd:] = _pos_lanes(kpos)
        bdiag_ref[...] = jnp.where(c_io <= r_io, 0.0, NEG)
        ones = jnp.ones((s_len, V7X_LANES), BF16)
        vsb[:, :hd] = vs_ref[...].astype(BF16)
        vsb[:, hd:] = ones
        vwb[:, :hd] = vw_ref[...].astype(BF16)
        vwb[:, hd:] = ones
        wshape = bwin_ref.shape[1:]
        rw = lax.broadcasted_iota(jnp.int32, wshape, 0)
        cw = lax.broadcasted_iota(jnp.int32, wshape, 1)
        for v in range(bwin_ref.shape[0]):
            dist = min(v * tq, WINDOW) + rw - cw
            bwin_ref[v] = jnp.where((dist >= 0) & (dist <= WINDOW), 0.0, NEG)

    coefs = [_coef_lanes(slopes_ref[h * gqa + g] * LOG2E, (V7X_SUBLANES, V7X_LANES))[0:1] for g in range(gqa)]
    for g in range(gqa):
        rs = slice(g * tq, (g + 1) * tq)
        qa_ref[rs, :hd] = q_ref[:, g * hd:(g + 1) * hd]
        qa_ref[rs, hd:] = jnp.broadcast_to(coefs[g], (tq, V7X_LANES)).astype(BF16)

    def head_rows(c, g):
        return slice(g * tq + c * rc, g * tq + (c + 1) * rc)

    def tile_rows(c):
        return slice(c * rc, (c + 1) * rc)

    s_ref[:, :ncp] = _nt_dot(qa_ref[...], kc_ref[...])
    n_io = lax.broadcasted_iota(jnp.int32, (tq, ncp), 1)
    qpos_c = q0 + lax.broadcasted_iota(jnp.int32, (tq, ncp), 0)
    bcmp_ref[...] = jnp.where((qpos_c >= (n_io + 2) * D_CMP - 1) & (n_io < n_cmp), 0.0, NEG)

    def cmp_chunk(c, carry):
        br = tile_rows(c)
        bias = bcmp_ref[br, :]
        live = bias == 0.0
        tot = jnp.zeros((rc, ncp), F32)
        for g in range(gqa):
            rr = head_rows(c, g)
            s = s_ref[rr, :ncp] + bias
            p = jnp.where(live, jnp.exp2(s - jnp.max(s, axis=1, keepdims=True)), 0.0)
            p = p / jnp.maximum(jnp.sum(p, axis=1, keepdims=True), 1e-30)
            tot = tot + p
            p_ref[rr, :ncp] = p.astype(BF16)
        psum_ref[br, :] = tot
        return carry

    for c in range(per_g):
        cmp_chunk(c, 0)
    oc_ref[...] = jnp.dot(p_ref[:, :ncp], vc_ref[...], preferred_element_type=F32)
    psum = psum_ref[...]

    j_io = lax.broadcasted_iota(jnp.int32, (V7X_LANES, ncp), 0)
    n_io2 = lax.broadcasted_iota(jnp.int32, (V7X_LANES, ncp), 1)
    cover_t = jnp.where((n_io2 * D_CMP < j_io * L_SEL + L_SEL) & (n_io2 * D_CMP + L_CMP > j_io * L_SEL)
                        & (n_io2 < n_cmp) & (j_io < nsel), 1.0, 0.0)
    imp_t = _nt_dot(cover_t, psum, precision=lax.Precision.HIGHEST)
    sel_t = _block_scores_topk_t(imp_t, q0, nsel, tq)
    sel_t = jnp.concatenate([sel_t, jnp.zeros((V7X_LANES - sel_t.shape[0], tq), F32)], axis=0)
    lane_q = lax.broadcasted_iota(jnp.int32, (tq, V7X_LANES), 1)
    unpicked = (sel_t.T - 1.0) * MASK_BIG
    for g in range(gqa):
        side = jnp.where(lane_q < POS_OFF, unpicked, jnp.broadcast_to(coefs[g], (tq, V7X_LANES)))
        qa_ref[g * tq:(g + 1) * tq, hd:] = side.astype(BF16)

    def softmax_passes(src_ref, width, bias_ref, first):
        lanes = [slice(j * V7X_LANES, (j + 1) * V7X_LANES) for j in range(width // V7X_LANES)]

        def scores(c, g):
            vals = [src_ref[head_rows(c, g), ls] for ls in lanes]
            if bias_ref is None:
                return vals
            return [v + bias_ref[tile_rows(c), ls] for v, ls in zip(vals, lanes)]

        for c in range(per_g):
            for g in range(gqa):
                rr = head_rows(c, g)
                m_new = jnp.max(functools.reduce(jnp.maximum, scores(c, g)), axis=1, keepdims=True)
                if first:
                    m_ref[rr, :] = jnp.broadcast_to(m_new, (rc, V7X_LANES))
                else:
                    m_prev = m_ref[rr, :]
                    m_new = jnp.maximum(m_prev, m_new)
                    alpha = jnp.exp2(m_prev - m_new)
                    m_ref[rr, :] = m_new
                    acc_ref[rr, :hd] = alpha * acc_ref[rr, :hd]
                    acc_ref[rr, hd:] = alpha * acc_ref[rr, hd:]
        for c in range(per_g):
            for g in range(gqa):
                rr = head_rows(c, g)
                m_new = m_ref[rr, :]
                for ls, v in zip(lanes, scores(c, g)):
                    p_ref[rr, ls] = jnp.exp2(v - m_new).astype(BF16)

    def sel_tile(kt, first, diagonal):
        k0 = kt * tq if isinstance(kt, int) else pl.multiple_of(kt * tq, tq)
        s_ref[:, :tq] = _nt_dot(qa_ref[...], ksa[pl.ds(k0, tq), :])
        softmax_passes(s_ref, tq, bdiag_ref if diagonal else None, first)
        pv = jnp.dot(p_ref[:, :tq], vsb[pl.ds(k0, tq), :], preferred_element_type=F32)
        if first:
            acc_ref[...] = pv
        else:
            acc_ref[...] += pv

    @pl.when(i == 0)
    def _():
        sel_tile(0, True, True)

    @pl.when(i > 0)
    def _():
        sel_tile(0, True, False)

        def sel_mid(kt, carry):
            sel_tile(kt, False, False)
            return carry

        lax.fori_loop(1, i, sel_mid, 0)
        sel_tile(i, False, True)

    os_ref[...] = acc_ref[:, :hd] / jnp.maximum(acc_ref[:, hd:], 1e-30)

    wk = bwin_ref.shape[2]
    k0w = pl.multiple_of(jnp.maximum(q0 - WINDOW, 0), tq)
    s_ref[...] = _nt_dot(qa_ref[...], kwa[pl.ds(k0w, wk), :])
    softmax_passes(s_ref, wk, bwin_ref.at[jnp.minimum(i, bwin_ref.shape[0] - 1)], True)
    pv = jnp.dot(p_ref[...], vwb[pl.ds(k0w, wk), :], preferred_element_type=F32)
    ow_ref[...] = pv[:, :hd] / jnp.maximum(pv[:, hd:], 1e-30)

    gt = gt_ref[...]
    for g in range(gqa):
        rs = slice(g * tq, (g + 1) * tq)
        o = (gt[:, 3 * g:3 * g + 1] * oc_ref[rs, :] + gt[:, 3 * g + 1:3 * g + 2] * os_ref[rs, :]
             + gt[:, 3 * g + 2:3 * g + 3] * ow_ref[rs, :])
        o_ref[:, g * hd:(g + 1) * hd] = o.astype(BF16)


def _nsa_prompt(q, gates, kc, vc, k_s, v_s, k_w, v_w, slopes, nb, tq, w_cast, cast_layer):
    t, dq = q.shape
    s_len = t // nb
    gqa = dq // (N_KV * HEAD_DIM)
    nq = s_len // tq
    ncp = kc.shape[2]
    assert WINDOW % tq == 0 and s_len % tq == 0 and s_len // L_SEL <= POS_OFF and s_len >= WINDOW + tq
    assert ncp == V7X_LANES and tq % ATTN_ROW_CHUNK == 0 and s_len < (1 << (POS_LOW_BITS + 8))
    rows = gqa * tq
    hd = HEAD_DIM
    qspec = pl.BlockSpec((tq, gqa * hd), lambda b, h, i, sl: (b * nq + i, h))
    gspec = pl.BlockSpec((tq, V7X_LANES), lambda b, h, i, sl: (b * nq + i, h))

    def cspec(w):
        return pl.BlockSpec((None, None, ncp, w), lambda b, h, i, sl: (b, h, 0, 0))
    kvspec = pl.BlockSpec((s_len, hd), lambda b, h, i, sl: (b, h))
    ka_scr = pltpu.VMEM((s_len, hd + V7X_LANES), BF16)
    v_scr = pltpu.VMEM((s_len, hd + V7X_LANES), BF16)
    row_scr = pltpu.VMEM((rows, hd), F32)
    cast_in, cast_out, cast_shape, cast_body, cast_vmem = _cast_rider(w_cast, cast_layer, (nb, N_KV, nq))
    return pl.pallas_call(
        functools.partial(_nsa_prompt_kernel, tq=tq, gqa=gqa, s_len=s_len, cast_body=cast_body),
        grid_spec=pltpu.PrefetchScalarGridSpec(
            num_scalar_prefetch=1, grid=(nb, N_KV, nq),
            in_specs=[qspec, gspec, cspec(hd + V7X_LANES), cspec(hd), kvspec, kvspec, kvspec, kvspec, cast_in],
            out_specs=[qspec, cast_out],
            scratch_shapes=[ka_scr, v_scr, ka_scr, v_scr,
                            pltpu.VMEM((rows, hd + V7X_LANES), BF16),
                            pltpu.VMEM((rows, WINDOW + tq), F32), pltpu.VMEM((rows, WINDOW + tq), BF16),
                            pltpu.VMEM((tq, tq), F32),
                            pltpu.VMEM((WINDOW // tq + 1, tq, WINDOW + tq), F32),
                            pltpu.VMEM((tq, ncp), F32),
                            pltpu.VMEM((rows, V7X_LANES), F32),
                            pltpu.VMEM((rows, hd + V7X_LANES), F32),
                            row_scr, row_scr, row_scr,
                            pltpu.VMEM((tq, ncp), F32)]),
        out_shape=[jax.ShapeDtypeStruct((t, dq), BF16), cast_shape],
        compiler_params=_params(("arbitrary", "arbitrary", "arbitrary"), (40 << 20) + cast_vmem),
        name="nsa_prompt",
    )(slopes, q, gates, kc, vc, k_s, v_s, k_w, v_w, w_cast)


def _page_chunks_kernel(pt_ref, *refs, pages):
    k_pages = refs[:pages]
    v_pages = refs[pages:2 * pages]
    wk_ref, wv_ref, fk_ref, sk_ref, fv_ref, sv_ref = refs[2 * pages:]
    per = PAGE_SIZE // D_CMP

    def sums(x, w):
        x4 = x.reshape((per, D_CMP) + x.shape[1:])
        return jnp.sum(x4 * w[None, :D_CMP], axis=1), jnp.sum(x4 * w[None, D_CMP:], axis=1)

    for p in range(pages):
        rs = slice(p * per, (p + 1) * per)
        fk_ref[rs], sk_ref[rs] = sums(k_pages[p][...], wk_ref[...])
        fv_ref[rs], sv_ref[rs] = sums(v_pages[p][...], wv_ref[...])


def _page_chunks(page_table, pool_k, pool_v, wk, wv, pages=16):
    bd, n_pages = page_table.shape
    n_pool, _, n_kv, hd = pool_k.shape
    per = PAGE_SIZE // D_CMP
    nch = n_pages * per
    assert n_pages % pages == 0

    def page_spec(p):
        return pl.BlockSpec((None, PAGE_SIZE, n_kv, hd), lambda b, t, pt: (pt[b, t * pages + p], 0, 0, 0))
    wspec = pl.BlockSpec((L_CMP, n_kv, hd), lambda b, t, pt: (0, 0, 0))
    ospec = pl.BlockSpec((None, pages * per, n_kv, hd), lambda b, t, pt: (b, t, 0, 0))
    osh = jax.ShapeDtypeStruct((bd, nch, n_kv, hd), F32)
    return pl.pallas_call(
        functools.partial(_page_chunks_kernel, pages=pages),
        grid_spec=pltpu.PrefetchScalarGridSpec(
            num_scalar_prefetch=1, grid=(bd, n_pages // pages),
            in_specs=[page_spec(p) for p in range(pages)] * 2 + [wspec, wspec],
            out_specs=[ospec] * 4),
        out_shape=[osh] * 4,
        compiler_params=_params(("arbitrary", "arbitrary"), 52 << 20),
        name="page_chunks",
    )(page_table, *([pool_k] * pages), *([pool_v] * pages), wk, wv)


def _nsa_sample_cmp_kernel(q_ref, sl_ref, fk_ref, sk_ref, fv_ref, sv_ref, cover_ref, oc_ref, idx_ref, ok_ref,
                           *, past, n_sel_blocks):
    n_kv, gqa, _ = q_ref.shape
    nch = fk_ref.shape[0]
    n_cmp = nch - 1
    nj = cover_ref.shape[1]
    q_pos = past
    n_io = lax.broadcasted_iota(jnp.int32, (gqa, nch), 1)
    dist = (q_pos - ((n_io + 2) * D_CMP - 1)).astype(F32)
    ok = (dist >= 0.0) & (n_io < n_cmp)
    j_io = lax.broadcasted_iota(jnp.int32, (gqa, nj), 1)
    cur = q_pos // L_SEL
    valid = (j_io <= cur) & (j_io < n_sel_blocks)
    forced = (j_io == 0) | (j_io == cur) | (j_io == cur - 1)
    lane = lax.broadcasted_iota(jnp.int32, (gqa, V7X_LANES), 1)
    j_f = j_io.astype(F32)
    for h in range(n_kv):
        kc = _blocks_from_sums(fk_ref[:, h, :], sk_ref[:, h, :]).astype(BF16)
        vc = _blocks_from_sums(fv_ref[:, h, :], sv_ref[:, h, :]).astype(BF16)
        s = jnp.where(ok, _nt_dot(q_ref[h], kc) - (sl_ref[h] * LOG2E) * dist, NEG)
        p = jnp.where(ok, jnp.exp2(s - jnp.max(s, axis=1, keepdims=True)), 0.0)
        p = p / jnp.maximum(jnp.sum(p, axis=1, keepdims=True), 1e-30)
        oc_ref[h] = jnp.dot(p.astype(BF16), vc, preferred_element_type=F32)

        psum = jnp.broadcast_to(jnp.sum(p, axis=0, keepdims=True), (gqa, nch))
        imp = jnp.dot(psum, cover_ref[...], precision=lax.Precision.HIGHEST, preferred_element_type=F32)
        score = jnp.where(valid, imp + jnp.where(forced, FORCE_BONUS, 0.0), NEG)
        score = jnp.where(j_io < n_sel_blocks, score, -jnp.inf)
        idx = jnp.zeros((gqa, V7X_LANES), F32)
        okv = jnp.zeros((gqa, V7X_LANES), jnp.int32)
        for t in range(min(N_SEL, n_sel_blocks)):
            mx = jnp.max(score, axis=1, keepdims=True)
            am = jnp.min(jnp.where(score == mx, j_f, float(nj)), axis=1, keepdims=True)
            idx = jnp.where(lane == t, am, idx)
            okv = jnp.where(lane == t, jnp.where(mx > 0.5 * NEG, 1, 0), okv)
            score = jnp.where(j_f == am, -jnp.inf, score)
        idx_ref[h] = idx.astype(jnp.int32)
        ok_ref[h] = okv


def _nsa_sample_cmp(q, slopes_col, sums, past):
    bh, gqa, hd = q.shape
    fk = sums[0]
    bd, nch, n_kv, _ = fk.shape
    n_sel_blocks = past // L_SEL + 1
    nj = -(-n_sel_blocks // V7X_LANES) * V7X_LANES
    n_io = np.arange(nch)[:, None]
    j_io = np.arange(nj)[None, :]
    cover = ((n_io * D_CMP < j_io * L_SEL + L_SEL) & (n_io * D_CMP + L_CMP > j_io * L_SEL)
             & (n_io < nch - 1) & (j_io < n_sel_blocks)).astype(np.float32)
    hspec = pl.BlockSpec((n_kv, gqa, hd), lambda b: (b, 0, 0))
    sspec = pl.BlockSpec((None, nch, n_kv, hd), lambda b: (b, 0, 0, 0))
    ispec = pl.BlockSpec((n_kv, gqa, V7X_LANES), lambda b: (b, 0, 0))
    return pl.pallas_call(
        functools.partial(_nsa_sample_cmp_kernel, past=past, n_sel_blocks=n_sel_blocks),
        grid=(bd,),
        in_specs=[hspec, pl.BlockSpec((n_kv, gqa, 1), lambda b: (0, 0, 0)), sspec, sspec, sspec, sspec,
                  pl.BlockSpec((nch, nj), lambda b: (0, 0))],
        out_specs=[hspec, ispec, ispec],
        out_shape=[jax.ShapeDtypeStruct((bh, gqa, hd), F32),
                   jax.ShapeDtypeStruct((bh, gqa, V7X_LANES), jnp.int32),
                   jax.ShapeDtypeStruct((bh, gqa, V7X_LANES), jnp.int32)],
        compiler_params=_params(("arbitrary",), 48 << 20),
        name="nsa_sample_cmp",
    )(q, slopes_col, *sums, jnp.asarray(cover))


def _nsa_sample_sel_kernel(pt_ref, idx_ref, okb_ref, *refs, past, nsb):
    k_blks = refs[:nsb]
    v_blks = refs[nsb:2 * nsb]
    (q_ref, sl_ref, gt_ref, oc_ref, ksn_ref, vsn_ref, kwc_ref, vwc_ref, kwn_ref, vwn_ref, o_ref) = refs[2 * nsb:]
    b = pl.program_id(0)
    h = pl.program_id(1)
    gqa, hd = q_ref.shape
    base = (b * N_KV + h) * nsb
    nb_past = past // L_SEL
    q_pos = past
    q = q_ref[...]
    slope = sl_ref[...] * LOG2E
    head = lambda ref: ref[:, pl.ds(h, 1), :].reshape(ref.shape[0], hd)

    row_io = lax.broadcasted_iota(jnp.int32, (L_SEL, hd), 0)
    new_k = jnp.where(row_io == 0, jnp.broadcast_to(ksn_ref[...], (L_SEL, hd)), 0.0)
    new_v = jnp.where(row_io == 0, jnp.broadcast_to(vsn_ref[...], (L_SEL, hd)), 0.0)
    lane = lax.broadcasted_iota(jnp.int32, (gqa, nsb * L_SEL), 1)
    kpos = lane & (L_SEL - 1)
    okl = jnp.zeros((gqa, nsb * L_SEL), jnp.int32)
    kg, vg = [], []
    for t in range(nsb):
        bid = idx_ref[base + t]
        is_new = bid >= nb_past
        kg.append(jnp.where(is_new, new_k, head(k_blks[t])).astype(BF16))
        vg.append(jnp.where(is_new, new_v, head(v_blks[t])).astype(BF16))
        in_t = (lane >> SEL_SHIFT) == t
        kpos = kpos + jnp.where(in_t, bid * L_SEL, 0)
        okl = okl + jnp.where(in_t, okb_ref[base + t], 0)
    kg = jnp.concatenate(kg, axis=0)
    vg = jnp.concatenate(vg, axis=0)
    dist = (q_pos - kpos).astype(F32)
    ok = (dist >= 0.0) & (okl > 0)
    s = jnp.where(ok, _nt_dot(q, kg) - slope * dist, NEG)
    p = jnp.where(ok, jnp.exp2(s - jnp.max(s, axis=1, keepdims=True)), 0.0)
    p = p / jnp.maximum(jnp.sum(p, axis=1, keepdims=True), 1e-30)
    o_s = jnp.dot(p.astype(BF16), vg, preferred_element_type=F32)

    wbuf = kwc_ref.shape[0]
    i_io = lax.broadcasted_iota(jnp.int32, (gqa, wbuf), 1)
    kpos_w = past - wbuf + i_io
    dist_w = (q_pos - kpos_w).astype(F32)
    ok_w = (dist_w >= 0.0) & (dist_w <= float(WINDOW)) & (kpos_w >= 0)
    s_w = jnp.where(ok_w, _nt_dot(q, head(kwc_ref).astype(BF16)) - slope * dist_w, NEG)
    kn = jnp.broadcast_to(kwn_ref[...], (V7X_SUBLANES, hd)).astype(BF16)
    s_n = _nt_dot(q, kn)[:, 0:1]
    m = jnp.maximum(jnp.max(s_w, axis=1, keepdims=True), s_n)
    p_w = jnp.where(ok_w, jnp.exp2(s_w - m), 0.0)
    p_n = jnp.exp2(s_n - m)
    den = jnp.maximum(jnp.sum(p_w, axis=1, keepdims=True) + p_n, 1e-30)
    p_w = p_w / den
    p_n = p_n / den
    o_w = (jnp.dot(p_w.astype(BF16), head(vwc_ref).astype(BF16), preferred_element_type=F32)
           + p_n.astype(BF16).astype(F32) * vwn_ref[...].astype(BF16).astype(F32))

    gt = gt_ref[...]
    o_ref[...] = (gt[:, 0:1] * oc_ref[...] + gt[:, 1:2] * o_s + gt[:, 2:3] * o_w).astype(BF16)


def _nsa_sample_sel(page_table, idx_flat, ok_flat, pool_k, pool_v, q, slopes_col, gates, o_c,
                    ks_new, vs_new, cache_kw, cache_vw, kw_new, vw_new, past):
    bh, gqa, hd = q.shape
    bd = bh // N_KV
    nsb = idx_flat.shape[0] // bh
    c = N_KV * hd
    halves = PAGE_SIZE // L_SEL
    nb_past = past // L_SEL
    wbuf = cache_kw.shape[1]

    def blk_spec(t):
        def imap(b, h, pt, idx, okb):
            bid = jnp.minimum(idx[(b * N_KV + h) * nsb + t], nb_past - 1)
            return (pt[b, bid // halves], bid % halves, 0, 0)
        return pl.BlockSpec((None, L_SEL, N_KV, hd), imap)
    hspec = pl.BlockSpec((None, gqa, hd), lambda b, h, *_: (b * N_KV + h, 0, 0))
    nspec = pl.BlockSpec((None, 1, hd), lambda b, h, *_: (b, 0, h))
    wspec = pl.BlockSpec((None, wbuf, N_KV, hd), lambda b, h, *_: (b, 0, 0, 0))
    return pl.pallas_call(
        functools.partial(_nsa_sample_sel_kernel, past=past, nsb=nsb),
        grid_spec=pltpu.PrefetchScalarGridSpec(
            num_scalar_prefetch=3, grid=(bd, N_KV),
            in_specs=[blk_spec(t) for t in range(nsb)] * 2
            + [hspec, pl.BlockSpec((None, gqa, 1), lambda b, h, *_: (h, 0, 0)),
               pl.BlockSpec((None, gqa, 3), lambda b, h, *_: (b * N_KV + h, 0, 0)), hspec,
               nspec, nspec, wspec, wspec, nspec, nspec],
            out_specs=hspec),
        out_shape=jax.ShapeDtypeStruct((bh, gqa, hd), BF16),
        compiler_params=_params(("arbitrary", "arbitrary"), 40 << 20),
        name="nsa_sample_sel",
    )(page_table, idx_flat, ok_flat, *([pool_k] * nsb), *([pool_v] * nsb), q, slopes_col, gates, o_c,
      ks_new.reshape(bd, 1, c), vs_new.reshape(bd, 1, c), cache_kw, cache_vw,
      kw_new.reshape(bd, 1, c), vw_new.reshape(bd, 1, c))


TM_MM = 1024
TM_ROWS = 256
TQ = 256


def kernel(x_prompt, x_sample, c_prompt, c_sample, page_table, cache_k_cmp, cache_v_cmp, cache_k_sel, cache_v_sel, cache_k_win, cache_v_win, state_h, state_conv, w_ada, b_ada, ln_g, ln_b, a_w_in, a_conv_w, a_conv_b, a_w_r, a_b_r, a_w_i, a_b_i, a_lambda, a_w_out, w_kv, w_cmp_k, w_cmp_v, b_w_qg, b_w_o, f_w_up, f_w_down):
    nb, s_len, d = x_prompt.shape
    bd, sd, _ = x_sample.shape
    assert sd == 1
    depth = w_ada.shape[0]
    n_a = a_w_in.shape[0]
    assert depth - n_a == 1 or depth == n_a, "one shared-KV NSA layer stack"
    d_ff = f_w_down.shape[1]
    lru = a_w_out.shape[1]
    n_heads = d // HEAD_DIM
    gqa = n_heads // N_KV
    ckv = N_KV * HEAD_DIM
    past = page_table.shape[1] * PAGE_SIZE
    alpha = (2.0 * depth) ** 0.25
    t = nb * s_len
    tm_mm = min(TM_MM, s_len)
    tm_rows = min(TM_ROWS, s_len)
    tn = 512

    rows = -(-(nb + bd) // V7X_SUBLANES) * V7X_SUBLANES
    c_all = jnp.concatenate([c_prompt, c_sample, jnp.zeros((rows - nb - bd, d), F32)], axis=0)
    mods = {0: _ada(c_all, w_ada, b_ada, 0)}

    def mod_p(l, k):
        return mods[l][:nb, k * d:(k + 1) * d].reshape(nb, 1, d)

    def mod_s(l, k):
        return mods[l][nb:nb + bd, k * d:(k + 1) * d].reshape(1, bd, d)

    slopes = jnp.exp2(-8.0 * jnp.arange(1, n_heads + 1, dtype=F32) / n_heads)
    slopes_col = slopes.reshape(N_KV, gqa, 1)

    xp = x_prompt.reshape(t, d)
    xs = x_sample.reshape(bd, d)
    up = _modulate(xp, mod_p(0, 0), mod_p(0, 1), tm_rows)
    us = _modulate(xs, mod_s(0, 0), mod_s(0, 1), bd)

    def ffn(x_p, x_s, u_p, u_s, l, nxt, w_down):
        hmid = _matmul(u_p, u_s, f_w_up, d_ff, layer=l, tm=min(2 * tm_mm, s_len), tn=256, out_dtype=BF16,
                       epilogue=_swiglu_epilogue, col_off=0, col_off2=d_ff, name="ffn_up")
        ys = _matmul(hmid[0], hmid[1], w_down, d, tm=min(tm_mm, 512), tn=tn, out_dtype=F32,
                     ada=None if nxt is None else (c_all, w_ada, b_ada, nxt[0]), name="ffn_down")
        if nxt is not None:
            mods[nxt[0]] = ys[2]
        outs = []
        for x, y, tm_r, mod in ((x_p, ys[0], tm_rows, mod_p), (x_s, ys[1], bd, mod_s)):
            if nxt is None:
                outs.append(_ln_mod(x, y, mod(l, 5), ln_g[l, 1], ln_b[l, 1], alpha, tm_r))
            else:
                outs.append(_ln_mod(x, y, mod(l, 5), ln_g[l, 1], ln_b[l, 1], alpha, tm_r,
                                    shift=mod(nxt[0], 0), scale=mod(nxt[0], 1), want_xb=nxt[1]))
        return outs

    h_p, h_s, cb_p, cb_s = [], [], [], []
    kv_p = kv_s = None
    for l in range(depth):
        if l < n_a:
            wr = a_w_r[l].astype(BF16)
            wi = a_w_i[l].astype(BF16)
            ggx_p, ggx_s = _matmul(up, us, a_w_in, lru, layer=l, tm=tm_mm, tn=tn, out_dtype=BF16,
                                   epilogue=_gelu_tanh, name="lru_in_g")
            rx_p, rx_s = _matmul(up, us, a_w_in, lru, layer=l, tm=tm_mm, tn=tn, out_dtype=F32, col_off=lru,
                                 name="lru_in_r")
            yin_p, hl_p, ct_p, w_down = _rglru_prompt(
                rx_p, ggx_p, jnp.zeros((nb, CONV_W - 1, lru), F32), jnp.zeros((nb, lru), F32),
                a_conv_w[l], a_conv_b[l], wr, a_b_r[l], wi, a_b_i[l], a_lambda[l], nb, min(TM_ROWS, s_len),
                f_w_down, l)
            yin_s, hl_s, ct_s = _rglru_step(rx_s, ggx_s, jnp.swapaxes(state_conv[l], 0, 1), state_h[l],
                                            a_conv_w[l], a_conv_b[l], wr, a_b_r[l], wi, a_b_i[l], a_lambda[l])
            h_p.append(hl_p.reshape(nb, lru))
            h_s.append(hl_s)
            cb_p.append(ct_p)
            cb_s.append(jnp.swapaxes(ct_s, 0, 1))
            y_p, y_s = _matmul(yin_p, yin_s, a_w_out, d, layer=l, tm=tm_mm, tn=tn, out_dtype=F32, name="lru_out")
        else:
            j = l - n_a
            w_g = b_w_qg[j][:, n_heads * HEAD_DIM:].reshape(d, N_KV, 3 * gqa)
            w_g = jnp.pad(w_g, ((0, 0), (0, 0), (0, V7X_LANES - 3 * gqa))).reshape(d, N_KV * V7X_LANES)
            w_q = b_w_qg[j, :, :n_heads * HEAD_DIM].astype(BF16)
            q_p, q_s = _matmul(up, us, w_q, d, tm=tm_mm, tn=tn, out_dtype=BF16,
                               epilogue=lambda a: a * EXP2_SCALE, name="nsa_q")
            g_p, g_s = _matmul(up, us, w_g, N_KV * V7X_LANES, tm=tm_mm, tn=N_KV * V7X_LANES, out_dtype=F32,
                               epilogue=jax.nn.sigmoid, name="nsa_gate")
            o_p, w_down = _nsa_prompt(q_p, g_p, ctx_p[0], ctx_p[1], kv_p[2], kv_p[3], kv_p[4], kv_p[5], slopes, nb,
                                      min(TQ, s_len), f_w_down, l)
            q_s = q_s.reshape(bd * N_KV, gqa, HEAD_DIM)
            g_s = g_s.reshape(bd * N_KV, V7X_LANES)[:, :3 * gqa].reshape(bd * N_KV, gqa, 3)
            o_c, idx, okb = _nsa_sample_cmp(q_s, slopes_col, ctx_s, past)
            nsb = min(N_SEL, past // L_SEL + 1)
            o_s = _nsa_sample_sel(page_table, idx[:, 0, :nsb].reshape(-1), okb[:, 0, :nsb].reshape(-1),
                                  cache_k_sel, cache_v_sel, q_s, slopes_col, g_s, o_c,
                                  kv_s[2], kv_s[3], cache_k_win, cache_v_win, kv_s[4], kv_s[5], past)
            y_p, y_s = _matmul(o_p, o_s.reshape(bd, d), b_w_o, d, layer=j, tm=tm_mm, tn=tn, out_dtype=F32,
                               name="nsa_out")

        xp, up = _ln_mod(xp, y_p, mod_p(l, 2), ln_g[l, 0], ln_b[l, 0], alpha, tm_rows,
                         shift=mod_p(l, 3), scale=mod_p(l, 4))
        xs, us = _ln_mod(xs, y_s, mod_s(l, 2), ln_g[l, 0], ln_b[l, 0], alpha, bd,
                         shift=mod_s(l, 3), scale=mod_s(l, 4))
        last = l == depth - 1
        res_p, res_s = ffn(xp, xs, up, us, l, None if last else (l + 1, l == n_a - 1), w_down)
        if last:
            xp, xs = res_p[0], res_s[0]
        elif l == n_a - 1:
            xp, up, xbp = res_p
            xs, us, xbs = res_s
            kv = [_matmul(xbp, xbs, w_kv, ckv, tm=tm_mm, tn=ckv, out_dtype=F32, col_off=jj * ckv,
                          heads=(N_KV, HEAD_DIM), name="kv_proj") for jj in range(6)]
            kv_p, kv_s, kv_p4, kv_s4 = zip(*kv)
            ctx_p = _cmp_blocks_prompt(kv_p[0], kv_p[1], w_cmp_k, w_cmp_v, nb)
            ctx_s = _page_chunks(page_table, cache_k_cmp, cache_v_cmp, w_cmp_k, w_cmp_v)
        else:
            xp, up = res_p
            xs, us = res_s

    wwin = min(WINDOW, s_len)
    shp_p = (nb, s_len, N_KV, HEAD_DIM)
    shp_s = (bd, sd, N_KV, HEAD_DIM)
    kvp = [a.reshape(shp_p) for a in kv_p4]
    kvs = [a.reshape(shp_s) for a in kv_s4]
    wbuf = cache_k_win.shape[1]
    keep = wbuf - sd
    k_win_s = jnp.concatenate([cache_k_win[:, wbuf - keep:], kvs[4]], axis=1)
    v_win_s = jnp.concatenate([cache_v_win[:, wbuf - keep:], kvs[5]], axis=1)
    return (xp.reshape(nb, s_len, d), xs.reshape(bd, sd, d),
            kvp[0], kvs[0], kvp[1], kvs[1], kvp[2], kvs[2], kvp[3], kvs[3],
            kvp[4][:, -wwin:], k_win_s, kvp[5][:, -wwin:], v_win_s,
            jnp.stack(h_p), jnp.stack(h_s), jnp.stack(cb_p), jnp.stack(cb_s))
```
